```python
import jax, jax.numpy as jnp
from jax import lax
import numpy as np

D_MODEL = 1024
BATCH = 4
SEQ = 4096
DEPTH = 2
DEC_BATCH = 32
DEC_SEQ = 8
PAST_LEN = 8192
PAGE_SIZE = 128

HEAD_DIM = 64
C_A = D_MODEL // 2
H_B = (D_MODEL // 2) // HEAD_DIM
C_B = H_B * HEAD_DIM
CONV_W = 31
H_C = D_MODEL // HEAD_DIM
KV_C = H_C // 4
H_IDX = 8
D_IDX = 64
TOPK_MAX = 256
Q_BLOCK = 128
N_GROUPS = 4
E_PER_GROUP = 4
N_EXPERTS = N_GROUPS * E_PER_GROUP
D_FF_E = D_MODEL // 2
TOP_K_INNER = 2
ROPE_THETA = 10000.0
RMS_EPS = 1e-6
LN_EPS = 1e-5
N_EVEN = (DEPTH + 1) // 2
N_ODD = DEPTH // 2
IN_EVEN = 2 * C_A + 3 * C_B + H_B
IN_ODD = H_C * HEAD_DIM + 2 * KV_C * HEAD_DIM + H_IDX * D_IDX + D_IDX + H_IDX

kernel_name = 'hybrid_conv_fox_dsa_hmoe_step'


def rms_norm(x, g):
    xf = x.astype(jnp.float32)
    y = xf * lax.rsqrt(jnp.mean(xf * xf, axis=-1, keepdims=True) + RMS_EPS)
    return (y * g.astype(jnp.float32)).astype(x.dtype)


def rope(x, pos):
    half = x.shape[-1] // 2
    inv = ROPE_THETA ** (-jnp.arange(half, dtype=jnp.float32) / half)
    ang = pos.astype(jnp.float32)[:, None] * inv[None, :]
    cos = jnp.cos(ang)[None, :, None, :]
    sin = jnp.sin(ang)[None, :, None, :]
    xf = x.astype(jnp.float32)
    x1, x2 = xf[..., :half], xf[..., half:]
    return jnp.concatenate([x1 * cos - x2 * sin, x2 * cos + x1 * sin], axis=-1).astype(x.dtype)


def _query_blocks(tq):
    qb = min(Q_BLOCK, tq)
    return qb, -(-tq // qb)


def _to_blocks(a, qb, nb):
    pad = nb * qb - a.shape[1]
    a = jnp.pad(a, [(0, 0), (0, pad)] + [(0, 0)] * (a.ndim - 2), mode='edge')
    a = a.reshape((a.shape[0], nb, qb) + a.shape[2:])
    return jnp.moveaxis(a, 1, 0)


def _pos_blocks(pos, qb, nb):
    return jnp.pad(pos, (0, nb * qb - pos.shape[0]), mode='edge').reshape(nb, qb)


def _from_blocks(o, tq):
    o = jnp.moveaxis(o, 0, 1)
    o = o.reshape((o.shape[0], o.shape[1] * o.shape[2]) + o.shape[3:])
    return o[:, :tq]


def fox_attention(q, k, v, c_q, c_k, pos_q, pos_k):
    tq = q.shape[1]
    qb, nb = _query_blocks(tq)
    ck_t = jnp.swapaxes(c_k, 1, 2)
    scale = q.shape[-1] ** -0.5

    def block(args):
        q_blk, cq_blk, pq_blk = args
        s = jnp.einsum('nqhd,nkhd->nhqk', q_blk, k, preferred_element_type=jnp.float32) * scale
        s = s + jnp.swapaxes(cq_blk, 1, 2)[..., None] - ck_t[:, :, None, :]
        s = jnp.where(pos_k[None, :] <= pq_blk[:, None], s, -jnp.inf)
        p = jax.nn.softmax(s, axis=-1)
        return jnp.einsum('nhqk,nkhd->nqhd', p.astype(v.dtype), v)

    out = lax.map(block, (_to_blocks(q, qb, nb), _to_blocks(c_q, qb, nb), _pos_blocks(pos_q, qb, nb)))
    return _from_blocks(out, tq)


def dsa_attention(q, k, v, q_idx, k_idx, w_idx, pos_q, pos_k):
    n, tq, _, d = q.shape
    n_sel = min(TOPK_MAX, k.shape[1] // 4)
    qb, nb = _query_blocks(tq)
    scale = d ** -0.5
    idx_scale = D_IDX ** -0.5

    def block(args):
        q_blk, qi_blk, wi_blk, pq_blk = args
        dots = jnp.einsum('nqhe,nke->nqhk', qi_blk, k_idx, preferred_element_type=jnp.float32) * idx_scale
        score = jnp.einsum('nqhk,nqh->nqk', jax.nn.relu(dots), wi_blk.astype(jnp.float32))
        score = jnp.where((pos_k[None, :] <= pq_blk[:, None])[None], score, -jnp.inf)
        _, sel = lax.top_k(score, n_sel)
        valid = pos_k[sel] <= pq_blk[None, :, None]
        k_sel = jax.vmap(lambda kk, ii: kk[ii])(k, sel)
        v_sel = jax.vmap(lambda vv, ii: vv[ii])(v, sel)
        qg = q_blk.reshape(n, qb, KV_C, H_C // KV_C, d)
        s = jnp.einsum('nqgrd,nqkgd->nqgrk', qg, k_sel, preferred_element_type=jnp.float32) * scale
        s = jnp.where(valid[:, :, None, None, :], s, -jnp.inf)
        p = jax.nn.softmax(s, axis=-1)
        o = jnp.einsum('nqgrk,nqkgd->nqgrd', p.astype(v.dtype), v_sel)
        return o.reshape(n, qb, H_C, d)

    out = lax.map(block, (_to_blocks(q, qb, nb), _to_blocks(q_idx, qb, nb),
                          _to_blocks(w_idx, qb, nb), _pos_blocks(pos_q, qb, nb)))
    return _from_blocks(out, tq)


def even_mixer(h, conv_state, past_k, past_v, past_logf, w_in, b_forget, w_dw, b_dw, ln_g, ln_b, w_out):
    n, t, _ = h.shape
    p = past_k.shape[1]
    proj = jnp.einsum('ntd,de->nte', h, w_in)
    a_val, a_gate, q, k, v, f_logit = jnp.split(
        proj, [C_A, 2 * C_A, 2 * C_A + C_B, 2 * C_A + 2 * C_B, 2 * C_A + 3 * C_B], axis=-1)
    u = a_val * jax.nn.sigmoid(a_gate)
    buf = jnp.concatenate([conv_state.astype(u.dtype), u], axis=1)
    conv = lax.conv_general_dilated(buf, w_dw.astype(u.dtype)[:, None, :], (1,), 'VALID',
                                    dimension_numbers=('NWC', 'WIO', 'NWC'),
                                    feature_group_count=C_A) + b_dw
    cf = conv.astype(jnp.float32)
    mu = jnp.mean(cf, axis=-1, keepdims=True)
    var = jnp.mean(jnp.square(cf - mu), axis=-1, keepdims=True)
    cn = (cf - mu) * lax.rsqrt(var + LN_EPS) * ln_g.astype(jnp.float32) + ln_b.astype(jnp.float32)
    a_out = jax.nn.silu(cn).astype(h.dtype)
    new_conv = buf[:, -(CONV_W - 1):]
    q = q.reshape(n, t, H_B, HEAD_DIM)
    k = k.reshape(n, t, H_B, HEAD_DIM)
    v = v.reshape(n, t, H_B, HEAD_DIM)
    logf = jax.nn.log_sigmoid(f_logit.astype(jnp.float32) + b_forget.astype(jnp.float32))
    k_all = jnp.concatenate([past_k.astype(k.dtype), k], axis=1)
    v_all = jnp.concatenate([past_v.astype(v.dtype), v], axis=1)
    c_all = jnp.cumsum(jnp.concatenate([past_logf.astype(jnp.float32), logf], axis=1), axis=1)
    pos_k = jnp.arange(p + t)
    pos_q = p + jnp.arange(t)
    b_out = fox_attention(q, k_all, v_all, c_all[:, p:], c_all, pos_q, pos_k)
    mixed = jnp.concatenate([a_out, b_out.reshape(n, t, C_B).astype(h.dtype)], axis=-1)
    return jnp.einsum('nte,ed->ntd', mixed, w_out), new_conv, k, v, logf


def odd_mixer(h, past_k, past_v, past_kidx, w_in, w_out):
    n, t, _ = h.shape
    p = past_k.shape[1]
    s1 = H_C * HEAD_DIM
    s2 = s1 + KV_C * HEAD_DIM
    s3 = s2 + KV_C * HEAD_DIM
    s4 = s3 + H_IDX * D_IDX
    s5 = s4 + D_IDX
    proj = jnp.einsum('ntd,de->nte', h, w_in)
    q, k, v, qi, ki, wi = jnp.split(proj, [s1, s2, s3, s4, s5], axis=-1)
    pos_q = p + jnp.arange(t)
    q = rope(q.reshape(n, t, H_C, HEAD_DIM), pos_q)
    k = rope(k.reshape(n, t, KV_C, HEAD_DIM), pos_q)
    v = v.reshape(n, t, KV_C, HEAD_DIM)
    qi = rope(qi.reshape(n, t, H_IDX, D_IDX), pos_q)
    ki = rope(ki[:, :, None, :], pos_q)[:, :, 0]
    k_all = jnp.concatenate([past_k.astype(k.dtype), k], axis=1)
    v_all = jnp.concatenate([past_v.astype(v.dtype), v], axis=1)
    ki_all = jnp.concatenate([past_kidx.astype(ki.dtype), ki], axis=1)
    pos_k = jnp.arange(p + t)
    o = dsa_attention(q, k_all, v_all, qi, ki_all, wi * (H_IDX ** -0.5), pos_q, pos_k)
    return jnp.einsum('nte,ed->ntd', o.reshape(n, t, H_C * HEAD_DIM), w_out), k, v, ki


def hier_moe(h, w_group, b_group, w_router, b_router, w_gate, w_up, w_down):
    n, t, d = h.shape
    m = n * t
    x2 = h.reshape(m, d)
    g_logits = jnp.einsum('md,dg->mg', x2, w_group, preferred_element_type=jnp.float32) + b_group.astype(jnp.float32)
    g_w, g_idx = lax.top_k(jax.nn.softmax(g_logits, axis=-1), 1)
    e_logits = (jnp.einsum('md,de->me', x2, w_router, preferred_element_type=jnp.float32)
                + b_router.astype(jnp.float32)).reshape(m, N_GROUPS, E_PER_GROUP)
    e_in = e_logits[jnp.arange(m), g_idx[:, 0]]
    top_v, top_i = lax.top_k(e_in, TOP_K_INNER)
    top_w = jax.nn.softmax(top_v, axis=-1) * g_w
    expert_id = g_idx * E_PER_GROUP + top_i
    combine = jnp.einsum('mk,mke->me', top_w, jax.nn.one_hot(expert_id, N_EXPERTS, dtype=jnp.float32))
    y = jnp.zeros((m, d), jnp.float32)
    for e in range(N_EXPERTS):
        hid = jax.nn.silu(x2 @ w_gate[e]) * (x2 @ w_up[e])
        y = y + combine[:, e:e + 1] * (hid @ w_down[e]).astype(jnp.float32)
    return y.astype(h.dtype).reshape(n, t, d)


def setup_inputs(seed: int = 0) -> dict:
    key = jax.random.key(seed)
    keys = jax.random.split(key, 48)
    counter = [0]

    def nxt():
        counter[0] += 1
        return keys[counter[0] - 1]

    def nrm(shape, scale=1.0):
        return jax.random.normal(nxt(), shape, jnp.float32) * scale

    n_pages = PAST_LEN // PAGE_SIZE
    n_used = DEC_BATCH * n_pages
    n_pool = n_used + max(1, n_used // 4)
    inp = {}
    inp['x_prompt'] = nrm((BATCH, SEQ, D_MODEL))
    inp['x_sample'] = nrm((DEC_BATCH, DEC_SEQ, D_MODEL))
    inp['state_conv'] = nrm((N_EVEN, DEC_BATCH, CONV_W - 1, C_A), 0.5)
    inp['cache_fox_k'] = nrm((N_EVEN, n_pool, PAGE_SIZE, H_B, HEAD_DIM))
    inp['cache_fox_v'] = nrm((N_EVEN, n_pool, PAGE_SIZE, H_B, HEAD_DIM))
    inp['cache_fox_logf'] = jax.nn.log_sigmoid(4.0 + nrm((N_EVEN, n_pool, PAGE_SIZE, H_B)))
    inp['cache_dsa_k'] = nrm((N_ODD, n_pool, PAGE_SIZE, KV_C, HEAD_DIM))
    inp['cache_dsa_v'] = nrm((N_ODD, n_pool, PAGE_SIZE, KV_C, HEAD_DIM))
    inp['cache_dsa_kidx'] = nrm((N_ODD, n_pool, PAGE_SIZE, D_IDX))
    inp['page_table'] = jax.random.permutation(nxt(), n_pool)[:n_used].reshape(DEC_BATCH, n_pages).astype(jnp.int32)
    inp['norm_mix_even'] = 1.0 + nrm((N_EVEN, D_MODEL), 0.02)
    inp['w_in_even'] = nrm((N_EVEN, D_MODEL, IN_EVEN), D_MODEL ** -0.5)
    inp['b_forget'] = jax.random.uniform(nxt(), (N_EVEN, H_B), jnp.float32, minval=2.0, maxval=6.0)
    inp['w_dw'] = nrm((N_EVEN, CONV_W, C_A), CONV_W ** -0.5)
    inp['b_dw'] = nrm((N_EVEN, C_A), 0.01)
    inp['ln_conv_g'] = 1.0 + nrm((N_EVEN, C_A), 0.02)
    inp['ln_conv_b'] = nrm((N_EVEN, C_A), 0.01)
    inp['w_out_even'] = nrm((N_EVEN, C_A + C_B, D_MODEL), (C_A + C_B) ** -0.5)
    inp['norm_mix_odd'] = 1.0 + nrm((N_ODD, D_MODEL), 0.02)
    inp['w_in_odd'] = nrm((N_ODD, D_MODEL, IN_ODD), D_MODEL ** -0.5)
    inp['w_out_odd'] = nrm((N_ODD, H_C * HEAD_DIM, D_MODEL), (H_C * HEAD_DIM) ** -0.5)
    inp['norm_ffn'] = 1.0 + nrm((DEPTH, D_MODEL), 0.02)
    inp['w_group'] = nrm((DEPTH, D_MODEL, N_GROUPS), D_MODEL ** -0.5)
    inp['b_group'] = nrm((DEPTH, N_GROUPS), 0.01)
    inp['w_router'] = nrm((DEPTH, D_MODEL, N_EXPERTS), D_MODEL ** -0.5)
    inp['b_router'] = nrm((DEPTH, N_EXPERTS), 0.01)
    inp['w_gate'] = nrm((DEPTH, N_EXPERTS, D_MODEL, D_FF_E), D_MODEL ** -0.5)
    inp['w_up'] = nrm((DEPTH, N_EXPERTS, D_MODEL, D_FF_E), D_MODEL ** -0.5)
    inp['w_down'] = nrm((DEPTH, N_EXPERTS, D_FF_E, D_MODEL), D_FF_E ** -0.5)
    inp['norm_final'] = 1.0 + nrm((D_MODEL,), 0.02)
    return inp


def reference(x_prompt, x_sample, state_conv, cache_fox_k, cache_fox_v, cache_fox_logf,
              cache_dsa_k, cache_dsa_v, cache_dsa_kidx, page_table,
              norm_mix_even, w_in_even, b_forget, w_dw, b_dw, ln_conv_g, ln_conv_b, w_out_even,
              norm_mix_odd, w_in_odd, w_out_odd,
              norm_ffn, w_group, b_group, w_router, b_router, w_gate, w_up, w_down, norm_final):
    n_p = x_prompt.shape[0]
    n_s = x_sample.shape[0]
    dt_p = x_prompt.dtype

    def paged_rows(pool):
        rows = pool[page_table]
        return rows.reshape((n_s, rows.shape[1] * rows.shape[2]) + rows.shape[3:])

    hp, hs = x_prompt, x_sample
    conv_p, conv_s, fk_p, fk_s, fv_p, fv_s, fl_p, fl_s = [], [], [], [], [], [], [], []
    dk_p, dk_s, dv_p, dv_s, di_p, di_s = [], [], [], [], [], []
    for layer in range(DEPTH):
        i = layer // 2
        if layer % 2 == 0:
            w = (w_in_even[i], b_forget[i], w_dw[i], b_dw[i], ln_conv_g[i], ln_conv_b[i], w_out_even[i])
            yp, cp, kp, vp, lp = even_mixer(
                rms_norm(hp, norm_mix_even[i]), jnp.zeros((n_p, CONV_W - 1, C_A), dt_p),
                jnp.zeros((n_p, 0, H_B, HEAD_DIM), dt_p), jnp.zeros((n_p, 0, H_B, HEAD_DIM), dt_p),
                jnp.zeros((n_p, 0, H_B), jnp.float32), *w)
            ys, cs, ks, vs, ls = even_mixer(
                rms_norm(hs, norm_mix_even[i]), state_conv[i], paged_rows(cache_fox_k[i]),
                paged_rows(cache_fox_v[i]), paged_rows(cache_fox_logf[i]), *w)
            conv_p.append(cp); conv_s.append(cs)
            fk_p.append(kp); fk_s.append(ks)
            fv_p.append(vp); fv_s.append(vs)
            fl_p.append(lp); fl_s.append(ls)
        else:
            yp, kp, vp, ip = odd_mixer(
                rms_norm(hp, norm_mix_odd[i]), jnp.zeros((n_p, 0, KV_C, HEAD_DIM), dt_p),
                jnp.zeros((n_p, 0, KV_C, HEAD_DIM), dt_p), jnp.zeros((n_p, 0, D_IDX), dt_p),
                w_in_odd[i], w_out_odd[i])
            ys, ks, vs, is_ = odd_mixer(
                rms_norm(hs, norm_mix_odd[i]), paged_rows(cache_dsa_k[i]), paged_rows(cache_dsa_v[i]),
                paged_rows(cache_dsa_kidx[i]), w_in_odd[i], w_out_odd[i])
            dk_p.append(kp); dk_s.append(ks)
            dv_p.append(vp); dv_s.append(vs)
            di_p.append(ip); di_s.append(is_)
        hp = hp + yp
        hs = hs + ys
        moe_w = (w_group[layer], b_group[layer], w_router[layer], b_router[layer],
                 w_gate[layer], w_up[layer], w_down[layer])
        hp = hp + hier_moe(rms_norm(hp, norm_ffn[layer]), *moe_w)
        hs = hs + hier_moe(rms_norm(hs, norm_ffn[layer]), *moe_w)
    y_prompt = rms_norm(hp, norm_final)
    y_sample = rms_norm(hs, norm_final)
    conv_state_prompt = jnp.stack(conv_p)
    conv_state_sample = jnp.stack(conv_s)
    fox_k_prompt = jnp.stack(fk_p)
    fox_k_sample = jnp.stack(fk_s)
    fox_v_prompt = jnp.stack(fv_p)
    fox_v_sample = jnp.stack(fv_s)
    fox_logf_prompt = jnp.stack(fl_p)
    fox_logf_sample = jnp.stack(fl_s)
    dsa_k_prompt = jnp.stack(dk_p)
    dsa_k_sample = jnp.stack(dk_s)
    dsa_v_prompt = jnp.stack(dv_p)
    dsa_v_sample = jnp.stack(dv_s)
    dsa_kidx_prompt = jnp.stack(di_p)
    dsa_kidx_sample = jnp.stack(di_s)
    return (y_prompt, y_sample, conv_state_prompt, conv_state_sample, fox_k_prompt, fox_k_sample,
            fox_v_prompt, fox_v_sample, fox_logf_prompt, fox_logf_sample, dsa_k_prompt, dsa_k_sample,
            dsa_v_prompt, dsa_v_sample, dsa_kidx_prompt, dsa_kidx_sample)
```

```python
import functools

import jax
import jax.numpy as jnp
from jax import lax
from jax.experimental import pallas as pl
from jax.experimental.pallas import tpu as pltpu

F32 = jnp.float32
BF16 = jnp.bfloat16

HEAD_DIM = 64
CONV_W = 31
H_IDX = 8
D_IDX = 64
TOPK_MAX = 256
N_GROUPS = 4
E_PER_GROUP = 4
N_EXPERTS = N_GROUPS * E_PER_GROUP
ROPE_THETA = 10000.0
RMS_EPS = 1e-6
LN_EPS = 1e-5

LANES = 128
SUBLANES = 8
VMEM_LIMIT = 56 * 1024 * 1024

MASK_VALUE = -1e30
HIST = CONV_W - 1
HIST_PAD = 32
BISECT_ITERS = 24
POS_BITS = 14

_NT = (((1,), (1,)), ((), ()))


def _cparams(sem):
    return pltpu.CompilerParams(dimension_semantics=sem, vmem_limit_bytes=VMEM_LIMIT)


def _rms(x, g):
    ms = jnp.mean(x * x, axis=-1, keepdims=True)
    return x * lax.rsqrt(ms + RMS_EPS) * g


def _rope128(xb, cos, sin_signed):
    lane = lax.broadcasted_iota(jnp.int32, xb.shape, 1)
    first_half = jnp.bitwise_and(lane, HEAD_DIM - 1) < (HEAD_DIM // 2)
    rot = jnp.where(first_half, pltpu.roll(xb, LANES - HEAD_DIM // 2, 1), pltpu.roll(xb, HEAD_DIM // 2, 1))
    return xb * cos + rot * sin_signed


def _proj_even_kernel(x_ref, g_ref, w_ref, bf_ref, ag_ref, q_ref, k_ref, v_ref, lf_ref, *, c_a, c_b):
    h = _rms(x_ref[...], g_ref[...]).astype(BF16)

    def seg(a, b):
        return jnp.dot(h, w_ref[:, a:b], preferred_element_type=F32)

    o = 2 * c_a
    ag_ref[...] = seg(0, o)
    q_ref[...] = seg(o, o + c_b) * (HEAD_DIM ** -0.5)
    k_ref[...] = seg(o + c_b, o + 2 * c_b)
    v_ref[...] = seg(o + 2 * c_b, o + 3 * c_b)
    z = seg(o + 3 * c_b, o + 3 * c_b + LANES) + bf_ref[...]
    lf_ref[...] = jnp.minimum(z, 0.0) - jnp.log1p(jnp.exp(-jnp.abs(z)))


def _proj_even(x, g, w, bf, c_a, c_b, tm):
    m, d = x.shape
    e = w.shape[1]
    row = lambda n: pl.BlockSpec((tm, n), lambda i: (i, 0))
    full = lambda a: pl.BlockSpec(a.shape, lambda i: (0, 0))
    return pl.pallas_call(
        functools.partial(_proj_even_kernel, c_a=c_a, c_b=c_b),
        grid=(m // tm,),
        in_specs=[row(d), full(g), full(w), full(bf)],
        out_specs=[row(2 * c_a), row(c_b), row(c_b), row(c_b), row(LANES)],
        out_shape=[jax.ShapeDtypeStruct((m, n), F32) for n in (2 * c_a, c_b, c_b, c_b, LANES)],
        compiler_params=_cparams(("parallel",)),
        name="proj_even",
    )(x, g, w, bf)


def _proj_odd_kernel(x_ref, g_ref, w_ref, cos_ref, sin_ref, q_ref, k_ref, v_ref, qi_ref, ki_ref, wi_ref,
                     *, nq, nkv, nqi, wi_scale):
    h = _rms(x_ref[...], g_ref[...]).astype(BF16)
    cos = cos_ref[...]
    sin = sin_ref[...]

    def seg(a, b):
        return jnp.dot(h, w_ref[:, a:b], preferred_element_type=F32)

    def roped(a, n, out_ref, scale=None):
        for c in range(n // LANES):
            blk = _rope128(seg(a + c * LANES, a + (c + 1) * LANES), cos, sin)
            out_ref[:, c * LANES:(c + 1) * LANES] = blk if scale is None else blk * scale

    o = 0
    roped(o, nq, q_ref, HEAD_DIM ** -0.5)
    o += nq
    roped(o, nkv, k_ref)
    o += nkv
    v_ref[...] = seg(o, o + nkv)
    o += nkv
    roped(o, nqi, qi_ref)
    o += nqi
    roped(o, LANES, ki_ref)
    o += LANES
    wi_ref[...] = seg(o, o + LANES) * wi_scale


def _proj_odd(x, g, w, cos, sin, nq, nkv, nqi, wi_scale, tm):
    m, d = x.shape
    nper = cos.shape[0] // tm
    row = lambda n: pl.BlockSpec((tm, n), lambda i: (i, 0))
    full = lambda a: pl.BlockSpec(a.shape, lambda i: (0, 0))
    tab = pl.BlockSpec((tm, LANES), lambda i: (i % nper, 0))
    widths = (nq, nkv, nkv, nqi, LANES, LANES)
    return pl.pallas_call(
        functools.partial(_proj_odd_kernel, nq=nq, nkv=nkv, nqi=nqi, wi_scale=wi_scale),
        grid=(m // tm,),
        in_specs=[row(d), full(g), full(w), tab, tab],
        out_specs=[row(n) for n in widths],
        out_shape=[jax.ShapeDtypeStruct((m, n), F32) for n in widths],
        compiler_params=_cparams(("parallel",)),
        name="proj_odd",
    )(x, g, w, cos, sin)


def _conv_kernel(ag_ref, st_ref, wdw_ref, bdw_ref, lng_ref, lnb_ref, out_ref, nst_ref, buf_ref, *, tt, c_a, rc):
    t = pl.program_id(1)
    lo = HIST_PAD - HIST

    @pl.when(t == 0)
    def _():
        buf_ref[lo:HIST_PAD, :] = st_ref[0]

    @pl.when(t > 0)
    def _():
        buf_ref[lo:HIST_PAD, :] = buf_ref[tt + lo:tt + HIST_PAD, :]

    ag = ag_ref[0]
    buf_ref[HIST_PAD:HIST_PAD + tt, :] = ag[:, :c_a] * jax.nn.sigmoid(ag[:, c_a:])

    bdw = bdw_ref[...]
    lng = lng_ref[...]
    lnb = lnb_ref[...]
    for r in range(tt // rc):
        acc = jnp.zeros((rc, c_a), F32) + bdw
        for j in range(CONV_W):
            s = r * rc + lo + j
            acc = acc + wdw_ref[j:j + 1, :] * buf_ref[s:s + rc, :]
        mu = jnp.mean(acc, axis=-1, keepdims=True)
        cen = acc - mu
        var = jnp.mean(cen * cen, axis=-1, keepdims=True)
        cn = cen * lax.rsqrt(var + LN_EPS) * lng + lnb
        out_ref[0, r * rc:(r + 1) * rc, :] = cn * jax.nn.sigmoid(cn)

    @pl.when(t == pl.num_programs(1) - 1)
    def _():
        nst_ref[0] = buf_ref[tt + lo:tt + HIST_PAD, :]


def _conv_module(ag, state, wdw, bdw, lng, lnb, tt):
    n, t, two_ca = ag.shape
    c_a = two_ca // 2
    rc = min(tt, 64)
    vec = lambda a: pl.BlockSpec(a.shape, lambda i, j: (0, 0))
    return pl.pallas_call(
        functools.partial(_conv_kernel, tt=tt, c_a=c_a, rc=rc),
        grid=(n, t // tt),
        in_specs=[pl.BlockSpec((1, tt, two_ca), lambda i, j: (i, j, 0)),
                  pl.BlockSpec((1, HIST, c_a), lambda i, j: (i, 0, 0)),
                  vec(wdw), vec(bdw), vec(lng), vec(lnb)],
        out_specs=[pl.BlockSpec((1, tt, c_a), lambda i, j: (i, j, 0)),
                   pl.BlockSpec((1, HIST, c_a), lambda i, j: (i, 0, 0))],
        out_shape=[jax.ShapeDtypeStruct((n, t, c_a), F32), jax.ShapeDtypeStruct((n, HIST, c_a), F32)],
        scratch_shapes=[pltpu.VMEM((HIST_PAD + tt, c_a), F32)],
        compiler_params=_cparams(("parallel", "arbitrary")),
        name="conv_module",
    )(ag, state, wdw, bdw, lng, lnb)


def _lane_cumsum(x):
    lane = lax.broadcasted_iota(jnp.int32, x.shape, 1)
    s = 1
    while s < LANES:
        x = x + jnp.where(lane >= s, pltpu.roll(x, s, 1), 0.0)
        s *= 2
    return x


def _cumsum_kernel(x_ref, o_ref, *, nb, carry):
    h = x_ref.shape[1]
    loc = _lane_cumsum(x_ref[...].reshape(nb * h, LANES)).reshape(nb, h, LANES)
    if not carry:
        o_ref[...] = loc
        return
    off = jnp.zeros((h, LANES), F32)
    for b in range(nb):
        cur = loc[b] + off
        o_ref[b] = cur
        off = jnp.broadcast_to(cur[:, LANES - 1:LANES], (h, LANES))


def _block_cumsum(x, nb, carry):
    b, h, _ = x.shape
    spec = pl.BlockSpec((nb, h, LANES), lambda i: (i, 0, 0))
    return pl.pallas_call(
        functools.partial(_cumsum_kernel, nb=nb, carry=carry),
        grid=(b // nb,),
        in_specs=[spec], out_specs=spec,
        out_shape=jax.ShapeDtypeStruct(x.shape, F32),
        compiler_params=_cparams(("parallel",)),
        name="block_cumsum",
    )(x)


def _fox_prompt_kernel(q_ref, k_ref, v_ref, cq_ref, ck_ref, o_ref, m_ref, l_ref, acc_ref, *, tq, tk):
    i = pl.program_id(2)
    j = pl.program_id(3)
    lane = lax.broadcasted_iota(jnp.int32, (tq, LANES), 1)
    first = lane < HEAD_DIM

    @pl.when(j == 0)
    def _():
        m_ref[...] = jnp.full(m_ref.shape, MASK_VALUE, F32)
        l_ref[...] = jnp.zeros(l_ref.shape, F32)
        acc_ref[...] = jnp.zeros(acc_ref.shape, F32)

    @pl.when(j * tk <= i * tq + tq - 1)
    def _():
        q = q_ref[0]
        k = k_ref[0].astype(BF16)
        v = v_ref[0].astype(BF16)
        rows = i * tq + lax.broadcasted_iota(jnp.int32, (tq, tk), 0)
        cols = j * tk + lax.broadcasted_iota(jnp.int32, (tq, tk), 1)
        causal = cols <= rows
        acc = acc_ref[...]
        for hh in range(2):
            half = first if hh == 0 else jnp.logical_not(first)
            qh = jnp.where(half, q, 0.0).astype(BF16)
            s = lax.dot_general(qh, k, _NT, preferred_element_type=F32)
            s = s + cq_ref[0, hh] - ck_ref[0, hh]
            s = jnp.where(causal, s, MASK_VALUE)
            m_old = m_ref[hh]
            m_new = jnp.maximum(m_old, jnp.max(s, axis=-1, keepdims=True))
            alpha = jnp.exp(m_old - m_new)
            p = jnp.exp(s - m_new)
            l_ref[hh] = alpha * l_ref[hh] + jnp.sum(p, axis=-1, keepdims=True)
            m_ref[hh] = m_new
            pv = jnp.dot(p.astype(BF16), v, preferred_element_type=F32)
            acc = jnp.where(half, acc * alpha + pv, acc)
        acc_ref[...] = acc

    @pl.when(j == pl.num_programs(3) - 1)
    def _():
        o_ref[0] = acc_ref[...] / jnp.where(first, l_ref[0], l_ref[1])


def _fox_prompt(q, k, v, c_col, c_row, tq, tk):
    n, t, cb = q.shape
    hp = cb // LANES
    kv_blk = lambda i, j: jnp.minimum(j, (i * tq + tq - 1) // tk)
    return pl.pallas_call(
        functools.partial(_fox_prompt_kernel, tq=tq, tk=tk),
        grid=(n, hp, t // tq, t // tk),
        in_specs=[pl.BlockSpec((1, tq, LANES), lambda b, h, i, j: (b, i, h)),
                  pl.BlockSpec((1, tk, LANES), lambda b, h, i, j: (b, kv_blk(i, j), h)),
                  pl.BlockSpec((1, tk, LANES), lambda b, h, i, j: (b, kv_blk(i, j), h)),
                  pl.BlockSpec((1, 2, tq, 1), lambda b, h, i, j: (b, h, i, 0)),
                  pl.BlockSpec((1, 2, 1, tk), lambda b, h, i, j: (b, h, 0, kv_blk(i, j)))],
        out_specs=pl.BlockSpec((1, tq, LANES), lambda b, h, i, j: (b, i, h)),
        out_shape=jax.ShapeDtypeStruct((n, t, cb), F32),
        scratch_shapes=[pltpu.VMEM((2, tq, 1), F32), pltpu.VMEM((2, tq, 1), F32), pltpu.VMEM((tq, LANES), F32)],
        compiler_params=_cparams(("parallel", "parallel", "parallel", "arbitrary")),
        name="fox_prompt",
    )(q, k, v, c_col, c_row)


def _online_softmax_step(s, v, m_ref, l_ref, acc_ref):
    m_old = m_ref[...]
    m_new = jnp.maximum(m_old, jnp.max(s, axis=-1, keepdims=True))
    alpha = jnp.exp(m_old - m_new)
    p = jnp.exp(s - m_new)
    l_ref[...] = alpha * l_ref[...] + jnp.sum(p, axis=-1, keepdims=True)
    m_ref[...] = m_new
    acc_ref[...] = acc_ref[...] * alpha + jnp.dot(p.astype(BF16), v, preferred_element_type=F32)


def _init_softmax(m_ref, l_ref, acc_ref):
    m_ref[...] = jnp.full(m_ref.shape, MASK_VALUE, F32)
    l_ref[...] = jnp.zeros(l_ref.shape, F32)
    acc_ref[...] = jnp.zeros(acc_ref.shape, F32)


def _fox_sample_kernel(pt_ref, wq_ref, kp_ref, vp_ref, cp_ref, kn_ref, vn_ref, cn_ref, o_ref,
                       m_ref, l_ref, acc_ref, off_ref, *, n_pages, page, tq, nh):
    del pt_ref
    j = pl.program_id(1)
    rows = nh * tq

    @pl.when(j == 0)
    def _():
        _init_softmax(m_ref, l_ref, acc_ref)
        off_ref[...] = jnp.zeros(off_ref.shape, F32)

    def step(k, v, cloc, is_new):
        s = lax.dot_general(wq_ref[0], k.astype(BF16), _NT, preferred_element_type=F32)
        c = cloc + off_ref[...]
        s = s - jnp.broadcast_to(c[:, None, :], (nh, tq, page)).reshape(rows, page)
        if is_new:
            qi = jnp.bitwise_and(lax.broadcasted_iota(jnp.int32, (rows, page), 0), tq - 1)
            ki = lax.broadcasted_iota(jnp.int32, (rows, page), 1)
            s = jnp.where(ki <= qi, s, MASK_VALUE)
        _online_softmax_step(s, v.astype(BF16), m_ref, l_ref, acc_ref)

    @pl.when(j < n_pages)
    def _():
        cloc = cp_ref[0]
        step(kp_ref[0], vp_ref[0], cloc, False)
        off_ref[...] = off_ref[...] + jnp.broadcast_to(cloc[:, page - 1:page], off_ref.shape)

    @pl.when(j == n_pages)
    def _():
        step(kn_ref[0], vn_ref[0], cn_ref[0], True)
        out = acc_ref[...] / l_ref[...]
        for h in range(nh):
            o_ref[0, :, h * HEAD_DIM:(h + 1) * HEAD_DIM] = out[h * tq:(h + 1) * tq, h * HEAD_DIM:(h + 1) * HEAD_DIM]


def _fox_sample(page_table, wq, kpool, vpool, cpool, knew, vnew, cnew, tq):
    nb, n_pages = page_table.shape
    page, cb = kpool.shape[1], kpool.shape[2]
    nh = cpool.shape[1]
    rows = wq.shape[1]
    pool_blk = lambda b, j, pt: (pt[b, jnp.minimum(j, n_pages - 1)], 0, 0)
    seq_blk = lambda b, j, pt: (b, 0, 0)
    grid_spec = pltpu.PrefetchScalarGridSpec(
        num_scalar_prefetch=1,
        grid=(nb, n_pages + 1),
        in_specs=[pl.BlockSpec((1, rows, cb), seq_blk),
                  pl.BlockSpec((1, page, cb), pool_blk),
                  pl.BlockSpec((1, page, cb), pool_blk),
                  pl.BlockSpec((1, nh, page), pool_blk),
                  pl.BlockSpec((1, page, cb), seq_blk),
                  pl.BlockSpec((1, page, cb), seq_blk),
                  pl.BlockSpec((1, nh, page), seq_blk)],
        out_specs=pl.BlockSpec((1, tq, cb), seq_blk),
        scratch_shapes=[pltpu.VMEM((rows, 1), F32), pltpu.VMEM((rows, 1), F32), pltpu.VMEM((rows, cb), F32),
                        pltpu.VMEM((nh, page), F32)],
    )
    return pl.pallas_call(
        functools.partial(_fox_sample_kernel, n_pages=n_pages, page=page, tq=tq, nh=nh),
        grid_spec=grid_spec,
        out_shape=jax.ShapeDtypeStruct((nb, tq, cb), F32),
        compiler_params=_cparams(("parallel", "arbitrary")),
        name="fox_sample",
    )(page_table, wq, kpool, vpool, cpool, knew, vnew, cnew)


def _outproj_kernel(*refs, n_in):
    ins, w_ref, res_ref, o_ref = refs[:n_in], refs[n_in], refs[n_in + 1], refs[n_in + 2]
    acc = res_ref[...]
    off = 0
    for r in ins:
        width = r.shape[1]
        acc = acc + jnp.dot(r[...].astype(BF16), w_ref[off:off + width, :], preferred_element_type=F32)
        off += width
    o_ref[...] = acc


def _outproj(ins, w, res, tm):
    m, d = res.shape
    row = lambda n: pl.BlockSpec((tm, n), lambda i: (i, 0))
    return pl.pallas_call(
        functools.partial(_outproj_kernel, n_in=len(ins)),
        grid=(m // tm,),
        in_specs=[row(a.shape[1]) for a in ins] + [pl.BlockSpec(w.shape, lambda i: (0, 0)), row(d)],
        out_specs=row(d),
        out_shape=jax.ShapeDtypeStruct((m, d), F32),
        compiler_params=_cparams(("parallel",)),
        name="outproj",
    )(*ins, w, res)


def _route(z):
    lane = lax.broadcasted_iota(jnp.int32, z.shape, 1)
    big = jnp.int32(1 << 20)
    neg = -jnp.inf
    rmax = lambda a: jnp.max(a, axis=-1, keepdims=True)
    rmin = lambda a: jnp.min(a, axis=-1, keepdims=True)
    is_g = lane < N_GROUPS
    zg = jnp.where(is_g, z, neg)
    gmax = rmax(zg)
    g_idx = rmin(jnp.where(zg == gmax, lane, big))
    g_w = 1.0 / jnp.sum(jnp.where(is_g, jnp.exp(zg - gmax), 0.0), axis=-1, keepdims=True)
    first = N_GROUPS + g_idx * E_PER_GROUP
    in_grp = jnp.logical_and(lane >= first, lane < first + E_PER_GROUP)
    v1 = jnp.where(in_grp, z, neg)
    top1 = rmax(v1)
    i1 = rmin(jnp.where(v1 == top1, lane, big))
    v2 = jnp.where(lane == i1, neg, v1)
    top2 = rmax(v2)
    i2 = rmin(jnp.where(v2 == top2, lane, big))
    e2 = jnp.exp(top2 - top1)
    den = 1.0 + e2
    return jnp.where(lane == i1, g_w / den, jnp.where(lane == i2, g_w * e2 / den, 0.0))


def _moe_kernel(x_ref, g_ref, wr_ref, br_ref, wg_ref, wu_ref, wd_ref, gf_ref, o_ref, xn_ref, cmb_ref, acc_ref,
                *, final_norm):
    e = pl.program_id(1)

    @pl.when(e == 0)
    def _():
        xn = _rms(x_ref[...], g_ref[...])
        xn_ref[...] = xn.astype(BF16)
        z = jnp.dot(xn, wr_ref[...], preferred_element_type=F32, precision=lax.Precision.HIGHEST) + br_ref[...]
        cmb_ref[...] = _route(z)
        acc_ref[...] = jnp.zeros(acc_ref.shape, F32)

    xn = xn_ref[...]
    gate = jnp.dot(xn, wg_ref[0], preferred_element_type=F32)
    up = jnp.dot(xn, wu_ref[0], preferred_element_type=F32)
    hid = (gate * jax.nn.sigmoid(gate) * up).astype(BF16)
    y = jnp.dot(hid, wd_ref[0], preferred_element_type=F32)
    lane = lax.broadcasted_iota(jnp.int32, cmb_ref.shape, 1)
    col = jnp.sum(jnp.where(lane == e + N_GROUPS, cmb_ref[...], 0.0), axis=-1, keepdims=True)
    acc_ref[...] = acc_ref[...] + col * y

    @pl.when(e == pl.num_programs(1) - 1)
    def _():
        out = x_ref[...] + acc_ref[...]
        o_ref[...] = _rms(out, gf_ref[...]) if final_norm else out


def _moe(x, g, wr, br, wg, wu, wd, gf, tm, final_norm):
    m, d = x.shape
    ne, _, dff = wg.shape
    row = pl.BlockSpec((tm, d), lambda i, e: (i, 0))
    full = lambda a: pl.BlockSpec(a.shape, lambda i, e: (0, 0))
    return pl.pallas_call(
        functools.partial(_moe_kernel, final_norm=final_norm),
        grid=(m // tm, ne),
        in_specs=[row, full(g), full(wr), full(br),
                  pl.BlockSpec((1, d, dff), lambda i, e: (e, 0, 0)),
                  pl.BlockSpec((1, d, dff), lambda i, e: (e, 0, 0)),
                  pl.BlockSpec((1, dff, d), lambda i, e: (e, 0, 0)),
                  full(gf)],
        out_specs=row,
        out_shape=jax.ShapeDtypeStruct((m, d), F32),
        scratch_shapes=[pltpu.VMEM((tm, d), BF16), pltpu.VMEM((tm, LANES), F32), pltpu.VMEM((tm, d), F32)],
        compiler_params=_cparams(("parallel", "arbitrary")),
        name="moe",
    )(x, g, wr, br, wg, wu, wd, gf)


def _select_topk(sc_ref, nc, n_sel):
    _, r, c = sc_ref.shape
    kf = jnp.float32(n_sel)
    inf = jnp.inf
    lane_pos = lax.broadcasted_iota(jnp.int32, (r, c), 1)

    def fold(fn, comb, init):
        part = lax.fori_loop(0, nc, lambda ch, a: comb(a, fn(sc_ref[ch], ch)), jnp.full((r, c), init, F32))
        return part

    def count(pred):
        part = fold(lambda x, ch: jnp.where(pred(x, ch), 1.0, 0.0), jnp.add, 0.0)
        return jnp.sum(part, axis=-1, keepdims=True)

    def row_min(fn):
        return jnp.min(fold(fn, jnp.minimum, inf), axis=-1, keepdims=True)

    mx = jnp.max(fold(lambda x, ch: x, jnp.maximum, -inf), axis=-1, keepdims=True)
    mn = row_min(lambda x, ch: jnp.where(x > -inf, x, inf))
    n_valid = count(lambda x, ch: x > -inf)
    take_all = n_valid < kf

    def bisect(_, lh):
        lo, hi = lh
        mid = 0.5 * (lo + hi)
        ge = count(lambda x, ch: x >= mid) >= kf
        return jnp.where(ge, mid, lo), jnp.where(ge, hi, mid)

    lo, _ = lax.fori_loop(0, BISECT_ITERS, bisect, (mn, mx))

    cand = row_min(lambda x, ch: jnp.where(x >= lo, x, inf))
    n_gt = count(lambda x, ch: x > cand)

    def unsettled(state):
        _, g = state
        return jnp.max(jnp.where(jnp.logical_and(g >= kf, jnp.logical_not(take_all)), 1.0, 0.0)) > 0.5

    def walk(state):
        cd, g = state
        nxt = row_min(lambda x, ch: jnp.where(x > cd, x, inf))
        cd = jnp.where(g >= kf, nxt, cd)
        return cd, count(lambda x, ch: x > cd)

    cand, n_gt = lax.while_loop(unsettled, walk, (cand, n_gt))
    thr = jnp.where(take_all, -inf, cand)
    n_gt = jnp.where(take_all, n_valid, n_gt)
    need = kf - n_gt
    n_eq = count(lambda x, ch: x == thr)
    has_excess = jnp.max(jnp.where(jnp.logical_and(n_eq > need, jnp.logical_not(take_all)), 1.0, 0.0)) > 0.5

    def tie_cut():
        def bit_step(b, ans):
            cnd = ans + jnp.left_shift(jnp.int32(1), POS_BITS - 1 - b)
            in_front = lambda x, ch: jnp.where(x == thr, jnp.where((ch * c + lane_pos) < cnd, 1.0, 0.0), 0.0)
            cnt = jnp.sum(fold(in_front, jnp.add, 0.0), axis=-1, keepdims=True)
            return jnp.where(cnt < need, cnd, ans)
        return lax.fori_loop(0, POS_BITS, bit_step, jnp.zeros((r, 1), jnp.int32))

    cut = lax.cond(has_excess, tie_cut, lambda: jnp.full((r, 1), (1 << POS_BITS) - 1, jnp.int32))

    def write(ch, _):
        x = sc_ref[ch]
        tie = jnp.where(x == thr, jnp.where((ch * c + lane_pos) <= cut, 0.0, MASK_VALUE), MASK_VALUE)
        sc_ref[ch] = jnp.where(x > -inf, jnp.where(x > thr, 0.0, tie), MASK_VALUE)
        return 0

    lax.fori_loop(0, nc, write, 0)


def _dsa_prompt_kernel(qi_ref, wi_ref, kidx_ref, q_ref, k_ref, v_ref, o_ref, sc_ref, *, tq, kc, n_sel, rep):
    i = pl.program_id(1)
    nc = ((i + 1) * tq + kc - 1) // kc
    nh = q_ref.shape[1]
    wi = wi_ref[0]
    wcols = [wi[:, h:h + 1] for h in range(H_IDX)]
    row_pos = i * tq + lax.broadcasted_iota(jnp.int32, (tq, kc), 0)
    col_iota = lax.broadcasted_iota(jnp.int32, (tq, kc), 1)

    def score_chunk(ch, _):
        start = pl.multiple_of(ch * kc, kc)
        kch = kidx_ref[0, pl.ds(start, kc), :]
        acc = jnp.zeros((tq, kc), F32)
        for h in range(H_IDX):
            d = lax.dot_general(qi_ref[0, h], kch, _NT, preferred_element_type=F32)
            acc = acc + jnp.maximum(d, 0.0) * wcols[h]
        sc_ref[ch] = jnp.where(ch * kc + col_iota <= row_pos, acc, -jnp.inf)
        return 0

    lax.fori_loop(0, nc, score_chunk, 0)
    _select_topk(sc_ref, nc, n_sel)

    for h in range(nh):
        g = h // rep
        qh = q_ref[0, h]

        def attend(ch, carry):
            m_old, l_old, acc = carry
            start = pl.multiple_of(ch * kc, kc)
            s = lax.dot_general(qh, k_ref[0, g, pl.ds(start, kc), :], _NT, preferred_element_type=F32) + sc_ref[ch]
            m_new = jnp.maximum(m_old, jnp.max(s, axis=-1, keepdims=True))
            alpha = jnp.exp(m_old - m_new)
            p = jnp.exp(s - m_new)
            l_new = alpha * l_old + jnp.sum(p, axis=-1, keepdims=True)
            pv = jnp.dot(p.astype(BF16), v_ref[0, g, pl.ds(start, kc), :], preferred_element_type=F32)
            return m_new, l_new, acc * alpha + pv

        init = (jnp.full((tq, 1), MASK_VALUE, F32), jnp.zeros((tq, 1), F32), jnp.zeros((tq, HEAD_DIM), F32))
        _, l_fin, acc = lax.fori_loop(0, nc, attend, init)
        o_ref[0, :, h * HEAD_DIM:(h + 1) * HEAD_DIM] = acc / l_fin


def _dsa_prompt(qi, wi, kidx, q, k, v, tq, kc, n_sel):
    n, nh, t, _ = q.shape
    nkv = k.shape[1]
    return pl.pallas_call(
        functools.partial(_dsa_prompt_kernel, tq=tq, kc=kc, n_sel=n_sel, rep=nh // nkv),
        grid=(n, t // tq),
        in_specs=[pl.BlockSpec((1, H_IDX, tq, D_IDX), lambda b, i: (b, 0, i, 0)),
                  pl.BlockSpec((1, tq, LANES), lambda b, i: (b, i, 0)),
                  pl.BlockSpec((1, t, D_IDX), lambda b, i: (b, 0, 0)),
                  pl.BlockSpec((1, nh, tq, HEAD_DIM), lambda b, i: (b, 0, i, 0)),
                  pl.BlockSpec((1, nkv, t, HEAD_DIM), lambda b, i: (b, 0, 0, 0)),
                  pl.BlockSpec((1, nkv, t, HEAD_DIM), lambda b, i: (b, 0, 0, 0))],
        out_specs=pl.BlockSpec((1, tq, nh * HEAD_DIM), lambda b, i: (b, i, 0)),
        out_shape=jax.ShapeDtypeStruct((n, t, nh * HEAD_DIM), F32),
        scratch_shapes=[pltpu.VMEM((t // kc, tq, kc), F32)],
        compiler_params=_cparams(("parallel", "arbitrary")),
        name="dsa_prompt",
    )(qi, wi, kidx, q, k, v)


def _dsa_sample_score_kernel(pt_ref, qi_ref, wi_ref, kp_ref, kn_ref, b_ref, sc_ref, *, n_pages, page, tq, past, n_sel):
    del pt_ref
    j = pl.program_id(1)

    def score(kpage):
        d = lax.dot_general(qi_ref[0], kpage.astype(BF16), _NT, preferred_element_type=F32)
        r = (jnp.maximum(d, 0.0) * wi_ref[0]).reshape(H_IDX, tq, page)
        acc = r[0]
        for h in range(1, H_IDX):
            acc = acc + r[h]
        pos = j * page + lax.broadcasted_iota(jnp.int32, (tq, page), 1)
        qpos = past + lax.broadcasted_iota(jnp.int32, (tq, page), 0)
        sc_ref[j] = jnp.where(pos <= qpos, acc, -jnp.inf)

    @pl.when(j < n_pages)
    def _():
        score(kp_ref[0])

    @pl.when(j == n_pages)
    def _():
        score(kn_ref[0])
        _select_topk(sc_ref, n_pages + 1, n_sel)
        b_ref[0] = sc_ref[...]


def _dsa_sample_scores(page_table, qi, wi, kidx_pool, kidx_new, tq, n_sel):
    nb, n_pages = page_table.shape
    page = kidx_pool.shape[1]
    rows = qi.shape[1]
    pool_blk = lambda b, j, pt: (pt[b, jnp.minimum(j, n_pages - 1)], 0, 0)
    seq_blk = lambda b, j, pt: (b, 0, 0)
    grid_spec = pltpu.PrefetchScalarGridSpec(
        num_scalar_prefetch=1,
        grid=(nb, n_pages + 1),
        in_specs=[pl.BlockSpec((1, rows, D_IDX), seq_blk),
                  pl.BlockSpec((1, rows, 1), seq_blk),
                  pl.BlockSpec((1, page, D_IDX), pool_blk),
                  pl.BlockSpec((1, page, D_IDX), seq_blk)],
        out_specs=pl.BlockSpec((1, n_pages + 1, tq, page), lambda b, j, pt: (b, 0, 0, 0)),
        scratch_shapes=[pltpu.VMEM((n_pages + 1, tq, page), F32)],
    )
    return pl.pallas_call(
        functools.partial(_dsa_sample_score_kernel, n_pages=n_pages, page=page, tq=tq,
                          past=n_pages * page, n_sel=n_sel),
        grid_spec=grid_spec,
        out_shape=jax.ShapeDtypeStruct((nb, n_pages + 1, tq, page), F32),
        compiler_params=_cparams(("parallel", "arbitrary")),
        name="dsa_sample_scores",
    )(page_table, qi, wi, kidx_pool, kidx_new)


def _dsa_sample_attn_kernel(pt_ref, wq_ref, kp_ref, vp_ref, kn_ref, vn_ref, b_ref, o_ref, m_ref, l_ref, acc_ref,
                            *, n_pages, tq, nh, rep):
    del pt_ref
    j = pl.program_id(1)

    @pl.when(j == 0)
    def _():
        _init_softmax(m_ref, l_ref, acc_ref)

    def step(k, v):
        s = lax.dot_general(wq_ref[0], k.astype(BF16), _NT, preferred_element_type=F32)
        s = s + jnp.tile(b_ref[0, 0], (nh, 1))
        _online_softmax_step(s, v.astype(BF16), m_ref, l_ref, acc_ref)

    @pl.when(j < n_pages)
    def _():
        step(kp_ref[0], vp_ref[0])

    @pl.when(j == n_pages)
    def _():
        step(kn_ref[0], vn_ref[0])
        out = acc_ref[...] / l_ref[...]
        for h in range(nh):
            g = h // rep
            o_ref[0, :, h * HEAD_DIM:(h + 1) * HEAD_DIM] = out[h * tq:(h + 1) * tq, g * HEAD_DIM:(g + 1) * HEAD_DIM]


def _dsa_sample_attn(page_table, wq, kpool, vpool, knew, vnew, bias, tq, nh):
    nb, n_pages = page_table.shape
    page, ckv = kpool.shape[1], kpool.shape[2]
    rows = wq.shape[1]
    pool_blk = lambda b, j, pt: (pt[b, jnp.minimum(j, n_pages - 1)], 0, 0)
    seq_blk = lambda b, j, pt: (b, 0, 0)
    grid_spec = pltpu.PrefetchScalarGridSpec(
        num_scalar_prefetch=1,
        grid=(nb, n_pages + 1),
        in_specs=[pl.BlockSpec((1, rows, ckv), seq_blk),
                  pl.BlockSpec((1, page, ckv), pool_blk),
                  pl.BlockSpec((1, page, ckv), pool_blk),
                  pl.BlockSpec((1, page, ckv), seq_blk),
                  pl.BlockSpec((1, page, ckv), seq_blk),
                  pl.BlockSpec((1, 1, tq, page), lambda b, j, pt: (b, j, 0, 0))],
        out_specs=pl.BlockSpec((1, tq, nh * HEAD_DIM), seq_blk),
        scratch_shapes=[pltpu.VMEM((rows, 1), F32), pltpu.VMEM((rows, 1), F32), pltpu.VMEM((rows, ckv), F32)],
    )
    return pl.pallas_call(
        functools.partial(_dsa_sample_attn_kernel, n_pages=n_pages, tq=tq, nh=nh, rep=nh * HEAD_DIM // ckv),
        grid_spec=grid_spec,
        out_shape=jax.ShapeDtypeStruct((nb, tq, nh * HEAD_DIM), F32),
        compiler_params=_cparams(("parallel", "arbitrary")),
        name="dsa_sample_attn",
    )(page_table, wq, kpool, vpool, knew, vnew, bias)


def _pad_cols(a, n):
    return jnp.pad(a, [(0, 0)] * (a.ndim - 1) + [(0, n - a.shape[-1])])


def _rope_tables(pos):
    half = HEAD_DIM // 2
    inv = ROPE_THETA ** (-jnp.arange(half, dtype=F32) / half)
    ang = pos.astype(F32)[:, None] * inv[None, :]
    cos, sin = jnp.cos(ang), jnp.sin(ang)
    return jnp.tile(cos, (1, LANES // half)), jnp.tile(jnp.concatenate([-sin, sin], axis=1), (1, LANES // HEAD_DIM))


def _block_diag_queries(q, n_heads, n_kv):
    b, t, _ = q.shape
    qh = q.reshape(b, t, n_heads, HEAD_DIM).transpose(0, 2, 1, 3)
    onehot = jax.nn.one_hot(jnp.arange(n_heads) // (n_heads // n_kv), n_kv, dtype=q.dtype)
    w = qh[:, :, :, None, :] * onehot[None, :, None, :, None]
    return w.reshape(b, n_heads * t, n_kv * HEAD_DIM).astype(BF16)


def _pad_rows(a, rows):
    return jnp.pad(a, [(0, 0), (0, rows - a.shape[1]), (0, 0)])


def _tile(m, pref):
    for t in pref:
        if m % t == 0:
            return t
    return m


def kernel(x_prompt, x_sample, state_conv, cache_fox_k, cache_fox_v, cache_fox_logf, cache_dsa_k, cache_dsa_v,
           cache_dsa_kidx, page_table, norm_mix_even, w_in_even, b_forget, w_dw, b_dw, ln_conv_g, ln_conv_b,
           w_out_even, norm_mix_odd, w_in_odd, w_out_odd, norm_ffn, w_group, b_group, w_router, b_router, w_gate,
           w_up, w_down, norm_final):
    n_p, t_p, d = x_prompt.shape
    n_s, t_s, _ = x_sample.shape
    depth = norm_ffn.shape[0]
    page = cache_fox_k.shape[2]
    n_pages = page_table.shape[1]
    past = n_pages * page
    c_a = w_dw.shape[2]
    h_b = b_forget.shape[1]
    c_b = h_b * HEAD_DIM
    kv_c = cache_dsa_k.shape[3]
    h_c = w_out_odd.shape[1] // HEAD_DIM
    nq, nkv, nqi = h_c * HEAD_DIM, kv_c * HEAD_DIM, H_IDX * D_IDX
    assert t_s == SUBLANES and depth == 2 and w_in_even.shape[0] == 1 and w_in_odd.shape[0] == 1

    mp, ms = n_p * t_p, n_s * t_s
    hp = x_prompt.reshape(mp, d)
    hs = x_sample.reshape(ms, d)
    tm_p = _tile(mp, (512, 256, 128))
    row = lambda a: a.reshape(1, -1)

    def moe_layer(h, layer, tm, final):
        wr = _pad_cols(jnp.concatenate([w_group[layer], w_router[layer]], axis=1), LANES)
        br = _pad_cols(jnp.concatenate([b_group[layer], b_router[layer]]).reshape(1, -1), LANES)
        return _moe(h, row(norm_ffn[layer]), wr, br, w_gate[layer].astype(BF16), w_up[layer].astype(BF16),
                    w_down[layer].astype(BF16), row(norm_final), tm, final)

    w_e = w_in_even[0]
    w_e = jnp.concatenate([w_e[:, :2 * c_a + 3 * c_b], _pad_cols(w_e[:, 2 * c_a + 3 * c_b:], LANES)], axis=1).astype(BF16)
    bf = _pad_cols(b_forget[0].reshape(1, -1), LANES)
    g_e = row(norm_mix_even[0])
    ag_p, q_p, k_p, v_p, lf_p = _proj_even(hp, g_e, w_e, bf, c_a, c_b, tm_p)
    ag_s, q_s, k_s, v_s, lf_s = _proj_even(hs, g_e, w_e, bf, c_a, c_b, ms)
    logf_p = lf_p[:, :h_b].reshape(n_p, t_p, h_b)
    logf_s = lf_s[:, :h_b].reshape(n_s, t_s, h_b)

    conv_args = (w_dw[0], row(b_dw[0]), row(ln_conv_g[0]), row(ln_conv_b[0]))
    a_p, cst_p = _conv_module(ag_p.reshape(n_p, t_p, 2 * c_a), jnp.zeros((n_p, HIST, c_a), F32), *conv_args,
                              _tile(t_p, (512, 256, 128)))
    a_s, cst_s = _conv_module(ag_s.reshape(n_s, t_s, 2 * c_a), state_conv[0], *conv_args, t_s)

    nblk = t_p // LANES
    lf_blocks = logf_p.reshape(n_p, nblk, LANES, h_b).transpose(0, 1, 3, 2).reshape(n_p * nblk, h_b, LANES)
    c_p = _block_cumsum(lf_blocks, nblk, True).reshape(n_p, nblk, h_b, LANES).transpose(0, 2, 1, 3)
    c_p = c_p.reshape(n_p, h_b, t_p)
    b_p = _fox_prompt(q_p.reshape(n_p, t_p, c_b), k_p.reshape(n_p, t_p, c_b), v_p.reshape(n_p, t_p, c_b),
                      c_p[..., None], c_p[:, :, None, :], _tile(t_p, (256, 128)), _tile(t_p, (512, 256, 128)))

    n_pool = cache_fox_k.shape[1]
    cpool = _block_cumsum(cache_fox_logf[0].transpose(0, 2, 1), _tile(n_pool, (64, 32, 16, 8, 4, 2)), False)
    cnew = _block_cumsum(_pad_cols(logf_s.transpose(0, 2, 1), page), _tile(n_s, (32, 16, 8, 4, 2)), False)
    wq_s = _block_diag_queries(q_s.reshape(n_s, t_s, c_b), h_b, h_b)
    b_s = _fox_sample(page_table, wq_s, cache_fox_k[0].reshape(n_pool, page, c_b),
                      cache_fox_v[0].reshape(n_pool, page, c_b), cpool,
                      _pad_rows(k_s.reshape(n_s, t_s, c_b), page), _pad_rows(v_s.reshape(n_s, t_s, c_b), page),
                      cnew, t_s)

    w_oe = w_out_even[0].astype(BF16)
    hp = _outproj([a_p.reshape(mp, c_a), b_p.reshape(mp, c_b)], w_oe, hp, tm_p)
    hs = _outproj([a_s.reshape(ms, c_a), b_s.reshape(ms, c_b)], w_oe, hs, ms)
    hp = moe_layer(hp, 0, _tile(mp, (1024, 512, 256, 128)), False)
    hs = moe_layer(hs, 0, ms, False)

    w_o = w_in_odd[0]
    s4 = nq + 2 * nkv + nqi
    w_o = jnp.concatenate([w_o[:, :s4], _pad_cols(w_o[:, s4:s4 + D_IDX], LANES), _pad_cols(w_o[:, s4 + D_IDX:], LANES)],
                          axis=1).astype(BF16)
    g_o = row(norm_mix_odd[0])
    wi_scale = (H_IDX ** -0.5) * (D_IDX ** -0.5)
    cos_p, sin_p = _rope_tables(jnp.arange(t_p))
    cos_s, sin_s = _rope_tables(jnp.tile(past + jnp.arange(t_s), n_s))
    dq_p, dk_p, dv_p, dqi_p, dki_p, dwi_p = _proj_odd(hp, g_o, w_o, cos_p, sin_p, nq, nkv, nqi, wi_scale,
                                                      _tile(t_p, (512, 256, 128)))
    dq_s, dk_s, dv_s, dqi_s, dki_s, dwi_s = _proj_odd(hs, g_o, w_o, cos_s, sin_s, nq, nkv, nqi, wi_scale, ms)
    dki_p, dki_s = dki_p[:, :D_IDX], dki_s[:, :D_IDX]

    heads = lambda a, n, nh: a.reshape(n, -1, nh, HEAD_DIM).transpose(0, 2, 1, 3).astype(BF16)
    n_sel_p = min(TOPK_MAX, t_p // 4)
    o_p = _dsa_prompt(heads(dqi_p, n_p, H_IDX), dwi_p.reshape(n_p, t_p, LANES),
                      dki_p.reshape(n_p, t_p, D_IDX).astype(BF16), heads(dq_p, n_p, h_c), heads(dk_p, n_p, kv_c),
                      heads(dv_p, n_p, kv_c), _tile(t_p, (128,)), _tile(t_p, (512, 256, 128)), n_sel_p)

    n_sel_s = min(TOPK_MAX, (past + t_s) // 4)
    qi_rows = heads(dqi_s, n_s, H_IDX).reshape(n_s, H_IDX * t_s, D_IDX)
    wi_rows = dwi_s[:, :H_IDX].reshape(n_s, t_s, H_IDX).transpose(0, 2, 1).reshape(n_s, H_IDX * t_s, 1)
    bias_s = _dsa_sample_scores(page_table, qi_rows, wi_rows, cache_dsa_kidx[0],
                                _pad_rows(dki_s.reshape(n_s, t_s, D_IDX), page), t_s, n_sel_s)
    wq_d = _block_diag_queries(dq_s.reshape(n_s, t_s, nq), h_c, kv_c)
    o_s = _dsa_sample_attn(page_table, wq_d, cache_dsa_k[0].reshape(n_pool, page, nkv),
                           cache_dsa_v[0].reshape(n_pool, page, nkv),
                           _pad_rows(dk_s.reshape(n_s, t_s, nkv), page), _pad_rows(dv_s.reshape(n_s, t_s, nkv), page),
                           bias_s, t_s, h_c)

    w_oo = w_out_odd[0].astype(BF16)
    hp = _outproj([o_p.reshape(mp, nq)], w_oo, hp, tm_p)
    hs = _outproj([o_s.reshape(ms, nq)], w_oo, hs, ms)
    hp = moe_layer(hp, 1, _tile(mp, (1024, 512, 256, 128)), True)
    hs = moe_layer(hs, 1, ms, True)

    return (hp.reshape(n_p, t_p, d), hs.reshape(n_s, t_s, d),
            cst_p[None], cst_s[None],
            k_p.reshape(1, n_p, t_p, h_b, HEAD_DIM), k_s.reshape(1, n_s, t_s, h_b, HEAD_DIM),
            v_p.reshape(1, n_p, t_p, h_b, HEAD_DIM), v_s.reshape(1, n_s, t_s, h_b, HEAD_DIM),
            logf_p[None], logf_s[None],
            dk_p.reshape(1, n_p, t_p, kv_c, HEAD_DIM), dk_s.reshape(1, n_s, t_s, kv_c, HEAD_DIM),
            dv_p.reshape(1, n_p, t_p, kv_c, HEAD_DIM), dv_s.reshape(1, n_s, t_s, kv_c, HEAD_DIM),
            dki_p.reshape(1, n_p, t_p, D_IDX), dki_s.reshape(1, n_s, t_s, D_IDX))
```

```python
import functools

import jax
import jax.numpy as jnp
from jax import lax
from jax.experimental import pallas as pl
from jax.experimental.pallas import tpu as pltpu

F32 = jnp.float32
BF16 = jnp.bfloat16

HEAD_DIM = 64
CONV_W = 31
H_IDX = 8
D_IDX = 64
TOPK_MAX = 256
N_GROUPS = 4
E_PER_GROUP = 4
N_EXPERTS = N_GROUPS * E_PER_GROUP
ROPE_THETA = 10000.0
RMS_EPS = 1e-6
LN_EPS = 1e-5

LANES = 128
SUBLANES = 8
VMEM_LIMIT = 56 * 1024 * 1024

MASK_VALUE = -1e30
HIST = CONV_W - 1
HIST_PAD = 32
BISECT_ITERS = 20
POS_BITS = 14
PAGES_PER_STEP = 8

_NT = (((1,), (1,)), ((), ()))


def _cparams(sem):
    return pltpu.CompilerParams(dimension_semantics=sem, vmem_limit_bytes=VMEM_LIMIT)


def _rms(x, g):
    ms = jnp.mean(x * x, axis=-1, keepdims=True)
    return x * lax.rsqrt(ms + RMS_EPS) * g


def _rope128(xb, cos, sin_signed):
    lane = lax.broadcasted_iota(jnp.int32, xb.shape, 1)
    first_half = jnp.bitwise_and(lane, HEAD_DIM - 1) < (HEAD_DIM // 2)
    rot = jnp.where(first_half, pltpu.roll(xb, LANES - HEAD_DIM // 2, 1), pltpu.roll(xb, HEAD_DIM // 2, 1))
    return xb * cos + rot * sin_signed


def _proj_even_kernel(x_ref, g_ref, w_ref, bf_ref, ag_ref, q_ref, k_ref, v_ref, lf_ref, *, c_a, c_b):
    h = _rms(x_ref[...], g_ref[...]).astype(BF16)

    def seg(a, b):
        return jnp.dot(h, w_ref[:, a:b], preferred_element_type=F32)

    o = 2 * c_a
    ag_ref[...] = seg(0, o)
    q_ref[...] = seg(o, o + c_b) * (HEAD_DIM ** -0.5)
    k_ref[...] = seg(o + c_b, o + 2 * c_b)
    v_ref[...] = seg(o + 2 * c_b, o + 3 * c_b)
    z = seg(o + 3 * c_b, o + 3 * c_b + LANES) + bf_ref[...]
    lf_ref[...] = jnp.minimum(z, 0.0) - jnp.log1p(jnp.exp(-jnp.abs(z)))


def _proj_even(x, g, w, bf, c_a, c_b, tm):
    m, d = x.shape
    row = lambda n: pl.BlockSpec((tm, n), lambda i: (i, 0))
    full = lambda a: pl.BlockSpec(a.shape, lambda i: (0, 0))
    return pl.pallas_call(
        functools.partial(_proj_even_kernel, c_a=c_a, c_b=c_b),
        grid=(m // tm,),
        in_specs=[row(d), full(g), full(w), full(bf)],
        out_specs=[row(2 * c_a), row(c_b), row(c_b), row(c_b), row(LANES)],
        out_shape=[jax.ShapeDtypeStruct((m, n), F32) for n in (2 * c_a, c_b, c_b, c_b, LANES)],
        compiler_params=_cparams(("parallel",)),
        name="proj_even",
    )(x, g, w, bf)


def _proj_odd_kernel(x_ref, g_ref, w_ref, cos_ref, sin_ref, q_ref, k_ref, v_ref, qi_ref, ki_ref, wi_ref,
                     *, nq, nkv, nqi, wi_scale):
    h = _rms(x_ref[...], g_ref[...]).astype(BF16)
    cos = cos_ref[...]
    sin = sin_ref[...]

    def seg(a, b):
        return jnp.dot(h, w_ref[:, a:b], preferred_element_type=F32)

    def roped(a, n, out_ref, scale=None):
        for c in range(n // LANES):
            blk = _rope128(seg(a + c * LANES, a + (c + 1) * LANES), cos, sin)
            out_ref[:, c * LANES:(c + 1) * LANES] = blk if scale is None else blk * scale

    o = 0
    roped(o, nq, q_ref, HEAD_DIM ** -0.5)
    o += nq
    roped(o, nkv, k_ref)
    o += nkv
    v_ref[...] = seg(o, o + nkv)
    o += nkv
    roped(o, nqi, qi_ref)
    o += nqi
    roped(o, LANES, ki_ref)
    o += LANES
    wi_ref[...] = seg(o, o + LANES) * wi_scale


def _proj_odd(x, g, w, cos, sin, nq, nkv, nqi, wi_scale, tm):
    m, d = x.shape
    nper = cos.shape[0] // tm
    row = lambda n: pl.BlockSpec((tm, n), lambda i: (i, 0))
    full = lambda a: pl.BlockSpec(a.shape, lambda i: (0, 0))
    tab = pl.BlockSpec((tm, LANES), lambda i: (i % nper, 0))
    widths = (nq, nkv, nkv, nqi, LANES, LANES)
    return pl.pallas_call(
        functools.partial(_proj_odd_kernel, nq=nq, nkv=nkv, nqi=nqi, wi_scale=wi_scale),
        grid=(m // tm,),
        in_specs=[row(d), full(g), full(w), tab, tab],
        out_specs=[row(n) for n in widths],
        out_shape=[jax.ShapeDtypeStruct((m, n), F32) for n in widths],
        compiler_params=_cparams(("parallel",)),
        name="proj_odd",
    )(x, g, w, cos, sin)


def _conv_kernel(ag_ref, st_ref, wdw_ref, bdw_ref, lng_ref, lnb_ref, out_ref, nst_ref, buf_ref, *, tt, c_a, rc):
    t = pl.program_id(1)
    lo = HIST_PAD - HIST

    @pl.when(t == 0)
    def _():
        buf_ref[lo:HIST_PAD, :] = st_ref[0]

    @pl.when(t > 0)
    def _():
        buf_ref[lo:HIST_PAD, :] = buf_ref[tt + lo:tt + HIST_PAD, :]

    ag = ag_ref[0]
    buf_ref[HIST_PAD:HIST_PAD + tt, :] = ag[:, :c_a] * jax.nn.sigmoid(ag[:, c_a:])

    bdw = bdw_ref[...]
    lng = lng_ref[...]
    lnb = lnb_ref[...]
    for r in range(tt // rc):
        acc = jnp.zeros((rc, c_a), F32) + bdw
        for j in range(CONV_W):
            s = r * rc + lo + j
            acc = acc + wdw_ref[j:j + 1, :] * buf_ref[s:s + rc, :]
        mu = jnp.mean(acc, axis=-1, keepdims=True)
        cen = acc - mu
        var = jnp.mean(cen * cen, axis=-1, keepdims=True)
        cn = cen * lax.rsqrt(var + LN_EPS) * lng + lnb
        out_ref[0, r * rc:(r + 1) * rc, :] = cn * jax.nn.sigmoid(cn)

    @pl.when(t == pl.num_programs(1) - 1)
    def _():
        nst_ref[0] = buf_ref[tt + lo:tt + HIST_PAD, :]


def _conv_module(ag, state, wdw, bdw, lng, lnb, tt):
    n, t, two_ca = ag.shape
    c_a = two_ca // 2
    rc = min(tt, 64)
    vec = lambda a: pl.BlockSpec(a.shape, lambda i, j: (0, 0))
    return pl.pallas_call(
        functools.partial(_conv_kernel, tt=tt, c_a=c_a, rc=rc),
        grid=(n, t // tt),
        in_specs=[pl.BlockSpec((1, tt, two_ca), lambda i, j: (i, j, 0)),
                  pl.BlockSpec((1, HIST, c_a), lambda i, j: (i, 0, 0)),
                  vec(wdw), vec(bdw), vec(lng), vec(lnb)],
        out_specs=[pl.BlockSpec((1, tt, c_a), lambda i, j: (i, j, 0)),
                   pl.BlockSpec((1, HIST, c_a), lambda i, j: (i, 0, 0))],
        out_shape=[jax.ShapeDtypeStruct((n, t, c_a), F32), jax.ShapeDtypeStruct((n, HIST, c_a), F32)],
        scratch_shapes=[pltpu.VMEM((HIST_PAD + tt, c_a), F32)],
        compiler_params=_cparams(("parallel", "arbitrary")),
        name="conv_module",
    )(ag, state, wdw, bdw, lng, lnb)


def _lane_cumsum(x):
    lane = lax.broadcasted_iota(jnp.int32, x.shape, 1)
    s = 1
    while s < LANES:
        x = x + jnp.where(lane >= s, pltpu.roll(x, s, 1), 0.0)
        s *= 2
    return x


def _cumsum_kernel(x_ref, o_ref, *, nb):
    h = x_ref.shape[1]
    loc = _lane_cumsum(x_ref[...].reshape(nb * h, LANES)).reshape(nb, h, LANES)
    off = jnp.zeros((h, LANES), F32)
    for b in range(nb):
        cur = loc[b] + off
        o_ref[b] = cur
        off = jnp.broadcast_to(cur[:, LANES - 1:LANES], (h, LANES))


def _block_cumsum(x, nb):
    b, h, _ = x.shape
    spec = pl.BlockSpec((nb, h, LANES), lambda i: (i, 0, 0))
    return pl.pallas_call(
        functools.partial(_cumsum_kernel, nb=nb),
        grid=(b // nb,),
        in_specs=[spec], out_specs=spec,
        out_shape=jax.ShapeDtypeStruct(x.shape, F32),
        compiler_params=_cparams(("parallel",)),
        name="block_cumsum",
    )(x)


def _page_cumsum_kernel(x_ref, c_ref, t_ref, *, nh):
    p = x_ref.shape[0]
    x = x_ref[...].reshape(p * SUBLANES, LANES)
    lane = lax.broadcasted_iota(jnp.int32, x.shape, 1)
    within = x
    total = x
    s = nh
    while s < LANES:
        within = within + jnp.where(lane >= s, pltpu.roll(within, s, 1), 0.0)
        total = total + pltpu.roll(total, s, 1)
        s *= 2
    within = within.reshape(p, SUBLANES, LANES)
    total = total.reshape(p, SUBLANES, LANES)
    before = jnp.zeros((p, 1, LANES), F32)
    for r in range(SUBLANES):
        c_ref[:, r:r + 1, :] = within[:, r:r + 1, :] + before
        before = before + total[:, r:r + 1, :]
    t_ref[...] = jnp.broadcast_to(before, (p, SUBLANES, LANES))


def _page_cumsum(x, nh, pb):
    p = x.shape[0]
    spec = pl.BlockSpec((pb, SUBLANES, LANES), lambda i: (i, 0, 0))
    return pl.pallas_call(
        functools.partial(_page_cumsum_kernel, nh=nh),
        grid=(p // pb,),
        in_specs=[spec], out_specs=[spec, spec],
        out_shape=[jax.ShapeDtypeStruct(x.shape, F32)] * 2,
        compiler_params=_cparams(("parallel",)),
        name="page_cumsum",
    )(x)


def _flash_step_t(s_t, v_t, m_ref, acc_ref, idx):
    m_old = m_ref[idx]
    m_new = jnp.maximum(m_old, jnp.max(s_t, axis=0, keepdims=True))
    alpha = jnp.exp(m_old - m_new)
    p = jnp.exp(s_t - m_new).astype(BF16)
    acc_ref[idx] = acc_ref[idx] * alpha + jnp.dot(v_t, p, preferred_element_type=F32)
    m_ref[idx] = m_new


def _fox_prompt_kernel(qt_ref, k_ref, vt_ref, c_ref, o_ref, m_ref, acc_ref, sa_ref, sb_ref, *, tq, kc):
    i = pl.program_id(2)
    m_ref[...] = jnp.full(m_ref.shape, MASK_VALUE, F32)
    acc_ref[...] = jnp.zeros(acc_ref.shape, F32)
    q_pos = i * tq + lax.broadcasted_iota(jnp.int32, (kc // 2, tq), 1)
    k_iota = lax.broadcasted_iota(jnp.int32, (kc // 2, tq), 0)

    sub = kc // 2

    def logits(ch, half, s_ref):
        start = pl.multiple_of(ch * kc + half * sub, sub)
        k = k_ref[0, pl.ds(start, sub), :].astype(BF16)
        for hh in range(2):
            s_ref[hh] = jnp.dot(k, qt_ref[0, hh], preferred_element_type=F32)

    def softmax_pv(ch, half, s_ref, masked):
        start = pl.multiple_of(ch * kc + half * sub, sub)
        for hh in range(2):
            s = s_ref[hh] - jnp.tile(c_ref[0, hh, pl.ds(start, sub), :], (1, tq // LANES))
            if masked:
                s = jnp.where(start + k_iota <= q_pos, s, MASK_VALUE)
            _flash_step_t(s, vt_ref[0, hh, :, pl.ds(start, sub)], m_ref, acc_ref, hh)

    def run(lo, hi, masked):
        def body(ch, _):
            logits(ch, 1, sb_ref)
            softmax_pv(ch, 0, sa_ref, masked)
            logits(jnp.minimum(ch + 1, hi - 1), 0, sa_ref)
            softmax_pv(ch, 1, sb_ref, masked)
            return 0

        @pl.when(lo < hi)
        def _():
            logits(lo, 0, sa_ref)
            lax.fori_loop(lo, hi, body, 0)

    n_full = (i * tq) // kc
    run(0, n_full, False)
    run(n_full, n_full + tq // kc, True)
    out_t = jnp.concatenate([acc_ref[hh, :HEAD_DIM, :] / acc_ref[hh, HEAD_DIM:HEAD_DIM + 1, :] for hh in range(2)],
                            axis=0)
    for b in range(tq // LANES):
        o_ref[0, b * LANES:(b + 1) * LANES, :] = out_t[:, b * LANES:(b + 1) * LANES].T


def _fox_prompt(qt, k, vt, c_b, tq, kc):
    n, t, cb = k.shape
    hp = cb // LANES
    vrows = vt.shape[2]
    return pl.pallas_call(
        functools.partial(_fox_prompt_kernel, tq=tq, kc=kc),
        grid=(n, hp, t // tq),
        in_specs=[pl.BlockSpec((1, 2, LANES, tq), lambda b, h, i: (b, h, 0, i)),
                  pl.BlockSpec((1, t, LANES), lambda b, h, i: (b, 0, h)),
                  pl.BlockSpec((1, 2, vrows, t), lambda b, h, i: (b, h, 0, 0)),
                  pl.BlockSpec((1, 2, t, LANES), lambda b, h, i: (b, h, 0, 0))],
        out_specs=pl.BlockSpec((1, tq, LANES), lambda b, h, i: (b, i, h)),
        out_shape=jax.ShapeDtypeStruct((n, t, cb), F32),
        scratch_shapes=[pltpu.VMEM((2, 1, tq), F32), pltpu.VMEM((2, vrows, tq), F32),
                        pltpu.VMEM((2, kc // 2, tq), F32), pltpu.VMEM((2, kc // 2, tq), F32)],
        compiler_params=_cparams(("parallel", "parallel", "arbitrary")),
        name="fox_prompt",
    )(qt, k, vt, c_b)


def _init_softmax(m_ref, l_ref, acc_ref):
    m_ref[...] = jnp.full(m_ref.shape, MASK_VALUE, F32)
    l_ref[...] = jnp.zeros(l_ref.shape, F32)
    acc_ref[...] = jnp.zeros(acc_ref.shape, F32)


def _softmax_update(ss, vs, m_ref, l_ref, acc_ref):
    m_old = m_ref[...]
    m_new = m_old
    for s in ss:
        m_new = jnp.maximum(m_new, jnp.max(s, axis=-1, keepdims=True))
    alpha = jnp.exp(m_old - m_new)
    l_new = alpha * l_ref[...]
    acc = alpha * acc_ref[...]
    for s, v in zip(ss, vs):
        p = jnp.exp(s - m_new)
        l_new = l_new + jnp.sum(p, axis=-1, keepdims=True)
        acc = acc + jnp.dot(p.astype(BF16), v, preferred_element_type=F32)
    m_ref[...] = m_new
    l_ref[...] = l_new
    acc_ref[...] = acc


def _paged_specs(pp, n_steps, block, index_tail):
    def spec(p):
        return pl.BlockSpec(block, lambda b, j, pt: index_tail(pt[b, jnp.minimum(j, n_steps - 1) * pp + p]))
    return [spec(p) for p in range(pp)]


def _fox_sample_kernel(pt_ref, q_ref, *refs, pp, n_steps, tq, nh):
    del pt_ref
    kp, vp, cp, tp = refs[:pp], refs[pp:2 * pp], refs[2 * pp:3 * pp], refs[3 * pp:4 * pp]
    kn_ref, vn_ref, cn_ref, o_ref, m_ref, l_ref, acc_ref, off_ref, dmask_ref = refs[4 * pp:]
    j = pl.program_id(1)
    rows, cols = dmask_ref.shape
    tq_bits = tq.bit_length() - 1
    nh_bits = nh.bit_length() - 1

    @pl.when(j == 0)
    def _():
        _init_softmax(m_ref, l_ref, acc_ref)
        off_ref[...] = jnp.zeros(off_ref.shape, F32)
        r = lax.broadcasted_iota(jnp.int32, (rows, cols), 0)
        c = lax.broadcasted_iota(jnp.int32, (rows, cols), 1)
        same_head = jnp.bitwise_and(c, nh - 1) == lax.shift_right_logical(r, tq_bits)
        dmask_ref[...] = jnp.where(same_head, 0.0, MASK_VALUE)

    @pl.when(j < n_steps)
    def _():
        q = q_ref[0]
        off = off_ref[...]
        ss, vs = [], []
        for p in range(pp):
            k2 = kp[p][0, 0].reshape(cols, HEAD_DIM).astype(BF16)
            s = lax.dot_general(q, k2, _NT, preferred_element_type=F32)
            ss.append(s - (cp[p][0] + off) + dmask_ref[...])
            vs.append(vp[p][0, 0].reshape(cols, HEAD_DIM).astype(BF16))
            off = off + tp[p][0]
        off_ref[...] = off
        _softmax_update(ss, vs, m_ref, l_ref, acc_ref)

    @pl.when(j == n_steps)
    def _():
        nn = tq * nh
        kn = kn_ref[0].reshape(nn, HEAD_DIM).astype(BF16)
        vn = vn_ref[0].reshape(nn, HEAD_DIM).astype(BF16)
        s = lax.dot_general(q_ref[0], kn, _NT, preferred_element_type=F32)
        r = lax.broadcasted_iota(jnp.int32, (rows, nn), 0)
        c = lax.broadcasted_iota(jnp.int32, (rows, nn), 1)
        same_head = jnp.bitwise_and(c, nh - 1) == lax.shift_right_logical(r, tq_bits)
        causal = lax.shift_right_logical(c, nh_bits) <= jnp.bitwise_and(r, tq - 1)
        s = jnp.where(jnp.logical_and(same_head, causal), s - (cn_ref[0] + off_ref[:, :nn]), MASK_VALUE)
        _softmax_update([s], [vn], m_ref, l_ref, acc_ref)
        o_ref[0] = acc_ref[...] / l_ref[...]


def _fox_sample(page_table, q2, kcache, vcache, cpool, tpool, knew, vnew, cnew, tq, pp):
    nb, n_pages = page_table.shape
    _, _, page, nh, _ = kcache.shape
    n_steps = n_pages // pp
    rows = q2.shape[1]
    cols = page * nh
    seq3 = lambda b, j, pt: (b, 0, 0)
    seq4 = lambda b, j, pt: (b, 0, 0, 0)
    kv_specs = lambda: _paged_specs(pp, n_steps, (1, 1, page, nh, HEAD_DIM), lambda pg: (0, pg, 0, 0, 0))
    c_specs = lambda: _paged_specs(pp, n_steps, (1, 1, cols), lambda pg: (pg, 0, 0))
    grid_spec = pltpu.PrefetchScalarGridSpec(
        num_scalar_prefetch=1,
        grid=(nb, n_steps + 1),
        in_specs=[pl.BlockSpec((1, rows, HEAD_DIM), seq3)] + kv_specs() + kv_specs() + c_specs() + c_specs()
                 + [pl.BlockSpec((1, tq, nh, HEAD_DIM), seq4), pl.BlockSpec((1, tq, nh, HEAD_DIM), seq4),
                    pl.BlockSpec((1, 1, tq * nh), seq3)],
        out_specs=pl.BlockSpec((1, rows, HEAD_DIM), seq3),
        scratch_shapes=[pltpu.VMEM((rows, 1), F32), pltpu.VMEM((rows, 1), F32), pltpu.VMEM((rows, HEAD_DIM), F32),
                        pltpu.VMEM((1, cols), F32), pltpu.VMEM((rows, cols), F32)],
    )
    return pl.pallas_call(
        functools.partial(_fox_sample_kernel, pp=pp, n_steps=n_steps, tq=tq, nh=nh),
        grid_spec=grid_spec,
        out_shape=jax.ShapeDtypeStruct((nb, rows, HEAD_DIM), F32),
        compiler_params=_cparams(("parallel", "arbitrary")),
        name="fox_sample",
    )(page_table, q2, *([kcache] * pp), *([vcache] * pp), *([cpool] * pp), *([tpool] * pp), knew, vnew, cnew)


def _outproj_kernel(*refs, n_in):
    ins, w_ref, res_ref, o_ref = refs[:n_in], refs[n_in], refs[n_in + 1], refs[n_in + 2]
    acc = res_ref[...]
    off = 0
    for r in ins:
        width = r.shape[1]
        acc = acc + jnp.dot(r[...].astype(BF16), w_ref[off:off + width, :], preferred_element_type=F32)
        off += width
    o_ref[...] = acc


def _outproj(ins, w, res, tm):
    m, d = res.shape
    row = lambda n: pl.BlockSpec((tm, n), lambda i: (i, 0))
    return pl.pallas_call(
        functools.partial(_outproj_kernel, n_in=len(ins)),
        grid=(m // tm,),
        in_specs=[row(a.shape[1]) for a in ins] + [pl.BlockSpec(w.shape, lambda i: (0, 0)), row(d)],
        out_specs=row(d),
        out_shape=jax.ShapeDtypeStruct((m, d), F32),
        compiler_params=_cparams(("parallel",)),
        name="outproj",
    )(*ins, w, res)


def _route(z):
    lane = lax.broadcasted_iota(jnp.int32, z.shape, 1)
    big = jnp.int32(1 << 20)
    neg = -jnp.inf
    rmax = lambda a: jnp.max(a, axis=-1, keepdims=True)
    rmin = lambda a: jnp.min(a, axis=-1, keepdims=True)
    is_g = lane < N_GROUPS
    zg = jnp.where(is_g, z, neg)
    gmax = rmax(zg)
    g_idx = rmin(jnp.where(zg == gmax, lane, big))
    g_w = 1.0 / jnp.sum(jnp.where(is_g, jnp.exp(zg - gmax), 0.0), axis=-1, keepdims=True)
    first = N_GROUPS + g_idx * E_PER_GROUP
    in_grp = jnp.logical_and(lane >= first, lane < first + E_PER_GROUP)
    v1 = jnp.where(in_grp, z, neg)
    top1 = rmax(v1)
    i1 = rmin(jnp.where(v1 == top1, lane, big))
    v2 = jnp.where(lane == i1, neg, v1)
    top2 = rmax(v2)
    i2 = rmin(jnp.where(v2 == top2, lane, big))
    e2 = jnp.exp(top2 - top1)
    den = 1.0 + e2
    return jnp.where(lane == i1, g_w / den, jnp.where(lane == i2, g_w * e2 / den, 0.0))


def _moe_kernel(x_ref, g_ref, wr_ref, br_ref, wg_ref, wu_ref, wd_ref, gf_ref, o_ref, xn_ref, cmb_ref, acc_ref,
                *, final_norm):
    e = pl.program_id(1)

    @pl.when(e == 0)
    def _():
        xn = _rms(x_ref[...], g_ref[...])
        xn_ref[...] = xn.astype(BF16)
        z = jnp.dot(xn, wr_ref[...], preferred_element_type=F32, precision=lax.Precision.HIGHEST) + br_ref[...]
        cmb_ref[...] = _route(z)
        acc_ref[...] = jnp.zeros(acc_ref.shape, F32)

    xn = xn_ref[...]
    gate = jnp.dot(xn, wg_ref[0], preferred_element_type=F32)
    up = jnp.dot(xn, wu_ref[0], preferred_element_type=F32)
    hid = (gate * jax.nn.sigmoid(gate) * up).astype(BF16)
    y = jnp.dot(hid, wd_ref[0], preferred_element_type=F32)
    lane = lax.broadcasted_iota(jnp.int32, cmb_ref.shape, 1)
    col = jnp.sum(jnp.where(lane == e + N_GROUPS, cmb_ref[...], 0.0), axis=-1, keepdims=True)
    acc_ref[...] = acc_ref[...] + col * y

    @pl.when(e == pl.num_programs(1) - 1)
    def _():
        out = x_ref[...] + acc_ref[...]
        o_ref[...] = _rms(out, gf_ref[...]) if final_norm else out


def _moe(x, g, wr, br, wg, wu, wd, gf, tm, final_norm):
    m, d = x.shape
    ne, _, dff = wg.shape
    row = pl.BlockSpec((tm, d), lambda i, e: (i, 0))
    full = lambda a: pl.BlockSpec(a.shape, lambda i, e: (0, 0))
    return pl.pallas_call(
        functools.partial(_moe_kernel, final_norm=final_norm),
        grid=(m // tm, ne),
        in_specs=[row, full(g), full(wr), full(br),
                  pl.BlockSpec((1, d, dff), lambda i, e: (e, 0, 0)),
                  pl.BlockSpec((1, d, dff), lambda i, e: (e, 0, 0)),
                  pl.BlockSpec((1, dff, d), lambda i, e: (e, 0, 0)),
                  full(gf)],
        out_specs=row,
        out_shape=jax.ShapeDtypeStruct((m, d), F32),
        scratch_shapes=[pltpu.VMEM((tm, d), BF16), pltpu.VMEM((tm, LANES), F32), pltpu.VMEM((tm, d), F32)],
        compiler_params=_cparams(("parallel", "arbitrary")),
        name="moe",
    )(x, g, wr, br, wg, wu, wd, gf)


def _select_topk(sc_ref, nc, n_sel, n_valid, key_axis):
    _, d0, d1 = sc_ref.shape
    static = isinstance(nc, int)
    chunk_keys = (d0, d1)[key_axis]
    kf = jnp.float32(n_sel)
    inf = jnp.inf
    key_off = lax.broadcasted_iota(jnp.int32, (d0, d1), key_axis)
    stat_shape = (1, d1) if key_axis == 0 else (d0, 1)
    acc_rows = 4 * SUBLANES if d0 % (4 * SUBLANES) == 0 else SUBLANES
    reducers = {"sum": (jnp.add, jnp.sum, 0.0), "min": (jnp.minimum, jnp.min, inf), "max": (jnp.maximum, jnp.max, -inf)}

    def fold(fn, kind):
        comb, red, init = reducers[kind]

        def narrow(a):
            if key_axis == 0:
                return red(a.reshape(d0 // acc_rows, acc_rows, d1), axis=0)
            out = a[:, :LANES]
            for b in range(1, d1 // LANES):
                out = comb(out, a[:, b * LANES:(b + 1) * LANES])
            return out

        if static:
            ch = lax.broadcasted_iota(jnp.int32, (nc, d0, d1), 0)
            part = narrow(red(fn(sc_ref[0:nc], ch), axis=0))
        else:
            part = lax.fori_loop(0, nc, lambda ch, a: comb(a, narrow(fn(sc_ref[ch], ch))),
                                 jnp.full((acc_rows, d1) if key_axis == 0 else (d0, LANES), init, F32))
        return red(part, axis=key_axis, keepdims=True)

    def count(pred):
        return fold(lambda x, ch: jnp.where(pred(x), 1.0, 0.0), "sum")

    def any_query(flag):
        return jnp.max(jnp.where(flag, 1.0, 0.0)) > 0.5

    mx = fold(lambda x, ch: x, "max")
    mn = fold(lambda x, ch: jnp.where(x > -inf, x, inf), "min")
    take_all = n_valid < kf

    def bisect(_, lh):
        lo, hi = lh
        mid = 0.5 * (lo + hi)
        ge = count(lambda x: x >= mid) >= kf
        return jnp.where(ge, mid, lo), jnp.where(ge, hi, mid)

    lo, _ = lax.fori_loop(0, BISECT_ITERS, bisect, (mn, mx))

    cand = fold(lambda x, ch: jnp.where(x >= lo, x, inf), "min")
    n_gt = count(lambda x: x > cand)

    def unsettled(state):
        return any_query(jnp.logical_and(state[1] >= kf, jnp.logical_not(take_all)))

    def walk(state):
        cd, g = state
        nxt = fold(lambda x, ch: jnp.where(x > cd, x, inf), "min")
        cd = jnp.where(g >= kf, nxt, cd)
        return cd, count(lambda x: x > cd)

    cand, n_gt = lax.while_loop(unsettled, walk, (cand, n_gt))
    thr = jnp.where(take_all, -inf, cand)
    n_gt = jnp.where(take_all, n_valid, n_gt)
    need = kf - n_gt
    n_eq = count(lambda x: x == thr)
    has_excess = any_query(jnp.logical_and(n_eq > need, jnp.logical_not(take_all)))

    def tie_cut():
        def bit_step(b, ans):
            cnd = ans + jnp.left_shift(jnp.int32(1), POS_BITS - 1 - b)
            in_front = lambda x, ch: jnp.where(x == thr, jnp.where((ch * chunk_keys + key_off) < cnd, 1.0, 0.0), 0.0)
            return jnp.where(fold(in_front, "sum") < need, cnd, ans)
        return lax.fori_loop(0, POS_BITS, bit_step, jnp.zeros(stat_shape, jnp.int32))

    cut = lax.cond(has_excess, tie_cut, lambda: jnp.full(stat_shape, (1 << POS_BITS) - 1, jnp.int32))

    def bias_of(x, ch):
        tie = jnp.where(x == thr, jnp.where((ch * chunk_keys + key_off) <= cut, 0.0, MASK_VALUE), MASK_VALUE)
        return jnp.where(x > -inf, jnp.where(x > thr, 0.0, tie), MASK_VALUE)

    if static:
        sc_ref[0:nc] = bias_of(sc_ref[0:nc], lax.broadcasted_iota(jnp.int32, (nc, d0, d1), 0))
    else:
        def write(ch, _):
            sc_ref[ch] = bias_of(sc_ref[ch], ch)
            return 0
        lax.fori_loop(0, nc, write, 0)


def _dsa_prompt_kernel(qit_ref, wit_ref, kidx_ref, qt_ref, k_ref, vt_ref, o_ref, sc_ref, m_ref, acc_ref, sa_ref, sb_ref,
                       *, tq, kc, n_sel, rep):
    i = pl.program_id(1)
    nc = ((i + 1) * tq + kc - 1) // kc
    nh = qt_ref.shape[1]
    nkv = k_ref.shape[1]
    q_pos = i * tq + lax.broadcasted_iota(jnp.int32, (kc, tq), 1)
    k_iota = lax.broadcasted_iota(jnp.int32, (kc, tq), 0)
    qi_all = jnp.concatenate([qit_ref[0, h] for h in range(H_IDX)], axis=1)
    wi_all = jnp.concatenate([wit_ref[0, h:h + 1, :] for h in range(H_IDX)], axis=1)

    def score_chunk(ch, _):
        start = pl.multiple_of(ch * kc, kc)
        d = jnp.dot(kidx_ref[0, pl.ds(start, kc), :], qi_all, preferred_element_type=F32)
        r = jnp.maximum(d, 0.0) * wi_all
        acc = r[:, :tq]
        for h in range(1, H_IDX):
            acc = acc + r[:, h * tq:(h + 1) * tq]
        sc_ref[ch] = jnp.where(start + k_iota <= q_pos, acc, -jnp.inf)
        return 0

    lax.fori_loop(0, nc, score_chunk, 0)
    n_valid = (i * tq + 1 + lax.broadcasted_iota(jnp.int32, (1, tq), 1)).astype(F32)
    _select_topk(sc_ref, nc, n_sel, n_valid, 0)

    m_ref[...] = jnp.full(m_ref.shape, MASK_VALUE, F32)
    acc_ref[...] = jnp.zeros(acc_ref.shape, F32)

    sub = kc // 2
    q_groups = lambda g: jnp.concatenate([qt_ref[0, g * rep + r] for r in range(rep)], axis=1)

    def logits(ch, half, s_ref):
        start = pl.multiple_of(ch * kc + half * sub, sub)
        for g in range(nkv):
            s_ref[g] = jnp.dot(k_ref[0, g, pl.ds(start, sub), :], q_groups(g), preferred_element_type=F32)

    def softmax_pv(ch, half, s_ref):
        start = pl.multiple_of(ch * kc + half * sub, sub)
        bias = jnp.tile(sc_ref[ch, half * sub:(half + 1) * sub, :], (1, rep))
        for g in range(nkv):
            _flash_step_t(s_ref[g] + bias, vt_ref[0, g, :, pl.ds(start, sub)], m_ref, acc_ref, g)

    logits(0, 0, sa_ref)

    def attend(ch, _):
        logits(ch, 1, sb_ref)
        softmax_pv(ch, 0, sa_ref)
        logits(jnp.minimum(ch + 1, nc - 1), 0, sa_ref)
        softmax_pv(ch, 1, sb_ref)
        return 0

    lax.fori_loop(0, nc, attend, 0)
    for h in range(0, nh, 2):
        pair = []
        for hh in (h, h + 1):
            g, r = hh // rep, hh % rep
            cols = slice(r * tq, (r + 1) * tq)
            pair.append(acc_ref[g, :HEAD_DIM, cols] / acc_ref[g, HEAD_DIM:HEAD_DIM + 1, cols])
        o_ref[0, :, h * HEAD_DIM:(h + 2) * HEAD_DIM] = jnp.concatenate(pair, axis=0).T


def _dsa_prompt(qit, wit, kidx, qt, k, vt, tq, kc, n_sel):
    n, nh, _, t = qt.shape
    nkv = k.shape[1]
    rep = nh // nkv
    vrows = vt.shape[2]
    return pl.pallas_call(
        functools.partial(_dsa_prompt_kernel, tq=tq, kc=kc, n_sel=n_sel, rep=rep),
        grid=(n, t // tq),
        in_specs=[pl.BlockSpec((1, H_IDX, D_IDX, tq), lambda b, i: (b, 0, 0, i)),
                  pl.BlockSpec((1, H_IDX, tq), lambda b, i: (b, 0, i)),
                  pl.BlockSpec((1, t, D_IDX), lambda b, i: (b, 0, 0)),
                  pl.BlockSpec((1, nh, HEAD_DIM, tq), lambda b, i: (b, 0, 0, i)),
                  pl.BlockSpec((1, nkv, t, HEAD_DIM), lambda b, i: (b, 0, 0, 0)),
                  pl.BlockSpec((1, nkv, vrows, t), lambda b, i: (b, 0, 0, 0))],
        out_specs=pl.BlockSpec((1, tq, nh * HEAD_DIM), lambda b, i: (b, i, 0)),
        out_shape=jax.ShapeDtypeStruct((n, t, nh * HEAD_DIM), F32),
        scratch_shapes=[pltpu.VMEM((t // kc, kc, tq), F32),
                        pltpu.VMEM((nkv, 1, rep * tq), F32), pltpu.VMEM((nkv, vrows, rep * tq), F32),
                        pltpu.VMEM((nkv, kc // 2, rep * tq), F32), pltpu.VMEM((nkv, kc // 2, rep * tq), F32)],
        compiler_params=_cparams(("parallel", "arbitrary")),
        name="dsa_prompt",
    )(qit, wit, kidx, qt, k, vt)


def _dsa_sample_score_kernel(pt_ref, qi_ref, wi_ref, *refs, pp, n_steps, page, tq, n_sel):
    del pt_ref
    kp = refs[:pp]
    kn_ref, bp_ref, bn_ref, sc_ref = refs[pp:]
    j = pl.program_id(1)
    n_pages = n_steps * pp
    q_off = lax.broadcasted_iota(jnp.int32, (tq, page), 0)
    k_off = lax.broadcasted_iota(jnp.int32, (tq, page), 1)

    def score(kpage, pg, new):
        d = lax.dot_general(qi_ref[0], kpage.astype(BF16), _NT, preferred_element_type=F32)
        r = (jnp.maximum(d, 0.0) * wi_ref[0]).reshape(H_IDX, tq, page)
        acc = r[0]
        for h in range(1, H_IDX):
            acc = acc + r[h]
        sc_ref[pg] = jnp.where(k_off <= q_off, acc, -jnp.inf) if new else acc

    @pl.when(j < n_steps)
    def _():
        for p in range(pp):
            score(kp[p][0], j * pp + p, False)

    @pl.when(j == n_steps)
    def _():
        score(kn_ref[0], n_pages, True)
        n_valid = (n_pages * page + 1 + lax.broadcasted_iota(jnp.int32, (tq, 1), 0)).astype(F32)
        _select_topk(sc_ref, n_pages + 1, n_sel, n_valid, 1)
        bp_ref[0] = sc_ref[0:n_pages]
        bn_ref[0] = sc_ref[n_pages]


def _dsa_sample_scores(page_table, qi, wi, kidx_pool, kidx_new, tq, n_sel, pp):
    nb, n_pages = page_table.shape
    page = kidx_pool.shape[1]
    n_steps = n_pages // pp
    rows = qi.shape[1]
    seq3 = lambda b, j, pt: (b, 0, 0)
    grid_spec = pltpu.PrefetchScalarGridSpec(
        num_scalar_prefetch=1,
        grid=(nb, n_steps + 1),
        in_specs=[pl.BlockSpec((1, rows, D_IDX), seq3), pl.BlockSpec((1, rows, 1), seq3)]
                 + _paged_specs(pp, n_steps, (1, page, D_IDX), lambda pg: (pg, 0, 0))
                 + [pl.BlockSpec((1, page, D_IDX), seq3)],
        out_specs=[pl.BlockSpec((1, n_pages, tq, page), lambda b, j, pt: (b, 0, 0, 0)),
                   pl.BlockSpec((1, tq, page), seq3)],
        scratch_shapes=[pltpu.VMEM((n_pages + 1, tq, page), F32)],
    )
    return pl.pallas_call(
        functools.partial(_dsa_sample_score_kernel, pp=pp, n_steps=n_steps, page=page, tq=tq, n_sel=n_sel),
        grid_spec=grid_spec,
        out_shape=[jax.ShapeDtypeStruct((nb, n_pages, tq, page), F32), jax.ShapeDtypeStruct((nb, tq, page), F32)],
        compiler_params=_cparams(("parallel", "arbitrary")),
        name="dsa_sample_scores",
    )(page_table, qi, wi, *([kidx_pool] * pp), kidx_new)


def _dsa_sample_attn_kernel(pt_ref, wq_ref, *refs, pp, n_steps, tq, nh, rep):
    del pt_ref
    kp, vp = refs[:pp], refs[pp:2 * pp]
    kn_ref, vn_ref, bp_ref, bn_ref, o_ref, m_ref, l_ref, acc_ref = refs[2 * pp:]
    j = pl.program_id(1)

    @pl.when(j == 0)
    def _():
        _init_softmax(m_ref, l_ref, acc_ref)

    def logits(k, bias):
        s = lax.dot_general(wq_ref[0], k.astype(BF16), _NT, preferred_element_type=F32)
        return s + jnp.tile(bias, (nh, 1))

    @pl.when(j < n_steps)
    def _():
        ss = [logits(kp[p][0], bp_ref[0, p]) for p in range(pp)]
        _softmax_update(ss, [vp[p][0].astype(BF16) for p in range(pp)], m_ref, l_ref, acc_ref)

    @pl.when(j == n_steps)
    def _():
        _softmax_update([logits(kn_ref[0], bn_ref[0])], [vn_ref[0].astype(BF16)], m_ref, l_ref, acc_ref)
        out = acc_ref[...] / l_ref[...]
        for h in range(nh):
            g = h // rep
            o_ref[0, :, h * HEAD_DIM:(h + 1) * HEAD_DIM] = out[h * tq:(h + 1) * tq, g * HEAD_DIM:(g + 1) * HEAD_DIM]


def _dsa_sample_attn(page_table, wq, kpool, vpool, knew, vnew, bias_pages, bias_new, tq, nh, pp):
    nb, n_pages = page_table.shape
    page, ckv = kpool.shape[1], kpool.shape[2]
    n_steps = n_pages // pp
    rows = wq.shape[1]
    seq3 = lambda b, j, pt: (b, 0, 0)
    kv_specs = lambda: _paged_specs(pp, n_steps, (1, page, ckv), lambda pg: (pg, 0, 0))
    grid_spec = pltpu.PrefetchScalarGridSpec(
        num_scalar_prefetch=1,
        grid=(nb, n_steps + 1),
        in_specs=[pl.BlockSpec((1, rows, ckv), seq3)] + kv_specs() + kv_specs()
                 + [pl.BlockSpec((1, page, ckv), seq3), pl.BlockSpec((1, page, ckv), seq3),
                    pl.BlockSpec((1, pp, tq, page), lambda b, j, pt: (b, jnp.minimum(j, n_steps - 1), 0, 0)),
                    pl.BlockSpec((1, tq, page), seq3)],
        out_specs=pl.BlockSpec((1, tq, nh * HEAD_DIM), seq3),
        scratch_shapes=[pltpu.VMEM((rows, 1), F32), pltpu.VMEM((rows, 1), F32), pltpu.VMEM((rows, ckv), F32)],
    )
    return pl.pallas_call(
        functools.partial(_dsa_sample_attn_kernel, pp=pp, n_steps=n_steps, tq=tq, nh=nh, rep=nh * HEAD_DIM // ckv),
        grid_spec=grid_spec,
        out_shape=jax.ShapeDtypeStruct((nb, tq, nh * HEAD_DIM), F32),
        compiler_params=_cparams(("parallel", "arbitrary")),
        name="dsa_sample_attn",
    )(page_table, wq, *([kpool] * pp), *([vpool] * pp), knew, vnew, bias_pages, bias_new)


def _pad_cols(a, n):
    return jnp.pad(a, [(0, 0)] * (a.ndim - 1) + [(0, n - a.shape[-1])])


def _rope_tables(pos):
    half = HEAD_DIM // 2
    inv = ROPE_THETA ** (-jnp.arange(half, dtype=F32) / half)
    ang = pos.astype(F32)[:, None] * inv[None, :]
    cos, sin = jnp.cos(ang), jnp.sin(ang)
    return jnp.tile(cos, (1, LANES // half)), jnp.tile(jnp.concatenate([-sin, sin], axis=1), (1, LANES // HEAD_DIM))


def _block_diag_queries(q, n_heads, n_kv):
    b, t, _ = q.shape
    qh = q.reshape(b, t, n_heads, HEAD_DIM).transpose(0, 2, 1, 3)
    onehot = jax.nn.one_hot(jnp.arange(n_heads) // (n_heads // n_kv), n_kv, dtype=q.dtype)
    w = qh[:, :, :, None, :] * onehot[None, :, None, :, None]
    return w.reshape(b, n_heads * t, n_kv * HEAD_DIM).astype(BF16)


def _lane_major(a, n, nh):
    return a.reshape(n, -1, nh, HEAD_DIM).transpose(0, 2, 3, 1).astype(BF16)


def _values_t(v, n, nh):
    vt = _lane_major(v, n, nh)
    t = vt.shape[-1]
    return jnp.concatenate([vt, jnp.ones((n, nh, 1, t), BF16), jnp.zeros((n, nh, SUBLANES - 1, t), BF16)], axis=2)


def _pad_rows(a, rows):
    return jnp.pad(a, [(0, 0), (0, rows - a.shape[1]), (0, 0)])


def _tile(m, pref):
    for t in pref:
        if m % t == 0:
            return t
    return m


def kernel(x_prompt, x_sample, state_conv, cache_fox_k, cache_fox_v, cache_fox_logf, cache_dsa_k, cache_dsa_v,
           cache_dsa_kidx, page_table, norm_mix_even, w_in_even, b_forget, w_dw, b_dw, ln_conv_g, ln_conv_b,
           w_out_even, norm_mix_odd, w_in_odd, w_out_odd, norm_ffn, w_group, b_group, w_router, b_router, w_gate,
           w_up, w_down, norm_final):
    n_p, t_p, d = x_prompt.shape
    n_s, t_s, _ = x_sample.shape
    depth = norm_ffn.shape[0]
    page = cache_fox_k.shape[2]
    n_pages = page_table.shape[1]
    past = n_pages * page
    c_a = w_dw.shape[2]
    h_b = b_forget.shape[1]
    c_b = h_b * HEAD_DIM
    kv_c = cache_dsa_k.shape[3]
    h_c = w_out_odd.shape[1] // HEAD_DIM
    nq, nkv, nqi = h_c * HEAD_DIM, kv_c * HEAD_DIM, H_IDX * D_IDX
    assert t_s == SUBLANES and depth == 2 and w_in_even.shape[0] == 1 and w_in_odd.shape[0] == 1

    mp, ms = n_p * t_p, n_s * t_s
    hp = x_prompt.reshape(mp, d)
    hs = x_sample.reshape(ms, d)
    tm_p = _tile(mp, (512, 256, 128))
    row = lambda a: a.reshape(1, -1)

    def moe_layer(h, layer, tm, final):
        wr = _pad_cols(jnp.concatenate([w_group[layer], w_router[layer]], axis=1), LANES)
        br = _pad_cols(jnp.concatenate([b_group[layer], b_router[layer]]).reshape(1, -1), LANES)
        return _moe(h, row(norm_ffn[layer]), wr, br, w_gate[layer].astype(BF16), w_up[layer].astype(BF16),
                    w_down[layer].astype(BF16), row(norm_final), tm, final)

    w_e = w_in_even[0]
    w_e = jnp.concatenate([w_e[:, :2 * c_a + 3 * c_b], _pad_cols(w_e[:, 2 * c_a + 3 * c_b:], LANES)], axis=1).astype(BF16)
    bf = _pad_cols(b_forget[0].reshape(1, -1), LANES)
    g_e = row(norm_mix_even[0])
    ag_p, q_p, k_p, v_p, lf_p = _proj_even(hp, g_e, w_e, bf, c_a, c_b, tm_p)
    ag_s, q_s, k_s, v_s, lf_s = _proj_even(hs, g_e, w_e, bf, c_a, c_b, ms)
    logf_p = lf_p[:, :h_b].reshape(n_p, t_p, h_b)
    logf_s = lf_s[:, :h_b].reshape(n_s, t_s, h_b)

    conv_args = (w_dw[0], row(b_dw[0]), row(ln_conv_g[0]), row(ln_conv_b[0]))
    a_p, cst_p = _conv_module(ag_p.reshape(n_p, t_p, 2 * c_a), jnp.zeros((n_p, HIST, c_a), F32), *conv_args,
                              _tile(t_p, (512, 256, 128)))
    a_s, cst_s = _conv_module(ag_s.reshape(n_s, t_s, 2 * c_a), state_conv[0], *conv_args, t_s)

    nblk = t_p // LANES
    lf_blocks = logf_p.reshape(n_p, nblk, LANES, h_b).transpose(0, 1, 3, 2).reshape(n_p * nblk, h_b, LANES)
    c_p = _block_cumsum(lf_blocks, nblk).reshape(n_p, nblk, h_b, LANES).transpose(0, 2, 1, 3).reshape(n_p, h_b, t_p)
    qt_p = _lane_major(q_p, n_p, h_b)
    slot = jnp.eye(2, dtype=BF16)[jnp.arange(h_b) % 2]
    qt_p = (qt_p[:, :, None] * slot[None, :, :, None, None]).reshape(n_p, h_b, 2 * HEAD_DIM, t_p)
    b_p = _fox_prompt(qt_p, k_p.reshape(n_p, t_p, c_b), _values_t(v_p, n_p, h_b),
                      jnp.broadcast_to(c_p[..., None], (n_p, h_b, t_p, LANES)),
                      _tile(t_p, (512, 256, 128)), _tile(t_p, (256, 128)))

    n_pool = cache_fox_k.shape[1]
    assert page * h_b == SUBLANES * LANES
    pp = _tile(n_pages, (PAGES_PER_STEP, 4, 2, 1))
    flat = lambda a: a.reshape(a.shape[0], 1, SUBLANES * LANES)
    cpool, tpool = _page_cumsum(cache_fox_logf[0].reshape(n_pool, SUBLANES, LANES), h_b,
                                _tile(n_pool, (64, 32, 16, 8, 4, 2)))
    lf_new = jnp.pad(logf_s, [(0, 0), (0, page - t_s), (0, 0)]).reshape(n_s, SUBLANES, LANES)
    cnew, _ = _page_cumsum(lf_new, h_b, _tile(n_s, (32, 16, 8, 4, 2)))
    q2 = q_s.reshape(n_s, t_s, h_b, HEAD_DIM).transpose(0, 2, 1, 3).reshape(n_s, h_b * t_s, HEAD_DIM).astype(BF16)
    b_s = _fox_sample(page_table, q2, cache_fox_k, cache_fox_v, flat(cpool), flat(tpool),
                      k_s.reshape(n_s, t_s, h_b, HEAD_DIM), v_s.reshape(n_s, t_s, h_b, HEAD_DIM),
                      flat(cnew)[:, :, :t_s * h_b], t_s, pp)
    b_s = b_s.reshape(n_s, h_b, t_s, HEAD_DIM).transpose(0, 2, 1, 3)

    w_oe = w_out_even[0].astype(BF16)
    hp = _outproj([a_p.reshape(mp, c_a), b_p.reshape(mp, c_b)], w_oe, hp, tm_p)
    hs = _outproj([a_s.reshape(ms, c_a), b_s.reshape(ms, c_b)], w_oe, hs, ms)
    hp = moe_layer(hp, 0, _tile(mp, (1024, 512, 256, 128)), False)
    hs = moe_layer(hs, 0, ms, False)

    w_o = w_in_odd[0]
    s4 = nq + 2 * nkv + nqi
    w_o = jnp.concatenate([w_o[:, :s4], _pad_cols(w_o[:, s4:s4 + D_IDX], LANES), _pad_cols(w_o[:, s4 + D_IDX:], LANES)],
                          axis=1).astype(BF16)
    g_o = row(norm_mix_odd[0])
    wi_scale = (H_IDX ** -0.5) * (D_IDX ** -0.5)
    cos_p, sin_p = _rope_tables(jnp.arange(t_p))
    cos_s, sin_s = _rope_tables(jnp.tile(past + jnp.arange(t_s), n_s))
    dq_p, dk_p, dv_p, dqi_p, dki_p, dwi_p = _proj_odd(hp, g_o, w_o, cos_p, sin_p, nq, nkv, nqi, wi_scale,
                                                      _tile(t_p, (512, 256, 128)))
    dq_s, dk_s, dv_s, dqi_s, dki_s, dwi_s = _proj_odd(hs, g_o, w_o, cos_s, sin_s, nq, nkv, nqi, wi_scale, ms)
    dki_p, dki_s = dki_p[:, :D_IDX], dki_s[:, :D_IDX]

    heads = lambda a, n, nh: a.reshape(n, -1, nh, HEAD_DIM).transpose(0, 2, 1, 3).astype(BF16)
    n_sel_p = min(TOPK_MAX, t_p // 4)
    o_p = _dsa_prompt(_lane_major(dqi_p, n_p, H_IDX), dwi_p[:, :H_IDX].reshape(n_p, t_p, H_IDX).transpose(0, 2, 1),
                      dki_p.reshape(n_p, t_p, D_IDX).astype(BF16), _lane_major(dq_p, n_p, h_c), heads(dk_p, n_p, kv_c),
                      _values_t(dv_p, n_p, kv_c), _tile(t_p, (128,)), _tile(t_p, (256, 128)), n_sel_p)

    n_sel_s = min(TOPK_MAX, (past + t_s) // 4)
    qi_rows = heads(dqi_s, n_s, H_IDX).reshape(n_s, H_IDX * t_s, D_IDX)
    wi_rows = dwi_s[:, :H_IDX].reshape(n_s, t_s, H_IDX).transpose(0, 2, 1).reshape(n_s, H_IDX * t_s, 1)
    bias_pages, bias_new = _dsa_sample_scores(page_table, qi_rows, wi_rows, cache_dsa_kidx[0],
                                              _pad_rows(dki_s.reshape(n_s, t_s, D_IDX), page), t_s, n_sel_s,
                                              pp)
    wq_d = _block_diag_queries(dq_s.reshape(n_s, t_s, nq), h_c, kv_c)
    o_s = _dsa_sample_attn(page_table, wq_d, cache_dsa_k[0].reshape(n_pool, page, nkv),
                           cache_dsa_v[0].reshape(n_pool, page, nkv),
                           _pad_rows(dk_s.reshape(n_s, t_s, nkv), page), _pad_rows(dv_s.reshape(n_s, t_s, nkv), page),
                           bias_pages, bias_new, t_s, h_c, pp)

    w_oo = w_out_odd[0].astype(BF16)
    hp = _outproj([o_p.reshape(mp, nq)], w_oo, hp, tm_p)
    hs = _outproj([o_s.reshape(ms, nq)], w_oo, hs, ms)
    hp = moe_layer(hp, 1, _tile(mp, (1024, 512, 256, 128)), True)
    hs = moe_layer(hs, 1, ms, True)

    return (hp.reshape(n_p, t_p, d), hs.reshape(n_s, t_s, d),
            cst_p[None], cst_s[None],
            k_p.reshape(1, n_p, t_p, h_b, HEAD_DIM), k_s.reshape(1, n_s, t_s, h_b, HEAD_DIM),
            v_p.reshape(1, n_p, t_p, h_b, HEAD_DIM), v_s.reshape(1, n_s, t_s, h_b, HEAD_DIM),
            logf_p[None], logf_s[None],
            dk_p.reshape(1, n_p, t_p, kv_c, HEAD_DIM), dk_s.reshape(1, n_s, t_s, kv_c, HEAD_DIM),
            dv_p.reshape(1, n_p, t_p, kv_c, HEAD_DIM), dv_s.reshape(1, n_s, t_s, kv_c, HEAD_DIM),
            dki_p.reshape(1, n_p, t_p, D_IDX), dki_s.reshape(1, n_s, t_s, D_IDX))
```

```python
import functools

import jax
import jax.numpy as jnp
from jax import lax
from jax.experimental import pallas as pl
from jax.experimental.pallas import tpu as pltpu

F32 = jnp.float32
BF16 = jnp.bfloat16

HEAD_DIM = 64
CONV_W = 31
H_IDX = 8
D_IDX = 64
TOPK_MAX = 256
N_GROUPS = 4
E_PER_GROUP = 4
N_EXPERTS = N_GROUPS * E_PER_GROUP
ROPE_THETA = 10000.0
RMS_EPS = 1e-6
LN_EPS = 1e-5

LANES = 128
SUBLANES = 8
VMEM_LIMIT = 56 * 1024 * 1024

MASK_VALUE = -1e30
HIST = CONV_W - 1
HIST_PAD = 32
BISECT_ITERS = 20
POS_BITS = 14
PAGES_PER_STEP = 8

_NT = (((1,), (1,)), ((), ()))


def _cparams(sem):
    return pltpu.CompilerParams(dimension_semantics=sem, vmem_limit_bytes=VMEM_LIMIT)


def _rms(x, g):
    ms = jnp.mean(x * x, axis=-1, keepdims=True)
    return x * lax.rsqrt(ms + RMS_EPS) * g


def _rope128(xb, cos, sin_signed):
    lane = lax.broadcasted_iota(jnp.int32, xb.shape, 1)
    first_half = jnp.bitwise_and(lane, HEAD_DIM - 1) < (HEAD_DIM // 2)
    rot = jnp.where(first_half, pltpu.roll(xb, LANES - HEAD_DIM // 2, 1), pltpu.roll(xb, HEAD_DIM // 2, 1))
    return xb * cos + rot * sin_signed


def _split_heads_t(tile_t):
    return tile_t.reshape(2, HEAD_DIM, tile_t.shape[1])


def _zero_padded_pair(tile_t, first_slot):
    top, bot = tile_t[:HEAD_DIM], tile_t[HEAD_DIM:]
    zero = jnp.zeros_like(top)
    place = lambda x, slot: jnp.concatenate([x, zero] if slot == 0 else [zero, x], axis=0).astype(BF16)
    return place(top, first_slot[0]), place(bot, first_slot[1])


def _proj_even_kernel(x_ref, g_ref, w_ref, bf_ref, *out_refs, c_a, c_b, lane_major):
    h = _rms(x_ref[...], g_ref[...]).astype(BF16)

    def seg(a, b):
        return jnp.dot(h, w_ref[:, a:b], preferred_element_type=F32)

    o = 2 * c_a
    z = seg(o + 3 * c_b, o + 3 * c_b + LANES) + bf_ref[...]
    logf = jnp.minimum(z, 0.0) - jnp.log1p(jnp.exp(-jnp.abs(z)))
    scale = HEAD_DIM ** -0.5
    if not lane_major:
        ag_ref, q_ref, k_ref, v_ref, lf_ref = out_refs
        ag_ref[...] = seg(0, o)
        q_ref[...] = seg(o, o + c_b) * scale
        k_ref[...] = seg(o + c_b, o + 2 * c_b)
        v_ref[...] = seg(o + 2 * c_b, o + 3 * c_b)
        lf_ref[...] = logf
        return
    ag_ref, k_ref, qt_ref, kt_ref, vt_ref, lft_ref = out_refs
    ag_ref[...] = seg(0, o)
    for b in range(c_b // LANES):
        cols = lambda base: (base + b * LANES, base + (b + 1) * LANES)
        qa, qb = _zero_padded_pair((seg(*cols(o)) * scale).T, (0, 1))
        qt_ref[0, 2 * b] = qa
        qt_ref[0, 2 * b + 1] = qb
        k = seg(*cols(o + c_b))
        k_ref[:, b * LANES:(b + 1) * LANES] = k
        kt_ref[0, 2 * b:2 * b + 2] = _split_heads_t(k.T)
        vt_ref[0, 2 * b:2 * b + 2] = _split_heads_t(seg(*cols(o + 2 * c_b)).T)
    lft_ref[0] = logf.T[:lft_ref.shape[1]]


def _proj_even(x, g, w, bf, c_a, c_b, tm, seq_len=None):
    m, d = x.shape
    nh = c_b // HEAD_DIM
    row = lambda n: pl.BlockSpec((tm, n), lambda i: (i, 0))
    full = lambda a: pl.BlockSpec(a.shape, lambda i: (0, 0))
    if seq_len is None:
        out_specs = [row(2 * c_a), row(c_b), row(c_b), row(c_b), row(LANES)]
        out_shape = [jax.ShapeDtypeStruct((m, n), F32) for n in (2 * c_a, c_b, c_b, c_b, LANES)]
    else:
        nper, n = seq_len // tm, m // seq_len
        lm = lambda rows: pl.BlockSpec((1, nh, rows, tm), lambda i: (i // nper, 0, 0, i % nper))
        out_specs = [row(2 * c_a), row(c_b), lm(2 * HEAD_DIM), lm(HEAD_DIM), lm(HEAD_DIM),
                     pl.BlockSpec((1, nh, tm), lambda i: (i // nper, 0, i % nper))]
        out_shape = [jax.ShapeDtypeStruct((m, 2 * c_a), F32), jax.ShapeDtypeStruct((m, c_b), F32),
                     jax.ShapeDtypeStruct((n, nh, 2 * HEAD_DIM, seq_len), BF16),
                     jax.ShapeDtypeStruct((n, nh, HEAD_DIM, seq_len), F32),
                     jax.ShapeDtypeStruct((n, nh, HEAD_DIM, seq_len), F32),
                     jax.ShapeDtypeStruct((n, nh, seq_len), F32)]
    return pl.pallas_call(
        functools.partial(_proj_even_kernel, c_a=c_a, c_b=c_b, lane_major=seq_len is not None),
        grid=(m // tm,),
        in_specs=[row(d), full(g), full(w), full(bf)],
        out_specs=out_specs, out_shape=out_shape,
        compiler_params=_cparams(("parallel",)),
        name="proj_even",
    )(x, g, w, bf)


def _proj_odd_kernel(x_ref, g_ref, w_ref, cos_ref, sin_ref, *out_refs, nq, nkv, nqi, wi_scale, lane_major):
    h = _rms(x_ref[...], g_ref[...]).astype(BF16)
    cos = cos_ref[...]
    sin = sin_ref[...]
    scale = HEAD_DIM ** -0.5

    def seg(a, b):
        return jnp.dot(h, w_ref[:, a:b], preferred_element_type=F32)

    def roped(a, c):
        return _rope128(seg(a + c * LANES, a + (c + 1) * LANES), cos, sin)

    o_k, o_v, o_qi = nq, nq + nkv, nq + 2 * nkv
    o_ki = o_qi + nqi
    wi = seg(o_ki + LANES, o_ki + 2 * LANES) * wi_scale
    if not lane_major:
        q_ref, k_ref, v_ref, qi_ref, ki_ref, wi_ref = out_refs
        for c in range(nq // LANES):
            q_ref[:, c * LANES:(c + 1) * LANES] = roped(0, c) * scale
        for c in range(nkv // LANES):
            k_ref[:, c * LANES:(c + 1) * LANES] = roped(o_k, c)
        v_ref[...] = seg(o_v, o_v + nkv)
        for c in range(nqi // LANES):
            qi_ref[:, c * LANES:(c + 1) * LANES] = roped(o_qi, c)
        ki_ref[...] = roped(o_ki, 0)
        wi_ref[...] = wi
        return
    qt_ref, k_ref, kt_ref, vt_ref, qit_ref, ki_ref, kit_ref, wit_ref = out_refs
    rep = nq // nkv
    for c in range(nq // LANES):
        slot = ((2 * c) // rep) % 2
        qa, qb = _zero_padded_pair((roped(0, c) * scale).T, (slot, slot))
        qt_ref[0, 2 * c] = qa
        qt_ref[0, 2 * c + 1] = qb
    for c in range(nkv // LANES):
        k = roped(o_k, c)
        k_ref[:, c * LANES:(c + 1) * LANES] = k
        kt_ref[0, 2 * c:2 * c + 2] = _split_heads_t(k.T)
        vt_ref[0, 2 * c:2 * c + 2] = _split_heads_t(seg(o_v + c * LANES, o_v + (c + 1) * LANES).T)
    for c in range(nqi // LANES):
        qa, qb = _zero_padded_pair(roped(o_qi, c).T, (0, 0))
        qit_ref[0, 2 * c] = qa
        qit_ref[0, 2 * c + 1] = qb
    ki = roped(o_ki, 0)
    ki_ref[...] = ki
    kit_ref[0] = ki.T[:D_IDX]
    wit_ref[0] = wi.T[:wit_ref.shape[1]]


def _proj_odd(x, g, w, cos, sin, nq, nkv, nqi, wi_scale, tm, seq_len=None):
    m, d = x.shape
    nper = cos.shape[0] // tm
    row = lambda n: pl.BlockSpec((tm, n), lambda i: (i, 0))
    full = lambda a: pl.BlockSpec(a.shape, lambda i: (0, 0))
    tab = pl.BlockSpec((tm, LANES), lambda i: (i % nper, 0))
    if seq_len is None:
        widths = (nq, nkv, nkv, nqi, LANES, LANES)
        out_specs = [row(n) for n in widths]
        out_shape = [jax.ShapeDtypeStruct((m, n), F32) for n in widths]
    else:
        n = m // seq_len
        lm = lambda nh, rows: pl.BlockSpec((1, nh, rows, tm), lambda i: (i // nper, 0, 0, i % nper))
        lm3 = lambda rows: pl.BlockSpec((1, rows, tm), lambda i: (i // nper, 0, i % nper))
        nhq, nhk, nhi = nq // HEAD_DIM, nkv // HEAD_DIM, nqi // D_IDX
        out_specs = [lm(nhq, 2 * HEAD_DIM), row(nkv), lm(nhk, HEAD_DIM), lm(nhk, HEAD_DIM), lm(nhi, 2 * D_IDX),
                     row(LANES), lm3(D_IDX), lm3(H_IDX)]
        out_shape = [jax.ShapeDtypeStruct((n, nhq, 2 * HEAD_DIM, seq_len), BF16), jax.ShapeDtypeStruct((m, nkv), F32),
                     jax.ShapeDtypeStruct((n, nhk, HEAD_DIM, seq_len), F32),
                     jax.ShapeDtypeStruct((n, nhk, HEAD_DIM, seq_len), F32),
                     jax.ShapeDtypeStruct((n, nhi, 2 * D_IDX, seq_len), BF16), jax.ShapeDtypeStruct((m, LANES), F32),
                     jax.ShapeDtypeStruct((n, D_IDX, seq_len), F32), jax.ShapeDtypeStruct((n, H_IDX, seq_len), F32)]
    return pl.pallas_call(
        functools.partial(_proj_odd_kernel, nq=nq, nkv=nkv, nqi=nqi, wi_scale=wi_scale, lane_major=seq_len is not None),
        grid=(m // tm,),
        in_specs=[row(d), full(g), full(w), tab, tab],
        out_specs=out_specs, out_shape=out_shape,
        compiler_params=_cparams(("parallel",)),
        name="proj_odd",
    )(x, g, w, cos, sin)


def _conv_kernel(ag_ref, st_ref, wdw_ref, bdw_ref, lng_ref, lnb_ref, out_ref, nst_ref, buf_ref, *, tt, c_a, rc):
    t = pl.program_id(1)
    lo = HIST_PAD - HIST

    @pl.when(t == 0)
    def _():
        buf_ref[lo:HIST_PAD, :] = st_ref[0]

    @pl.when(t > 0)
    def _():
        buf_ref[lo:HIST_PAD, :] = buf_ref[tt + lo:tt + HIST_PAD, :]

    ag = ag_ref[0]
    buf_ref[HIST_PAD:HIST_PAD + tt, :] = ag[:, :c_a] * jax.nn.sigmoid(ag[:, c_a:])

    bdw = bdw_ref[...]
    lng = lng_ref[...]
    lnb = lnb_ref[...]
    for r in range(tt // rc):
        acc = jnp.zeros((rc, c_a), F32) + bdw
        for j in range(CONV_W):
            s = r * rc + lo + j
            acc = acc + wdw_ref[j:j + 1, :] * buf_ref[s:s + rc, :]
        mu = jnp.mean(acc, axis=-1, keepdims=True)
        cen = acc - mu
        var = jnp.mean(cen * cen, axis=-1, keepdims=True)
        cn = cen * lax.rsqrt(var + LN_EPS) * lng + lnb
        out_ref[0, r * rc:(r + 1) * rc, :] = cn * jax.nn.sigmoid(cn)

    @pl.when(t == pl.num_programs(1) - 1)
    def _():
        nst_ref[0] = buf_ref[tt + lo:tt + HIST_PAD, :]


def _conv_module(ag, state, wdw, bdw, lng, lnb, tt):
    n, t, two_ca = ag.shape
    c_a = two_ca // 2
    rc = min(tt, 64)
    vec = lambda a: pl.BlockSpec(a.shape, lambda i, j: (0, 0))
    return pl.pallas_call(
        functools.partial(_conv_kernel, tt=tt, c_a=c_a, rc=rc),
        grid=(n, t // tt),
        in_specs=[pl.BlockSpec((1, tt, two_ca), lambda i, j: (i, j, 0)),
                  pl.BlockSpec((1, HIST, c_a), lambda i, j: (i, 0, 0)),
                  vec(wdw), vec(bdw), vec(lng), vec(lnb)],
        out_specs=[pl.BlockSpec((1, tt, c_a), lambda i, j: (i, j, 0)),
                   pl.BlockSpec((1, HIST, c_a), lambda i, j: (i, 0, 0))],
        out_shape=[jax.ShapeDtypeStruct((n, t, c_a), F32), jax.ShapeDtypeStruct((n, HIST, c_a), F32)],
        scratch_shapes=[pltpu.VMEM((HIST_PAD + tt, c_a), F32)],
        compiler_params=_cparams(("parallel", "arbitrary")),
        name="conv_module",
    )(ag, state, wdw, bdw, lng, lnb)


def _lane_cumsum(x):
    lane = lax.broadcasted_iota(jnp.int32, x.shape, 1)
    s = 1
    while s < LANES:
        x = x + jnp.where(lane >= s, pltpu.roll(x, s, 1), 0.0)
        s *= 2
    return x


def _page_cumsum_kernel(x_ref, o_ref):
    nb, h, _ = x_ref.shape
    o_ref[...] = _lane_cumsum(x_ref[...].reshape(nb * h, LANES)).reshape(nb, h, LANES)


def _page_cumsum(x, nb):
    b, h, _ = x.shape
    spec = pl.BlockSpec((nb, h, LANES), lambda i: (i, 0, 0))
    return pl.pallas_call(
        _page_cumsum_kernel,
        grid=(b // nb,),
        in_specs=[spec], out_specs=spec,
        out_shape=jax.ShapeDtypeStruct(x.shape, F32),
        compiler_params=_cparams(("parallel",)),
        name="page_cumsum",
    )(x)


def _seq_cumsum_kernel(x_ref, o_ref):
    t = x_ref.shape[3]
    nb = t // LANES
    x = x_ref[0, 0]
    loc = _lane_cumsum(jnp.concatenate([x[:, b * LANES:(b + 1) * LANES] for b in range(nb)], axis=0))
    off = jnp.zeros((1, LANES), F32)
    for b in range(nb):
        cur = loc[b:b + 1] + off
        o_ref[0, 0, b * LANES:(b + 1) * LANES, :] = jnp.broadcast_to(cur, (LANES, LANES)).T
        off = jnp.broadcast_to(cur[:, LANES - 1:LANES], (1, LANES))


def _seq_cumsum_columns(x):
    n, h, t = x.shape
    return pl.pallas_call(
        _seq_cumsum_kernel,
        grid=(n, h),
        in_specs=[pl.BlockSpec((1, 1, 1, t), lambda i, j: (i, j, 0, 0))],
        out_specs=pl.BlockSpec((1, 1, t, LANES), lambda i, j: (i, j, 0, 0)),
        out_shape=jax.ShapeDtypeStruct((n, h, t, LANES), F32),
        compiler_params=_cparams(("parallel", "parallel")),
        name="seq_cumsum",
    )(x.reshape(n, h, 1, t))


DENOM_ROWS = 16


def _flash_step_t(s_t, v_t, m_ref, acc_ref, idx):
    m_old = m_ref[idx]
    m_new = jnp.maximum(m_old, jnp.max(s_t, axis=0, keepdims=True))
    alpha = jnp.exp(m_old - m_new)
    p = jnp.exp(s_t - m_new).astype(BF16)
    kc = v_t.shape[1]
    ones_row = jnp.where(lax.broadcasted_iota(jnp.int32, (DENOM_ROWS, kc), 0) == 0, 1.0, 0.0).astype(BF16)
    v_aug = jnp.concatenate([v_t.astype(BF16), ones_row], axis=0)
    acc_ref[idx] = acc_ref[idx] * alpha + jnp.dot(v_aug, p, preferred_element_type=F32)
    m_ref[idx] = m_new


def _fox_prompt_kernel(qt_ref, k_ref, vt_ref, c_ref, o_ref, m_ref, acc_ref, sa_ref, sb_ref, *, tq, kc):
    i = pl.program_id(2)
    m_ref[...] = jnp.full(m_ref.shape, MASK_VALUE, F32)
    acc_ref[...] = jnp.zeros(acc_ref.shape, F32)
    q_pos = i * tq + lax.broadcasted_iota(jnp.int32, (kc // 2, tq), 1)
    k_iota = lax.broadcasted_iota(jnp.int32, (kc // 2, tq), 0)

    sub = kc // 2

    def logits(ch, half, s_ref):
        start = pl.multiple_of(ch * kc + half * sub, sub)
        k = k_ref[0, pl.ds(start, sub), :].astype(BF16)
        for hh in range(2):
            s_ref[hh] = jnp.dot(k, qt_ref[0, hh], preferred_element_type=F32)

    def softmax_pv(ch, half, s_ref, masked):
        start = pl.multiple_of(ch * kc + half * sub, sub)
        for hh in range(2):
            s = s_ref[hh] - jnp.tile(c_ref[0, hh, pl.ds(start, sub), :], (1, tq // LANES))
            if masked:
                s = jnp.where(start + k_iota <= q_pos, s, MASK_VALUE)
            _flash_step_t(s, vt_ref[0, hh, :, pl.ds(start, sub)], m_ref, acc_ref, hh)

    def run(lo, hi, masked):
        def body(ch, _):
            logits(ch, 1, sb_ref)
            softmax_pv(ch, 0, sa_ref, masked)
            logits(jnp.minimum(ch + 1, hi - 1), 0, sa_ref)
            softmax_pv(ch, 1, sb_ref, masked)
            return 0

        @pl.when(lo < hi)
        def _():
            logits(lo, 0, sa_ref)
            lax.fori_loop(lo, hi, body, 0)

    n_full = (i * tq) // kc
    run(0, n_full, False)
    run(n_full, n_full + tq // kc, True)
    out_t = jnp.concatenate([acc_ref[hh, :HEAD_DIM, :] / acc_ref[hh, HEAD_DIM:HEAD_DIM + 1, :] for hh in range(2)],
                            axis=0)
    for b in range(tq // LANES):
        o_ref[0, b * LANES:(b + 1) * LANES, :] = out_t[:, b * LANES:(b + 1) * LANES].T


def _fox_prompt(qt, k, vt, c_b, tq, kc):
    n, t, cb = k.shape
    hp = cb // LANES
    vrows = vt.shape[2]
    arows = vrows + DENOM_ROWS
    return pl.pallas_call(
        functools.partial(_fox_prompt_kernel, tq=tq, kc=kc),
        grid=(n, hp, t // tq),
        in_specs=[pl.BlockSpec((1, 2, LANES, tq), lambda b, h, i: (b, h, 0, i)),
                  pl.BlockSpec((1, t, LANES), lambda b, h, i: (b, 0, h)),
                  pl.BlockSpec((1, 2, vrows, t), lambda b, h, i: (b, h, 0, 0)),
                  pl.BlockSpec((1, 2, t, LANES), lambda b, h, i: (b, h, 0, 0))],
        out_specs=pl.BlockSpec((1, tq, LANES), lambda b, h, i: (b, i, h)),
        out_shape=jax.ShapeDtypeStruct((n, t, cb), F32),
        scratch_shapes=[pltpu.VMEM((2, 1, tq), F32), pltpu.VMEM((2, arows, tq), F32),
                        pltpu.VMEM((2, kc // 2, tq), F32), pltpu.VMEM((2, kc // 2, tq), F32)],
        compiler_params=_cparams(("parallel", "parallel", "arbitrary")),
        name="fox_prompt",
    )(qt, k, vt, c_b)


def _init_softmax(m_ref, l_ref, acc_ref):
    m_ref[...] = jnp.full(m_ref.shape, MASK_VALUE, F32)
    l_ref[...] = jnp.zeros(l_ref.shape, F32)
    acc_ref[...] = jnp.zeros(acc_ref.shape, F32)


def _softmax_update(ss, vts, m_ref, l_ref, acc_ref):
    m_old = m_ref[...]
    smax = ss[0]
    for s in ss[1:]:
        smax = jnp.maximum(smax, s)
    m_new = jnp.maximum(m_old, jnp.max(smax, axis=-1, keepdims=True))
    alpha = jnp.exp(m_old - m_new)
    acc = alpha * acc_ref[...]
    psum = None
    for s, vt in zip(ss, vts):
        p = jnp.exp(s - m_new)
        psum = p if psum is None else psum + p
        acc = acc + lax.dot_general(p.astype(BF16), vt, _NT, preferred_element_type=F32)
    l_new = alpha * l_ref[...] + jnp.sum(psum, axis=-1, keepdims=True)
    m_ref[...] = m_new
    l_ref[...] = l_new
    acc_ref[...] = acc


def _paged_specs(pp, n_steps, block, index_tail):
    def spec(p):
        return pl.BlockSpec(block, lambda b, j, pt: index_tail(pt[b, jnp.minimum(j, n_steps - 1) * pp + p]))
    return [spec(p) for p in range(pp)]


def _fox_sample_kernel(pt_ref, wq_ref, *refs, pp, n_steps, tq, nh):
    del pt_ref
    kp, vp, cp = refs[:pp], refs[pp:2 * pp], refs[2 * pp:3 * pp]
    kn_ref, vn_ref, cn_ref, o_ref, m_ref, l_ref, acc_ref, off_ref = refs[3 * pp:]
    j = pl.program_id(1)
    rows, feat = acc_ref.shape
    page = off_ref.shape[1]

    @pl.when(j == 0)
    def _():
        _init_softmax(m_ref, l_ref, acc_ref)
        off_ref[...] = jnp.zeros(off_ref.shape, F32)

    def logits(kt, c):
        s = jnp.dot(wq_ref[0], kt.reshape(feat, page).astype(BF16), preferred_element_type=F32)
        return s - jnp.broadcast_to(c[:, None, :], (nh, tq, page)).reshape(rows, page)

    @pl.when(j < n_steps)
    def _():
        off = off_ref[...]
        ss, vs = [], []
        for p in range(pp):
            cloc = cp[p][0]
            ss.append(logits(kp[p][0, 0], cloc + off))
            vs.append(vp[p][0, 0].reshape(feat, page).astype(BF16))
            off = off + jnp.broadcast_to(cloc[:, page - 1:page], off.shape)
        off_ref[...] = off
        _softmax_update(ss, vs, m_ref, l_ref, acc_ref)

    @pl.when(j == n_steps)
    def _():
        s = logits(kn_ref[0], cn_ref[0] + off_ref[...])
        qi = jnp.bitwise_and(lax.broadcasted_iota(jnp.int32, (rows, page), 0), tq - 1)
        ki = lax.broadcasted_iota(jnp.int32, (rows, page), 1)
        s = jnp.where(ki <= qi, s, MASK_VALUE)
        _softmax_update([s], [vn_ref[0].astype(BF16)], m_ref, l_ref, acc_ref)
        out = acc_ref[...] / l_ref[...]
        for h in range(nh):
            o_ref[0, :, h * HEAD_DIM:(h + 1) * HEAD_DIM] = out[h * tq:(h + 1) * tq, h * HEAD_DIM:(h + 1) * HEAD_DIM]


def _fox_sample(page_table, wq, kcache_t, vcache_t, cpool, knew_t, vnew_t, cnew, tq, pp):
    nb, n_pages = page_table.shape
    _, _, nh, hd, page = kcache_t.shape
    feat = nh * hd
    n_steps = n_pages // pp
    rows = wq.shape[1]
    seq3 = lambda b, j, pt: (b, 0, 0)
    kv_specs = lambda: _paged_specs(pp, n_steps, (1, 1, nh, hd, page), lambda pg: (0, pg, 0, 0, 0))
    grid_spec = pltpu.PrefetchScalarGridSpec(
        num_scalar_prefetch=1,
        grid=(nb, n_steps + 1),
        in_specs=[pl.BlockSpec((1, rows, feat), seq3)] + kv_specs() + kv_specs()
                 + _paged_specs(pp, n_steps, (1, nh, page), lambda pg: (pg, 0, 0))
                 + [pl.BlockSpec((1, feat, page), seq3), pl.BlockSpec((1, feat, page), seq3),
                    pl.BlockSpec((1, nh, page), seq3)],
        out_specs=pl.BlockSpec((1, tq, feat), seq3),
        scratch_shapes=[pltpu.VMEM((rows, 1), F32), pltpu.VMEM((rows, 1), F32), pltpu.VMEM((rows, feat), F32),
                        pltpu.VMEM((nh, page), F32)],
    )
    return pl.pallas_call(
        functools.partial(_fox_sample_kernel, pp=pp, n_steps=n_steps, tq=tq, nh=nh),
        grid_spec=grid_spec,
        out_shape=jax.ShapeDtypeStruct((nb, tq, feat), F32),
        compiler_params=_cparams(("parallel", "arbitrary")),
        name="fox_sample",
    )(page_table, wq, *([kcache_t] * pp), *([vcache_t] * pp), *([cpool] * pp), knew_t, vnew_t, cnew)


def _outproj_kernel(*refs, n_in):
    ins, w_ref, res_ref, o_ref = refs[:n_in], refs[n_in], refs[n_in + 1], refs[n_in + 2]
    acc = res_ref[...]
    off = 0
    for r in ins:
        width = r.shape[1]
        acc = acc + jnp.dot(r[...].astype(BF16), w_ref[off:off + width, :], preferred_element_type=F32)
        off += width
    o_ref[...] = acc


def _outproj(ins, w, res, tm):
    m, d = res.shape
    row = lambda n: pl.BlockSpec((tm, n), lambda i: (i, 0))
    return pl.pallas_call(
        functools.partial(_outproj_kernel, n_in=len(ins)),
        grid=(m // tm,),
        in_specs=[row(a.shape[1]) for a in ins] + [pl.BlockSpec(w.shape, lambda i: (0, 0)), row(d)],
        out_specs=row(d),
        out_shape=jax.ShapeDtypeStruct((m, d), F32),
        compiler_params=_cparams(("parallel",)),
        name="outproj",
    )(*ins, w, res)


def _route(z):
    lane = lax.broadcasted_iota(jnp.int32, z.shape, 1)
    big = jnp.int32(1 << 20)
    neg = -jnp.inf
    rmax = lambda a: jnp.max(a, axis=-1, keepdims=True)
    rmin = lambda a: jnp.min(a, axis=-1, keepdims=True)
    is_g = lane < N_GROUPS
    zg = jnp.where(is_g, z, neg)
    gmax = rmax(zg)
    g_idx = rmin(jnp.where(zg == gmax, lane, big))
    g_w = 1.0 / jnp.sum(jnp.where(is_g, jnp.exp(zg - gmax), 0.0), axis=-1, keepdims=True)
    first = N_GROUPS + g_idx * E_PER_GROUP
    in_grp = jnp.logical_and(lane >= first, lane < first + E_PER_GROUP)
    v1 = jnp.where(in_grp, z, neg)
    top1 = rmax(v1)
    i1 = rmin(jnp.where(v1 == top1, lane, big))
    v2 = jnp.where(lane == i1, neg, v1)
    top2 = rmax(v2)
    i2 = rmin(jnp.where(v2 == top2, lane, big))
    e2 = jnp.exp(top2 - top1)
    den = 1.0 + e2
    return jnp.where(lane == i1, g_w / den, jnp.where(lane == i2, g_w * e2 / den, 0.0))


def _moe_kernel(x_ref, g_ref, wr_ref, br_ref, wg_ref, wu_ref, wd_ref, gf_ref, o_ref, xn_ref, cmb_ref, acc_ref,
                *, final_norm):
    e = pl.program_id(1)

    @pl.when(e == 0)
    def _():
        xn = _rms(x_ref[...], g_ref[...])
        xn_ref[...] = xn.astype(BF16)
        z = jnp.dot(xn, wr_ref[...], preferred_element_type=F32, precision=lax.Precision.HIGHEST) + br_ref[...]
        cmb_ref[...] = _route(z)
        acc_ref[...] = jnp.zeros(acc_ref.shape, F32)

    xn = xn_ref[...]
    gate = jnp.dot(xn, wg_ref[0], preferred_element_type=F32)
    up = jnp.dot(xn, wu_ref[0], preferred_element_type=F32)
    hid = (gate * jax.nn.sigmoid(gate) * up).astype(BF16)
    y = jnp.dot(hid, wd_ref[0], preferred_element_type=F32)
    lane = lax.broadcasted_iota(jnp.int32, cmb_ref.shape, 1)
    col = jnp.sum(jnp.where(lane == e + N_GROUPS, cmb_ref[...], 0.0), axis=-1, keepdims=True)
    acc_ref[...] = acc_ref[...] + col * y

    @pl.when(e == pl.num_programs(1) - 1)
    def _():
        out = x_ref[...] + acc_ref[...]
        o_ref[...] = _rms(out, gf_ref[...]) if final_norm else out


def _moe(x, g, wr, br, wg, wu, wd, gf, tm, final_norm):
    m, d = x.shape
    ne, _, dff = wg.shape
    row = pl.BlockSpec((tm, d), lambda i, e: (i, 0))
    full = lambda a: pl.BlockSpec(a.shape, lambda i, e: (0, 0))
    return pl.pallas_call(
        functools.partial(_moe_kernel, final_norm=final_norm),
        grid=(m // tm, ne),
        in_specs=[row, full(g), full(wr), full(br),
                  pl.BlockSpec((1, d, dff), lambda i, e: (e, 0, 0)),
                  pl.BlockSpec((1, d, dff), lambda i, e: (e, 0, 0)),
                  pl.BlockSpec((1, dff, d), lambda i, e: (e, 0, 0)),
                  full(gf)],
        out_specs=row,
        out_shape=jax.ShapeDtypeStruct((m, d), F32),
        scratch_shapes=[pltpu.VMEM((tm, d), BF16), pltpu.VMEM((tm, LANES), F32), pltpu.VMEM((tm, d), F32)],
        compiler_params=_cparams(("parallel", "arbitrary")),
        name="moe",
    )(x, g, wr, br, wg, wu, wd, gf)


def _select_topk(sc_ref, nc, n_sel, n_valid, key_axis):
    _, d0, d1 = sc_ref.shape
    static = isinstance(nc, int)
    chunk_keys = (d0, d1)[key_axis]
    kf = jnp.float32(n_sel)
    inf = jnp.inf
    key_off = lax.broadcasted_iota(jnp.int32, (d0, d1), key_axis)
    stat_shape = (1, d1) if key_axis == 0 else (d0, 1)
    acc_rows = 4 * SUBLANES if d0 % (4 * SUBLANES) == 0 else SUBLANES
    reducers = {"sum": (jnp.add, jnp.sum, 0.0), "min": (jnp.minimum, jnp.min, inf), "max": (jnp.maximum, jnp.max, -inf)}

    def fold(fn, kind):
        comb, red, init = reducers[kind]

        def narrow(a):
            if key_axis == 0:
                return red(a.reshape(d0 // acc_rows, acc_rows, d1), axis=0)
            out = a[:, :LANES]
            for b in range(1, d1 // LANES):
                out = comb(out, a[:, b * LANES:(b + 1) * LANES])
            return out

        if static:
            ch = lax.broadcasted_iota(jnp.int32, (nc, d0, d1), 0)
            part = narrow(red(fn(sc_ref[0:nc], ch), axis=0))
        else:
            part = lax.fori_loop(0, nc, lambda ch, a: comb(a, narrow(fn(sc_ref[ch], ch))),
                                 jnp.full((acc_rows, d1) if key_axis == 0 else (d0, LANES), init, F32))
        return red(part, axis=key_axis, keepdims=True)

    def count(pred):
        return fold(lambda x, ch: jnp.where(pred(x), 1.0, 0.0), "sum")

    def any_query(flag):
        return jnp.max(jnp.where(flag, 1.0, 0.0)) > 0.5

    mx = fold(lambda x, ch: x, "max")
    mn = fold(lambda x, ch: jnp.where(x > -inf, x, inf), "min")
    take_all = n_valid < kf

    def bisect(_, lh):
        lo, hi = lh
        mid = 0.5 * (lo + hi)
        ge = count(lambda x: x >= mid) >= kf
        return jnp.where(ge, mid, lo), jnp.where(ge, hi, mid)

    lo, _ = lax.fori_loop(0, BISECT_ITERS, bisect, (mn, mx))

    cand = fold(lambda x, ch: jnp.where(x >= lo, x, inf), "min")
    n_gt = count(lambda x: x > cand)

    def unsettled(state):
        return any_query(jnp.logical_and(state[1] >= kf, jnp.logical_not(take_all)))

    def walk(state):
        cd, g = state
        nxt = fold(lambda x, ch: jnp.where(x > cd, x, inf), "min")
        cd = jnp.where(g >= kf, nxt, cd)
        return cd, count(lambda x: x > cd)

    cand, n_gt = lax.while_loop(unsettled, walk, (cand, n_gt))
    thr = jnp.where(take_all, -inf, cand)
    n_gt = jnp.where(take_all, n_valid, n_gt)
    need = kf - n_gt
    n_eq = count(lambda x: x == thr)
    has_excess = any_query(jnp.logical_and(n_eq > need, jnp.logical_not(take_all)))

    def tie_cut():
        def bit_step(b, ans):
            cnd = ans + jnp.left_shift(jnp.int32(1), POS_BITS - 1 - b)
            in_front = lambda x, ch: jnp.where(x == thr, jnp.where((ch * chunk_keys + key_off) < cnd, 1.0, 0.0), 0.0)
            return jnp.where(fold(in_front, "sum") < need, cnd, ans)
        return lax.fori_loop(0, POS_BITS, bit_step, jnp.zeros(stat_shape, jnp.int32))

    cut = lax.cond(has_excess, tie_cut, lambda: jnp.full(stat_shape, (1 << POS_BITS) - 1, jnp.int32))

    def bias_of(x, ch):
        tie = jnp.where(x == thr, jnp.where((ch * chunk_keys + key_off) <= cut, 0.0, MASK_VALUE), MASK_VALUE)
        return jnp.where(x > -inf, jnp.where(x > thr, 0.0, tie), MASK_VALUE)

    if static:
        sc_ref[0:nc] = bias_of(sc_ref[0:nc], lax.broadcasted_iota(jnp.int32, (nc, d0, d1), 0))
    else:
        def write(ch, _):
            sc_ref[ch] = bias_of(sc_ref[ch], ch)
            return 0
        lax.fori_loop(0, nc, write, 0)


def _dsa_prompt_kernel(qit_ref, wit_ref, kidx_ref, qt_ref, k_ref, vt_ref, o_ref, sc_ref, m_ref, acc_ref, sa_ref, sb_ref,
                       *, tq, kc, n_sel, rep):
    i = pl.program_id(1)
    nc = ((i + 1) * tq + kc - 1) // kc
    nh = qt_ref.shape[1]
    nkv = vt_ref.shape[1]
    q_pos = i * tq + lax.broadcasted_iota(jnp.int32, (kc, tq), 1)
    k_iota = lax.broadcasted_iota(jnp.int32, (kc, tq), 0)
    qi_all = jnp.concatenate([qit_ref[0, h] for h in range(H_IDX)], axis=1)
    wi_all = jnp.concatenate([wit_ref[0, h:h + 1, :] for h in range(H_IDX)], axis=1)

    def score_chunk(ch, _):
        start = pl.multiple_of(ch * kc, kc)
        d = jnp.dot(kidx_ref[0, pl.ds(start, kc), :].astype(BF16), qi_all, preferred_element_type=F32)
        r = jnp.maximum(d, 0.0) * wi_all
        acc = r[:, :tq]
        for h in range(1, H_IDX):
            acc = acc + r[:, h * tq:(h + 1) * tq]
        sc_ref[ch] = jnp.where(start + k_iota <= q_pos, acc, -jnp.inf)
        return 0

    lax.fori_loop(0, nc, score_chunk, 0)
    n_valid = (i * tq + 1 + lax.broadcasted_iota(jnp.int32, (1, tq), 1)).astype(F32)
    _select_topk(sc_ref, nc, n_sel, n_valid, 0)

    m_ref[...] = jnp.full(m_ref.shape, MASK_VALUE, F32)
    acc_ref[...] = jnp.zeros(acc_ref.shape, F32)

    sub = kc // 2
    q_groups = lambda g: jnp.concatenate([qt_ref[0, g * rep + r] for r in range(rep)], axis=1)

    def logits(ch, half, s_ref):
        start = pl.multiple_of(ch * kc + half * sub, sub)
        for g in range(nkv):
            kblk = k_ref[0, pl.ds(start, sub), (g // 2) * LANES:(g // 2 + 1) * LANES].astype(BF16)
            s_ref[g] = jnp.dot(kblk, q_groups(g), preferred_element_type=F32)

    def softmax_pv(ch, half, s_ref):
        start = pl.multiple_of(ch * kc + half * sub, sub)
        bias = jnp.tile(sc_ref[ch, half * sub:(half + 1) * sub, :], (1, rep))
        for g in range(nkv):
            _flash_step_t(s_ref[g] + bias, vt_ref[0, g, :, pl.ds(start, sub)], m_ref, acc_ref, g)

    logits(0, 0, sa_ref)

    def attend(ch, _):
        logits(ch, 1, sb_ref)
        softmax_pv(ch, 0, sa_ref)
        logits(jnp.minimum(ch + 1, nc - 1), 0, sa_ref)
        softmax_pv(ch, 1, sb_ref)
        return 0

    lax.fori_loop(0, nc, attend, 0)
    for h in range(0, nh, 2):
        pair = []
        for hh in (h, h + 1):
            g, r = hh // rep, hh % rep
            cols = slice(r * tq, (r + 1) * tq)
            pair.append(acc_ref[g, :HEAD_DIM, cols] / acc_ref[g, HEAD_DIM:HEAD_DIM + 1, cols])
        o_ref[0, :, h * HEAD_DIM:(h + 2) * HEAD_DIM] = jnp.concatenate(pair, axis=0).T


def _dsa_prompt(qit, wit, kidx, qt, k, vt, tq, kc, n_sel):
    n, nh, _, t = qt.shape
    nkv = vt.shape[1]
    rep = nh // nkv
    vrows = vt.shape[2]
    arows = vrows + DENOM_ROWS
    return pl.pallas_call(
        functools.partial(_dsa_prompt_kernel, tq=tq, kc=kc, n_sel=n_sel, rep=rep),
        grid=(n, t // tq),
        in_specs=[pl.BlockSpec((1, H_IDX, 2 * D_IDX, tq), lambda b, i: (b, 0, 0, i)),
                  pl.BlockSpec((1, H_IDX, tq), lambda b, i: (b, 0, i)),
                  pl.BlockSpec((1, t, 2 * D_IDX), lambda b, i: (b, 0, 0)),
                  pl.BlockSpec((1, nh, 2 * HEAD_DIM, tq), lambda b, i: (b, 0, 0, i)),
                  pl.BlockSpec((1, t, nkv * HEAD_DIM), lambda b, i: (b, 0, 0)),
                  pl.BlockSpec((1, nkv, vrows, t), lambda b, i: (b, 0, 0, 0))],
        out_specs=pl.BlockSpec((1, tq, nh * HEAD_DIM), lambda b, i: (b, i, 0)),
        out_shape=jax.ShapeDtypeStruct((n, t, nh * HEAD_DIM), F32),
        scratch_shapes=[pltpu.VMEM((t // kc, kc, tq), F32),
                        pltpu.VMEM((nkv, 1, rep * tq), F32), pltpu.VMEM((nkv, arows, rep * tq), F32),
                        pltpu.VMEM((nkv, kc // 2, rep * tq), F32), pltpu.VMEM((nkv, kc // 2, rep * tq), F32)],
        compiler_params=_cparams(("parallel", "arbitrary")),
        name="dsa_prompt",
    )(qit, wit, kidx, qt, k, vt)


def _dsa_sample_score_kernel(pt_ref, qi_ref, wi_ref, *refs, pp, n_steps, page, tq, n_sel):
    del pt_ref
    kp = refs[:pp]
    kn_ref, bp_ref, bn_ref, sc_ref = refs[pp:]
    j = pl.program_id(1)
    n_pages = n_steps * pp
    q_off = lax.broadcasted_iota(jnp.int32, (tq, page), 0)
    k_off = lax.broadcasted_iota(jnp.int32, (tq, page), 1)

    def score(kpage_t, pg, new):
        d = jnp.dot(qi_ref[0], kpage_t.astype(BF16), preferred_element_type=F32)
        r = (jnp.maximum(d, 0.0) * wi_ref[0]).reshape(H_IDX, tq, page)
        acc = r[0]
        for h in range(1, H_IDX):
            acc = acc + r[h]
        sc_ref[pg] = jnp.where(k_off <= q_off, acc, -jnp.inf) if new else acc

    @pl.when(j < n_steps)
    def _():
        for p in range(pp):
            score(kp[p][0, 0], j * pp + p, False)

    @pl.when(j == n_steps)
    def _():
        score(kn_ref[0], n_pages, True)
        n_valid = (n_pages * page + 1 + lax.broadcasted_iota(jnp.int32, (tq, 1), 0)).astype(F32)
        _select_topk(sc_ref, n_pages + 1, n_sel, n_valid, 1)
        bp_ref[0] = sc_ref[0:n_pages]
        bn_ref[0] = sc_ref[n_pages]


def _dsa_sample_scores(page_table, qi, wi, kidx_cache_t, kidx_new_t, tq, n_sel, pp):
    nb, n_pages = page_table.shape
    page = kidx_cache_t.shape[3]
    n_steps = n_pages // pp
    rows = qi.shape[1]
    seq3 = lambda b, j, pt: (b, 0, 0)
    grid_spec = pltpu.PrefetchScalarGridSpec(
        num_scalar_prefetch=1,
        grid=(nb, n_steps + 1),
        in_specs=[pl.BlockSpec((1, rows, D_IDX), seq3), pl.BlockSpec((1, rows, 1), seq3)]
                 + _paged_specs(pp, n_steps, (1, 1, D_IDX, page), lambda pg: (0, pg, 0, 0))
                 + [pl.BlockSpec((1, D_IDX, page), seq3)],
        out_specs=[pl.BlockSpec((1, n_pages, tq, page), lambda b, j, pt: (b, 0, 0, 0)),
                   pl.BlockSpec((1, tq, page), seq3)],
        scratch_shapes=[pltpu.VMEM((n_pages + 1, tq, page), F32)],
    )
    return pl.pallas_call(
        functools.partial(_dsa_sample_score_kernel, pp=pp, n_steps=n_steps, page=page, tq=tq, n_sel=n_sel),
        grid_spec=grid_spec,
        out_shape=[jax.ShapeDtypeStruct((nb, n_pages, tq, page), F32), jax.ShapeDtypeStruct((nb, tq, page), F32)],
        compiler_params=_cparams(("parallel", "arbitrary")),
        name="dsa_sample_scores",
    )(page_table, qi, wi, *([kidx_cache_t] * pp), kidx_new_t)


def _dsa_sample_attn_kernel(pt_ref, wq_ref, *refs, pp, n_steps, tq, nh, rep):
    del pt_ref
    kp, vp = refs[:pp], refs[pp:2 * pp]
    kn_ref, vn_ref, bp_ref, bn_ref, o_ref, m_ref, l_ref, acc_ref = refs[2 * pp:]
    j = pl.program_id(1)

    @pl.when(j == 0)
    def _():
        _init_softmax(m_ref, l_ref, acc_ref)

    ckv = acc_ref.shape[1]

    def logits(kt, bias):
        s = jnp.dot(wq_ref[0], kt.reshape(ckv, -1).astype(BF16), preferred_element_type=F32)
        return s + jnp.tile(bias, (nh, 1))

    @pl.when(j < n_steps)
    def _():
        ss = [logits(kp[p][0, 0], bp_ref[0, p]) for p in range(pp)]
        vts = [vp[p][0, 0].reshape(ckv, -1).astype(BF16) for p in range(pp)]
        _softmax_update(ss, vts, m_ref, l_ref, acc_ref)

    @pl.when(j == n_steps)
    def _():
        _softmax_update([logits(kn_ref[0], bn_ref[0])], [vn_ref[0].astype(BF16)], m_ref, l_ref, acc_ref)
        out = acc_ref[...] / l_ref[...]
        for h in range(nh):
            g = h // rep
            o_ref[0, :, h * HEAD_DIM:(h + 1) * HEAD_DIM] = out[h * tq:(h + 1) * tq, g * HEAD_DIM:(g + 1) * HEAD_DIM]


def _dsa_sample_attn(page_table, wq, kcache_t, vcache_t, knew_t, vnew_t, bias_pages, bias_new, tq, nh, pp):
    nb, n_pages = page_table.shape
    _, _, nkv, hd, page = kcache_t.shape
    ckv = nkv * hd
    n_steps = n_pages // pp
    rows = wq.shape[1]
    seq3 = lambda b, j, pt: (b, 0, 0)
    kv_specs = lambda: _paged_specs(pp, n_steps, (1, 1, nkv, hd, page), lambda pg: (0, pg, 0, 0, 0))
    grid_spec = pltpu.PrefetchScalarGridSpec(
        num_scalar_prefetch=1,
        grid=(nb, n_steps + 1),
        in_specs=[pl.BlockSpec((1, rows, ckv), seq3)] + kv_specs() + kv_specs()
                 + [pl.BlockSpec((1, ckv, page), seq3), pl.BlockSpec((1, ckv, page), seq3),
                    pl.BlockSpec((1, pp, tq, page), lambda b, j, pt: (b, jnp.minimum(j, n_steps - 1), 0, 0)),
                    pl.BlockSpec((1, tq, page), seq3)],
        out_specs=pl.BlockSpec((1, tq, nh * HEAD_DIM), seq3),
        scratch_shapes=[pltpu.VMEM((rows, 1), F32), pltpu.VMEM((rows, 1), F32), pltpu.VMEM((rows, ckv), F32)],
    )
    return pl.pallas_call(
        functools.partial(_dsa_sample_attn_kernel, pp=pp, n_steps=n_steps, tq=tq, nh=nh, rep=nh * HEAD_DIM // ckv),
        grid_spec=grid_spec,
        out_shape=jax.ShapeDtypeStruct((nb, tq, nh * HEAD_DIM), F32),
        compiler_params=_cparams(("parallel", "arbitrary")),
        name="dsa_sample_attn",
    )(page_table, wq, *([kcache_t] * pp), *([vcache_t] * pp), knew_t, vnew_t, bias_pages, bias_new)


def _pad_cols(a, n):
    return jnp.pad(a, [(0, 0)] * (a.ndim - 1) + [(0, n - a.shape[-1])])


def _rope_tables(pos):
    half = HEAD_DIM // 2
    inv = ROPE_THETA ** (-jnp.arange(half, dtype=F32) / half)
    ang = pos.astype(F32)[:, None] * inv[None, :]
    cos, sin = jnp.cos(ang), jnp.sin(ang)
    return jnp.tile(cos, (1, LANES // half)), jnp.tile(jnp.concatenate([-sin, sin], axis=1), (1, LANES // HEAD_DIM))


def _block_diag_queries(q, n_heads, n_kv):
    b, t, _ = q.shape
    qh = q.reshape(b, t, n_heads, HEAD_DIM).transpose(0, 2, 1, 3)
    onehot = jax.nn.one_hot(jnp.arange(n_heads) // (n_heads // n_kv), n_kv, dtype=q.dtype)
    w = qh[:, :, :, None, :] * onehot[None, :, None, :, None]
    return w.reshape(b, n_heads * t, n_kv * HEAD_DIM).astype(BF16)


def _tile(m, pref):
    for t in pref:
        if m % t == 0:
            return t
    return m


def kernel(x_prompt, x_sample, state_conv, cache_fox_k, cache_fox_v, cache_fox_logf, cache_dsa_k, cache_dsa_v,
           cache_dsa_kidx, page_table, norm_mix_even, w_in_even, b_forget, w_dw, b_dw, ln_conv_g, ln_conv_b,
           w_out_even, norm_mix_odd, w_in_odd, w_out_odd, norm_ffn, w_group, b_group, w_router, b_router, w_gate,
           w_up, w_down, norm_final):
    n_p, t_p, d = x_prompt.shape
    n_s, t_s, _ = x_sample.shape
    depth = norm_ffn.shape[0]
    page = cache_fox_k.shape[2]
    n_pages = page_table.shape[1]
    past = n_pages * page
    c_a = w_dw.shape[2]
    h_b = b_forget.shape[1]
    c_b = h_b * HEAD_DIM
    kv_c = cache_dsa_k.shape[3]
    h_c = w_out_odd.shape[1] // HEAD_DIM
    nq, nkv, nqi = h_c * HEAD_DIM, kv_c * HEAD_DIM, H_IDX * D_IDX
    assert t_s == SUBLANES and depth == 2 and w_in_even.shape[0] == 1 and w_in_odd.shape[0] == 1

    mp, ms = n_p * t_p, n_s * t_s
    hp = x_prompt.reshape(mp, d)
    hs = x_sample.reshape(ms, d)
    tm_p = _tile(mp, (512, 256, 128))
    row = lambda a: a.reshape(1, -1)

    def moe_layer(h, layer, tm, final):
        wr = _pad_cols(jnp.concatenate([w_group[layer], w_router[layer]], axis=1), LANES)
        br = _pad_cols(jnp.concatenate([b_group[layer], b_router[layer]]).reshape(1, -1), LANES)
        return _moe(h, row(norm_ffn[layer]), wr, br, w_gate[layer].astype(BF16), w_up[layer].astype(BF16),
                    w_down[layer].astype(BF16), row(norm_final), tm, final)

    tm_seq = _tile(t_p, (512, 256, 128))
    w_e = w_in_even[0]
    w_e = jnp.concatenate([w_e[:, :2 * c_a + 3 * c_b], _pad_cols(w_e[:, 2 * c_a + 3 * c_b:], LANES)], axis=1).astype(BF16)
    bf = _pad_cols(b_forget[0].reshape(1, -1), LANES)
    g_e = row(norm_mix_even[0])
    ag_p, k_p, qt_p, kt_p, vt_p, lft_p = _proj_even(hp, g_e, w_e, bf, c_a, c_b, tm_seq, t_p)
    ag_s, q_s, k_s, v_s, lf_s = _proj_even(hs, g_e, w_e, bf, c_a, c_b, ms)
    logf_s = lf_s[:, :h_b].reshape(n_s, t_s, h_b)

    conv_args = (w_dw[0], row(b_dw[0]), row(ln_conv_g[0]), row(ln_conv_b[0]))
    a_p, cst_p = _conv_module(ag_p.reshape(n_p, t_p, 2 * c_a), jnp.zeros((n_p, HIST, c_a), F32), *conv_args, tm_seq)
    a_s, cst_s = _conv_module(ag_s.reshape(n_s, t_s, 2 * c_a), state_conv[0], *conv_args, t_s)

    b_p = _fox_prompt(qt_p, k_p.reshape(n_p, t_p, c_b), vt_p, _seq_cumsum_columns(lft_p),
                      _tile(t_p, (512, 256, 128)), _tile(t_p, (256, 128)))

    n_pool = cache_fox_k.shape[1]
    pp = _tile(n_pages, (PAGES_PER_STEP, 4, 2, 1))
    keys_minor = lambda a: jnp.moveaxis(a, 2, -1)
    new_t = lambda a, f: _pad_cols(a.reshape(n_s, t_s, f).transpose(0, 2, 1), page)
    cpool = _page_cumsum(keys_minor(cache_fox_logf)[0], _tile(n_pool, (64, 32, 16, 8, 4, 2)))
    cnew = _page_cumsum(new_t(logf_s, h_b), _tile(n_s, (32, 16, 8, 4, 2)))
    b_s = _fox_sample(page_table, _block_diag_queries(q_s.reshape(n_s, t_s, c_b), h_b, h_b),
                      keys_minor(cache_fox_k), keys_minor(cache_fox_v), cpool,
                      new_t(k_s, c_b), new_t(v_s, c_b), cnew, t_s, pp)

    w_oe = w_out_even[0].astype(BF16)
    hp = _outproj([a_p.reshape(mp, c_a), b_p.reshape(mp, c_b)], w_oe, hp, tm_p)
    hs = _outproj([a_s.reshape(ms, c_a), b_s.reshape(ms, c_b)], w_oe, hs, ms)
    hp = moe_layer(hp, 0, _tile(mp, (1024, 512, 256, 128)), False)
    hs = moe_layer(hs, 0, ms, False)

    w_o = w_in_odd[0]
    s4 = nq + 2 * nkv + nqi
    w_o = jnp.concatenate([w_o[:, :s4], _pad_cols(w_o[:, s4:s4 + D_IDX], LANES), _pad_cols(w_o[:, s4 + D_IDX:], LANES)],
                          axis=1).astype(BF16)
    g_o = row(norm_mix_odd[0])
    wi_scale = (H_IDX ** -0.5) * (D_IDX ** -0.5)
    cos_p, sin_p = _rope_tables(jnp.arange(t_p))
    cos_s, sin_s = _rope_tables(jnp.tile(past + jnp.arange(t_s), n_s))
    dqt_p, dk_p, dkt_p, dvt_p, dqit_p, dki_p, dkit_p, dwit_p = _proj_odd(hp, g_o, w_o, cos_p, sin_p, nq, nkv, nqi,
                                                                         wi_scale, tm_seq, t_p)
    dq_s, dk_s, dv_s, dqi_s, dki_s, dwi_s = _proj_odd(hs, g_o, w_o, cos_s, sin_s, nq, nkv, nqi, wi_scale, ms)
    dki_s = dki_s[:, :D_IDX]

    n_sel_p = min(TOPK_MAX, t_p // 4)
    o_p = _dsa_prompt(dqit_p, dwit_p, dki_p.reshape(n_p, t_p, LANES), dqt_p, dk_p.reshape(n_p, t_p, nkv), dvt_p,
                      _tile(t_p, (128,)), _tile(t_p, (256, 128)), n_sel_p)

    n_sel_s = min(TOPK_MAX, (past + t_s) // 4)
    heads = lambda a, n, nh: a.reshape(n, -1, nh, HEAD_DIM).transpose(0, 2, 1, 3).astype(BF16)
    qi_rows = heads(dqi_s, n_s, H_IDX).reshape(n_s, H_IDX * t_s, D_IDX)
    wi_rows = dwi_s[:, :H_IDX].reshape(n_s, t_s, H_IDX).transpose(0, 2, 1).reshape(n_s, H_IDX * t_s, 1)
    bias_pages, bias_new = _dsa_sample_scores(page_table, qi_rows, wi_rows, keys_minor(cache_dsa_kidx),
                                              new_t(dki_s, D_IDX), t_s, n_sel_s, pp)
    o_s = _dsa_sample_attn(page_table, _block_diag_queries(dq_s.reshape(n_s, t_s, nq), h_c, kv_c),
                           keys_minor(cache_dsa_k), keys_minor(cache_dsa_v), new_t(dk_s, nkv), new_t(dv_s, nkv),
                           bias_pages, bias_new, t_s, h_c, pp)

    w_oo = w_out_odd[0].astype(BF16)
    hp = _outproj([o_p.reshape(mp, nq)], w_oo, hp, tm_p)
    hs = _outproj([o_s.reshape(ms, nq)], w_oo, hs, ms)
    hp = moe_layer(hp, 1, _tile(mp, (1024, 512, 256, 128)), True)
    hs = moe_layer(hs, 1, ms, True)

    t_minor = lambda a: jnp.moveaxis(a, -1, 1)[None]
    return (hp.reshape(n_p, t_p, d), hs.reshape(n_s, t_s, d),
            cst_p[None], cst_s[None],
            t_minor(kt_p), k_s.reshape(1, n_s, t_s, h_b, HEAD_DIM),
            t_minor(vt_p), v_s.reshape(1, n_s, t_s, h_b, HEAD_DIM),
            t_minor(lft_p), logf_s[None],
            t_minor(dkt_p), dk_s.reshape(1, n_s, t_s, kv_c, HEAD_DIM),
            t_minor(dvt_p), dv_s.reshape(1, n_s, t_s, kv_c, HEAD_DIM),
            t_minor(dkit_p), dki_s.reshape(1, n_s, t_s, D_IDX))
```

```python
import functools

import jax
import jax.numpy as jnp
from jax import lax
from jax.experimental import pallas as pl
from jax.experimental.pallas import tpu as pltpu

F32 = jnp.float32
BF16 = jnp.bfloat16

HEAD_DIM = 64
CONV_W = 31
H_IDX = 8
D_IDX = 64
TOPK_MAX = 256
N_GROUPS = 4
E_PER_GROUP = 4
N_EXPERTS = N_GROUPS * E_PER_GROUP
ROPE_THETA = 10000.0
RMS_EPS = 1e-6
LN_EPS = 1e-5

LANES = 128
SUBLANES = 8
VMEM_LIMIT = 56 * 1024 * 1024

MASK_VALUE = -1e30
HIST = CONV_W - 1
HIST_PAD = 32
BISECT_ITERS = 20
POS_BITS = 14
PAGES_PER_STEP = 8

_NT = (((1,), (1,)), ((), ()))


def _cparams(sem):
    return pltpu.CompilerParams(dimension_semantics=sem, vmem_limit_bytes=VMEM_LIMIT)


def _rms(x, g):
    ms = jnp.mean(x * x, axis=-1, keepdims=True)
    return x * lax.rsqrt(ms + RMS_EPS) * g


def _rope128(xb, cos, sin_signed):
    lane = lax.broadcasted_iota(jnp.int32, xb.shape, 1)
    first_half = jnp.bitwise_and(lane, HEAD_DIM - 1) < (HEAD_DIM // 2)
    rot = jnp.where(first_half, pltpu.roll(xb, LANES - HEAD_DIM // 2, 1), pltpu.roll(xb, HEAD_DIM // 2, 1))
    return xb * cos + rot * sin_signed


def _split_heads_t(tile_t):
    return tile_t.reshape(2, HEAD_DIM, tile_t.shape[1])


def _zero_padded_pair(tile_t, first_slot):
    top, bot = tile_t[:HEAD_DIM], tile_t[HEAD_DIM:]
    zero = jnp.zeros_like(top)
    place = lambda x, slot: jnp.concatenate([x, zero] if slot == 0 else [zero, x], axis=0).astype(BF16)
    return place(top, first_slot[0]), place(bot, first_slot[1])


def _proj_even_kernel(x_ref, g_ref, w_ref, bf_ref, *out_refs, c_a, c_b, lane_major):
    h = _rms(x_ref[...], g_ref[...]).astype(BF16)

    def seg(a, b):
        return jnp.dot(h, w_ref[:, a:b], preferred_element_type=F32)

    o = 2 * c_a
    z = seg(o + 3 * c_b, o + 3 * c_b + LANES) + bf_ref[...]
    logf = jnp.minimum(z, 0.0) - jnp.log1p(jnp.exp(-jnp.abs(z)))
    scale = HEAD_DIM ** -0.5
    if not lane_major:
        ag_ref, q_ref, k_ref, v_ref, lf_ref = out_refs
        ag_ref[...] = seg(0, o)
        q_ref[...] = seg(o, o + c_b) * scale
        k_ref[...] = seg(o + c_b, o + 2 * c_b)
        v_ref[...] = seg(o + 2 * c_b, o + 3 * c_b)
        lf_ref[...] = logf
        return
    ag_ref, k_ref, qt_ref, kt_ref, vt_ref, lft_ref = out_refs
    ag_ref[...] = seg(0, o)
    for b in range(c_b // LANES):
        cols = lambda base: (base + b * LANES, base + (b + 1) * LANES)
        qa, qb = _zero_padded_pair((seg(*cols(o)) * scale).T, (0, 1))
        qt_ref[0, 2 * b] = qa
        qt_ref[0, 2 * b + 1] = qb
        k = seg(*cols(o + c_b))
        k_ref[:, b * LANES:(b + 1) * LANES] = k
        kt_ref[0, 2 * b:2 * b + 2] = _split_heads_t(k.T)
        vt_ref[0, 2 * b:2 * b + 2] = _split_heads_t(seg(*cols(o + 2 * c_b)).T)
    lft_ref[0] = logf.T[:lft_ref.shape[1]]


def _proj_even(x, g, w, bf, c_a, c_b, tm, seq_len=None):
    m, d = x.shape
    nh = c_b // HEAD_DIM
    row = lambda n: pl.BlockSpec((tm, n), lambda i: (i, 0))
    full = lambda a: pl.BlockSpec(a.shape, lambda i: (0, 0))
    if seq_len is None:
        out_specs = [row(2 * c_a), row(c_b), row(c_b), row(c_b), row(LANES)]
        out_shape = [jax.ShapeDtypeStruct((m, n), F32) for n in (2 * c_a, c_b, c_b, c_b, LANES)]
    else:
        nper, n = seq_len // tm, m // seq_len
        lm = lambda rows: pl.BlockSpec((1, nh, rows, tm), lambda i: (i // nper, 0, 0, i % nper))
        out_specs = [row(2 * c_a), row(c_b), lm(2 * HEAD_DIM), lm(HEAD_DIM), lm(HEAD_DIM),
                     pl.BlockSpec((1, nh, tm), lambda i: (i // nper, 0, i % nper))]
        out_shape = [jax.ShapeDtypeStruct((m, 2 * c_a), F32), jax.ShapeDtypeStruct((m, c_b), F32),
                     jax.ShapeDtypeStruct((n, nh, 2 * HEAD_DIM, seq_len), BF16),
                     jax.ShapeDtypeStruct((n, nh, HEAD_DIM, seq_len), F32),
                     jax.ShapeDtypeStruct((n, nh, HEAD_DIM, seq_len), F32),
                     jax.ShapeDtypeStruct((n, nh, seq_len), F32)]
    return pl.pallas_call(
        functools.partial(_proj_even_kernel, c_a=c_a, c_b=c_b, lane_major=seq_len is not None),
        grid=(m // tm,),
        in_specs=[row(d), full(g), full(w), full(bf)],
        out_specs=out_specs, out_shape=out_shape,
        compiler_params=_cparams(("parallel",)),
        name="proj_even",
    )(x, g, w, bf)


def _proj_odd_kernel(x_ref, g_ref, w_ref, cos_ref, sin_ref, *out_refs, nq, nkv, nqi, wi_scale, lane_major):
    h = _rms(x_ref[...], g_ref[...]).astype(BF16)
    cos = cos_ref[...]
    sin = sin_ref[...]
    scale = HEAD_DIM ** -0.5

    def seg(a, b):
        return jnp.dot(h, w_ref[:, a:b], preferred_element_type=F32)

    def roped(a, c):
        return _rope128(seg(a + c * LANES, a + (c + 1) * LANES), cos, sin)

    o_k, o_v, o_qi = nq, nq + nkv, nq + 2 * nkv
    o_ki = o_qi + nqi
    wi = seg(o_ki + LANES, o_ki + 2 * LANES) * wi_scale
    if not lane_major:
        q_ref, k_ref, v_ref, qi_ref, ki_ref, wi_ref = out_refs
        for c in range(nq // LANES):
            q_ref[:, c * LANES:(c + 1) * LANES] = roped(0, c) * scale
        for c in range(nkv // LANES):
            k_ref[:, c * LANES:(c + 1) * LANES] = roped(o_k, c)
        v_ref[...] = seg(o_v, o_v + nkv)
        for c in range(nqi // LANES):
            qi_ref[:, c * LANES:(c + 1) * LANES] = roped(o_qi, c)
        ki_ref[...] = roped(o_ki, 0)
        wi_ref[...] = wi
        return
    qt_ref, k_ref, kt_ref, vt_ref, qit_ref, ki_ref, kit_ref, wit_ref = out_refs
    rep = nq // nkv
    for c in range(nq // LANES):
        slot = ((2 * c) // rep) % 2
        qa, qb = _zero_padded_pair((roped(0, c) * scale).T, (slot, slot))
        qt_ref[0, 2 * c] = qa
        qt_ref[0, 2 * c + 1] = qb
    for c in range(nkv // LANES):
        k = roped(o_k, c)
        k_ref[:, c * LANES:(c + 1) * LANES] = k
        kt_ref[0, 2 * c:2 * c + 2] = _split_heads_t(k.T)
        vt_ref[0, 2 * c:2 * c + 2] = _split_heads_t(seg(o_v + c * LANES, o_v + (c + 1) * LANES).T)
    for c in range(nqi // LANES):
        qa, qb = _zero_padded_pair(roped(o_qi, c).T, (0, 0))
        qit_ref[0, 2 * c] = qa
        qit_ref[0, 2 * c + 1] = qb
    ki = roped(o_ki, 0)
    ki_ref[...] = ki
    kit_ref[0] = ki.T[:D_IDX]
    wit_ref[0] = wi.T[:wit_ref.shape[1]]


def _proj_odd(x, g, w, cos, sin, nq, nkv, nqi, wi_scale, tm, seq_len=None):
    m, d = x.shape
    nper = cos.shape[0] // tm
    row = lambda n: pl.BlockSpec((tm, n), lambda i: (i, 0))
    full = lambda a: pl.BlockSpec(a.shape, lambda i: (0, 0))
    tab = pl.BlockSpec((tm, LANES), lambda i: (i % nper, 0))
    if seq_len is None:
        widths = (nq, nkv, nkv, nqi, LANES, LANES)
        out_specs = [row(n) for n in widths]
        out_shape = [jax.ShapeDtypeStruct((m, n), F32) for n in widths]
    else:
        n = m // seq_len
        lm = lambda nh, rows: pl.BlockSpec((1, nh, rows, tm), lambda i: (i // nper, 0, 0, i % nper))
        lm3 = lambda rows: pl.BlockSpec((1, rows, tm), lambda i: (i // nper, 0, i % nper))
        nhq, nhk, nhi = nq // HEAD_DIM, nkv // HEAD_DIM, nqi // D_IDX
        out_specs = [lm(nhq, 2 * HEAD_DIM), row(nkv), lm(nhk, HEAD_DIM), lm(nhk, HEAD_DIM), lm(nhi, 2 * D_IDX),
                     row(LANES), lm3(D_IDX), lm3(H_IDX)]
        out_shape = [jax.ShapeDtypeStruct((n, nhq, 2 * HEAD_DIM, seq_len), BF16), jax.ShapeDtypeStruct((m, nkv), F32),
                     jax.ShapeDtypeStruct((n, nhk, HEAD_DIM, seq_len), F32),
                     jax.ShapeDtypeStruct((n, nhk, HEAD_DIM, seq_len), F32),
                     jax.ShapeDtypeStruct((n, nhi, 2 * D_IDX, seq_len), BF16), jax.ShapeDtypeStruct((m, LANES), F32),
                     jax.ShapeDtypeStruct((n, D_IDX, seq_len), F32), jax.ShapeDtypeStruct((n, H_IDX, seq_len), F32)]
    return pl.pallas_call(
        functools.partial(_proj_odd_kernel, nq=nq, nkv=nkv, nqi=nqi, wi_scale=wi_scale, lane_major=seq_len is not None),
        grid=(m // tm,),
        in_specs=[row(d), full(g), full(w), tab, tab],
        out_specs=out_specs, out_shape=out_shape,
        compiler_params=_cparams(("parallel",)),
        name="proj_odd",
    )(x, g, w, cos, sin)


def _conv_kernel(ag_ref, st_ref, wdw_ref, bdw_ref, lng_ref, lnb_ref, out_ref, nst_ref, buf_ref, *, tt, c_a, rc):
    t = pl.program_id(1)
    lo = HIST_PAD - HIST

    @pl.when(t == 0)
    def _():
        buf_ref[lo:HIST_PAD, :] = st_ref[0]

    @pl.when(t > 0)
    def _():
        buf_ref[lo:HIST_PAD, :] = buf_ref[tt + lo:tt + HIST_PAD, :]

    ag = ag_ref[0]
    buf_ref[HIST_PAD:HIST_PAD + tt, :] = ag[:, :c_a] * jax.nn.sigmoid(ag[:, c_a:])

    bdw = bdw_ref[...]
    lng = lng_ref[...]
    lnb = lnb_ref[...]
    for r in range(tt // rc):
        acc = jnp.zeros((rc, c_a), F32) + bdw
        for j in range(CONV_W):
            s = r * rc + lo + j
            acc = acc + wdw_ref[j:j + 1, :] * buf_ref[s:s + rc, :]
        mu = jnp.mean(acc, axis=-1, keepdims=True)
        cen = acc - mu
        var = jnp.mean(cen * cen, axis=-1, keepdims=True)
        cn = cen * lax.rsqrt(var + LN_EPS) * lng + lnb
        out_ref[0, r * rc:(r + 1) * rc, :] = cn * jax.nn.sigmoid(cn)

    @pl.when(t == pl.num_programs(1) - 1)
    def _():
        nst_ref[0] = buf_ref[tt + lo:tt + HIST_PAD, :]


def _conv_module(ag, state, wdw, bdw, lng, lnb, tt):
    n, t, two_ca = ag.shape
    c_a = two_ca // 2
    rc = min(tt, 64)
    vec = lambda a: pl.BlockSpec(a.shape, lambda i, j: (0, 0))
    return pl.pallas_call(
        functools.partial(_conv_kernel, tt=tt, c_a=c_a, rc=rc),
        grid=(n, t // tt),
        in_specs=[pl.BlockSpec((1, tt, two_ca), lambda i, j: (i, j, 0)),
                  pl.BlockSpec((1, HIST, c_a), lambda i, j: (i, 0, 0)),
                  vec(wdw), vec(bdw), vec(lng), vec(lnb)],
        out_specs=[pl.BlockSpec((1, tt, c_a), lambda i, j: (i, j, 0)),
                   pl.BlockSpec((1, HIST, c_a), lambda i, j: (i, 0, 0))],
        out_shape=[jax.ShapeDtypeStruct((n, t, c_a), F32), jax.ShapeDtypeStruct((n, HIST, c_a), F32)],
        scratch_shapes=[pltpu.VMEM((HIST_PAD + tt, c_a), F32)],
        compiler_params=_cparams(("parallel", "arbitrary")),
        name="conv_module",
    )(ag, state, wdw, bdw, lng, lnb)


def _lane_cumsum(x):
    lane = lax.broadcasted_iota(jnp.int32, x.shape, 1)
    s = 1
    while s < LANES:
        x = x + jnp.where(lane >= s, pltpu.roll(x, s, 1), 0.0)
        s *= 2
    return x


def _page_cumsum_kernel(x_ref, o_ref):
    nb, h, _ = x_ref.shape
    o_ref[...] = _lane_cumsum(x_ref[...].reshape(nb * h, LANES)).reshape(nb, h, LANES)


def _page_cumsum(x, nb):
    b, h, _ = x.shape
    spec = pl.BlockSpec((nb, h, LANES), lambda i: (i, 0, 0))
    return pl.pallas_call(
        _page_cumsum_kernel,
        grid=(b // nb,),
        in_specs=[spec], out_specs=spec,
        out_shape=jax.ShapeDtypeStruct(x.shape, F32),
        compiler_params=_cparams(("parallel",)),
        name="page_cumsum",
    )(x)


def _seq_cumsum_kernel(x_ref, o_ref):
    t = x_ref.shape[3]
    nb = t // LANES
    x = x_ref[0, 0]
    loc = _lane_cumsum(jnp.concatenate([x[:, b * LANES:(b + 1) * LANES] for b in range(nb)], axis=0))
    off = jnp.zeros((1, LANES), F32)
    for b in range(nb):
        cur = loc[b:b + 1] + off
        o_ref[0, 0, b * LANES:(b + 1) * LANES, :] = jnp.broadcast_to(cur, (LANES, LANES)).T
        off = jnp.broadcast_to(cur[:, LANES - 1:LANES], (1, LANES))


def _seq_cumsum_columns(x):
    n, h, t = x.shape
    return pl.pallas_call(
        _seq_cumsum_kernel,
        grid=(n, h),
        in_specs=[pl.BlockSpec((1, 1, 1, t), lambda i, j: (i, j, 0, 0))],
        out_specs=pl.BlockSpec((1, 1, t, LANES), lambda i, j: (i, j, 0, 0)),
        out_shape=jax.ShapeDtypeStruct((n, h, t, LANES), F32),
        compiler_params=_cparams(("parallel", "parallel")),
        name="seq_cumsum",
    )(x.reshape(n, h, 1, t))


DENOM_ROWS = 16


def _flash_step_t(s_t, v_t, m_ref, acc_ref, idx):
    m_old = m_ref[idx]
    m_new = jnp.maximum(m_old, jnp.max(s_t, axis=0, keepdims=True))
    alpha = jnp.exp(m_old - m_new)
    p = jnp.exp(s_t - m_new).astype(BF16)
    kc = v_t.shape[1]
    ones_row = jnp.where(lax.broadcasted_iota(jnp.int32, (DENOM_ROWS, kc), 0) == 0, 1.0, 0.0).astype(BF16)
    v_aug = jnp.concatenate([v_t.astype(BF16), ones_row], axis=0)
    acc_ref[idx] = acc_ref[idx] * alpha + jnp.dot(v_aug, p, preferred_element_type=F32)
    m_ref[idx] = m_new


def _fox_prompt_kernel(qt_ref, k_ref, vt_ref, c_ref, o_ref, m_ref, acc_ref, sa_ref, sb_ref, *, tq, kc):
    i = pl.program_id(2)
    m_ref[...] = jnp.full(m_ref.shape, MASK_VALUE, F32)
    acc_ref[...] = jnp.zeros(acc_ref.shape, F32)
    q_pos = i * tq + lax.broadcasted_iota(jnp.int32, (kc // 2, tq), 1)
    k_iota = lax.broadcasted_iota(jnp.int32, (kc // 2, tq), 0)

    sub = kc // 2

    def logits(ch, half, s_ref):
        start = pl.multiple_of(ch * kc + half * sub, sub)
        k = k_ref[0, pl.ds(start, sub), :].astype(BF16)
        for hh in range(2):
            s_ref[hh] = jnp.dot(k, qt_ref[0, hh], preferred_element_type=F32)

    def softmax_pv(ch, half, s_ref, masked):
        start = pl.multiple_of(ch * kc + half * sub, sub)
        for hh in range(2):
            s = s_ref[hh] - jnp.tile(c_ref[0, hh, pl.ds(start, sub), :], (1, tq // LANES))
            if masked:
                s = jnp.where(start + k_iota <= q_pos, s, MASK_VALUE)
            _flash_step_t(s, vt_ref[0, hh, :, pl.ds(start, sub)], m_ref, acc_ref, hh)

    def run(lo, hi, masked):
        def body(ch, _):
            logits(ch, 1, sb_ref)
            softmax_pv(ch, 0, sa_ref, masked)
            logits(jnp.minimum(ch + 1, hi - 1), 0, sa_ref)
            softmax_pv(ch, 1, sb_ref, masked)
            return 0

        @pl.when(lo < hi)
        def _():
            logits(lo, 0, sa_ref)
            lax.fori_loop(lo, hi, body, 0)

    n_full = (i * tq) // kc
    run(0, n_full, False)
    run(n_full, n_full + tq // kc, True)
    out_t = jnp.concatenate([acc_ref[hh, :HEAD_DIM, :] / acc_ref[hh, HEAD_DIM:HEAD_DIM + 1, :] for hh in range(2)],
                            axis=0)
    for b in range(tq // LANES):
        o_ref[0, b * LANES:(b + 1) * LANES, :] = out_t[:, b * LANES:(b + 1) * LANES].T


def _fox_prompt(qt, k, vt, c_b, tq, kc):
    n, t, cb = k.shape
    hp = cb // LANES
    vrows = vt.shape[2]
    arows = vrows + DENOM_ROWS
    return pl.pallas_call(
        functools.partial(_fox_prompt_kernel, tq=tq, kc=kc),
        grid=(n, hp, t // tq),
        in_specs=[pl.BlockSpec((1, 2, LANES, tq), lambda b, h, i: (b, h, 0, i)),
                  pl.BlockSpec((1, t, LANES), lambda b, h, i: (b, 0, h)),
                  pl.BlockSpec((1, 2, vrows, t), lambda b, h, i: (b, h, 0, 0)),
                  pl.BlockSpec((1, 2, t, LANES), lambda b, h, i: (b, h, 0, 0))],
        out_specs=pl.BlockSpec((1, tq, LANES), lambda b, h, i: (b, i, h)),
        out_shape=jax.ShapeDtypeStruct((n, t, cb), F32),
        scratch_shapes=[pltpu.VMEM((2, 1, tq), F32), pltpu.VMEM((2, arows, tq), F32),
                        pltpu.VMEM((2, kc // 2, tq), F32), pltpu.VMEM((2, kc // 2, tq), F32)],
        compiler_params=_cparams(("parallel", "parallel", "arbitrary")),
        name="fox_prompt",
    )(qt, k, vt, c_b)


def _init_softmax(m_ref, l_ref, acc_ref):
    m_ref[...] = jnp.full(m_ref.shape, MASK_VALUE, F32)
    l_ref[...] = jnp.zeros(l_ref.shape, F32)
    acc_ref[...] = jnp.zeros(acc_ref.shape, F32)


def _softmax_update(ss, vts, m_ref, l_ref, acc_ref):
    m_old = m_ref[...]
    smax = ss[0]
    for s in ss[1:]:
        smax = jnp.maximum(smax, s)
    m_new = jnp.maximum(m_old, jnp.max(smax, axis=-1, keepdims=True))
    alpha = jnp.exp(m_old - m_new)
    acc = alpha * acc_ref[...]
    psum = None
    for s, vt in zip(ss, vts):
        p = jnp.exp(s - m_new)
        psum = p if psum is None else psum + p
        acc = acc + lax.dot_general(p.astype(BF16), vt, _NT, preferred_element_type=F32)
    l_new = alpha * l_ref[...] + jnp.sum(psum, axis=-1, keepdims=True)
    m_ref[...] = m_new
    l_ref[...] = l_new
    acc_ref[...] = acc


def _paged_specs(pp, n_steps, block, index_tail):
    def spec(p):
        return pl.BlockSpec(block, lambda b, j, pt: index_tail(pt[b, jnp.minimum(j, n_steps - 1) * pp + p]))
    return [spec(p) for p in range(pp)]


def _fox_sample_kernel(pt_ref, wq_ref, *refs, pp, n_steps, tq, nh):
    del pt_ref
    kp, vp, cp = refs[:pp], refs[pp:2 * pp], refs[2 * pp:3 * pp]
    kn_ref, vn_ref, cn_ref, o_ref, m_ref, l_ref, acc_ref, off_ref = refs[3 * pp:]
    j = pl.program_id(1)
    rows, feat = acc_ref.shape
    page = off_ref.shape[1]

    @pl.when(j == 0)
    def _():
        _init_softmax(m_ref, l_ref, acc_ref)
        off_ref[...] = jnp.zeros(off_ref.shape, F32)

    def logits(kt, c):
        s = jnp.dot(wq_ref[0], kt.reshape(feat, page).astype(BF16), preferred_element_type=F32)
        return s - jnp.broadcast_to(c[:, None, :], (nh, tq, page)).reshape(rows, page)

    @pl.when(j < n_steps)
    def _():
        off = off_ref[...]
        ss, vs = [], []
        for p in range(pp):
            cloc = cp[p][0]
            ss.append(logits(kp[p][0, 0], cloc + off))
            vs.append(vp[p][0, 0].reshape(feat, page).astype(BF16))
            off = off + jnp.broadcast_to(cloc[:, page - 1:page], off.shape)
        off_ref[...] = off
        _softmax_update(ss, vs, m_ref, l_ref, acc_ref)

    @pl.when(j == n_steps)
    def _():
        s = logits(kn_ref[0], cn_ref[0] + off_ref[...])
        qi = jnp.bitwise_and(lax.broadcasted_iota(jnp.int32, (rows, page), 0), tq - 1)
        ki = lax.broadcasted_iota(jnp.int32, (rows, page), 1)
        s = jnp.where(ki <= qi, s, MASK_VALUE)
        _softmax_update([s], [vn_ref[0].astype(BF16)], m_ref, l_ref, acc_ref)
        out = acc_ref[...] / l_ref[...]
        for h in range(nh):
            o_ref[0, :, h * HEAD_DIM:(h + 1) * HEAD_DIM] = out[h * tq:(h + 1) * tq, h * HEAD_DIM:(h + 1) * HEAD_DIM]


def _fox_sample(page_table, wq, kcache_t, vcache_t, cpool, knew_t, vnew_t, cnew, tq, pp):
    nb, n_pages = page_table.shape
    _, _, nh, hd, page = kcache_t.shape
    feat = nh * hd
    n_steps = n_pages // pp
    rows = wq.shape[1]
    seq3 = lambda b, j, pt: (b, 0, 0)
    kv_specs = lambda: _paged_specs(pp, n_steps, (1, 1, nh, hd, page), lambda pg: (0, pg, 0, 0, 0))
    grid_spec = pltpu.PrefetchScalarGridSpec(
        num_scalar_prefetch=1,
        grid=(nb, n_steps + 1),
        in_specs=[pl.BlockSpec((1, rows, feat), seq3)] + kv_specs() + kv_specs()
                 + _paged_specs(pp, n_steps, (1, nh, page), lambda pg: (pg, 0, 0))
                 + [pl.BlockSpec((1, feat, page), seq3), pl.BlockSpec((1, feat, page), seq3),
                    pl.BlockSpec((1, nh, page), seq3)],
        out_specs=pl.BlockSpec((1, tq, feat), seq3),
        scratch_shapes=[pltpu.VMEM((rows, 1), F32), pltpu.VMEM((rows, 1), F32), pltpu.VMEM((rows, feat), F32),
                        pltpu.VMEM((nh, page), F32)],
    )
    return pl.pallas_call(
        functools.partial(_fox_sample_kernel, pp=pp, n_steps=n_steps, tq=tq, nh=nh),
        grid_spec=grid_spec,
        out_shape=jax.ShapeDtypeStruct((nb, tq, feat), F32),
        compiler_params=_cparams(("parallel", "arbitrary")),
        name="fox_sample",
    )(page_table, wq, *([kcache_t] * pp), *([vcache_t] * pp), *([cpool] * pp), knew_t, vnew_t, cnew)


def _outproj_kernel(*refs, n_in):
    ins, w_ref, res_ref, o_ref = refs[:n_in], refs[n_in], refs[n_in + 1], refs[n_in + 2]
    acc = res_ref[...]
    off = 0
    for r in ins:
        width = r.shape[1]
        acc = acc + jnp.dot(r[...].astype(BF16), w_ref[off:off + width, :], preferred_element_type=F32)
        off += width
    o_ref[...] = acc


def _outproj(ins, w, res, tm):
    m, d = res.shape
    row = lambda n: pl.BlockSpec((tm, n), lambda i: (i, 0))
    return pl.pallas_call(
        functools.partial(_outproj_kernel, n_in=len(ins)),
        grid=(m // tm,),
        in_specs=[row(a.shape[1]) for a in ins] + [pl.BlockSpec(w.shape, lambda i: (0, 0)), row(d)],
        out_specs=row(d),
        out_shape=jax.ShapeDtypeStruct((m, d), F32),
        compiler_params=_cparams(("parallel",)),
        name="outproj",
    )(*ins, w, res)


def _route(z):
    lane = lax.broadcasted_iota(jnp.int32, z.shape, 1).astype(F32)
    big = jnp.float32(1 << 20)
    neg = -jnp.inf
    rmax = lambda a: jnp.max(a, axis=-1, keepdims=True)
    rmin = lambda a: jnp.min(a, axis=-1, keepdims=True)
    is_g = lane < N_GROUPS
    zg = jnp.where(is_g, z, neg)
    gmax = rmax(zg)
    g_idx = rmin(jnp.where(zg == gmax, lane, big))
    g_w = 1.0 / jnp.sum(jnp.where(is_g, jnp.exp(zg - gmax), 0.0), axis=-1, keepdims=True)
    first = N_GROUPS + g_idx * E_PER_GROUP
    in_grp = jnp.logical_and(lane >= first, lane < first + E_PER_GROUP)
    v1 = jnp.where(in_grp, z, neg)
    top1 = rmax(v1)
    i1 = rmin(jnp.where(v1 == top1, lane, big))
    v2 = jnp.where(lane == i1, neg, v1)
    top2 = rmax(v2)
    i2 = rmin(jnp.where(v2 == top2, lane, big))
    e2 = jnp.exp(top2 - top1)
    den = 1.0 + e2
    return jnp.where(lane == i1, g_w / den, jnp.where(lane == i2, g_w * e2 / den, 0.0)), g_idx.astype(jnp.int32)


def _moe_kernel(x_ref, g_ref, wr_ref, br_ref, wg_ref, wu_ref, wd_ref, gf_ref, o_ref, xn_ref, cmb_ref, acc_ref,
                *, final_norm):
    e = pl.program_id(1)

    @pl.when(e == 0)
    def _():
        xn = _rms(x_ref[...], g_ref[...])
        xn_ref[...] = xn.astype(BF16)
        z = jnp.dot(xn, wr_ref[...], preferred_element_type=F32, precision=lax.Precision.HIGHEST) + br_ref[...]
        cmb_ref[...] = _route(z)[0]
        acc_ref[...] = jnp.zeros(acc_ref.shape, F32)

    xn = xn_ref[...]
    gate = jnp.dot(xn, wg_ref[0], preferred_element_type=F32)
    up = jnp.dot(xn, wu_ref[0], preferred_element_type=F32)
    hid = (gate * jax.nn.sigmoid(gate) * up).astype(BF16)
    y = jnp.dot(hid, wd_ref[0], preferred_element_type=F32)
    lane = lax.broadcasted_iota(jnp.int32, cmb_ref.shape, 1)
    col = jnp.sum(jnp.where(lane == e + N_GROUPS, cmb_ref[...], 0.0), axis=-1, keepdims=True)
    acc_ref[...] = acc_ref[...] + col * y

    @pl.when(e == pl.num_programs(1) - 1)
    def _():
        out = x_ref[...] + acc_ref[...]
        o_ref[...] = _rms(out, gf_ref[...]) if final_norm else out


def _moe(x, g, wr, br, wg, wu, wd, gf, tm, final_norm):
    m, d = x.shape
    ne, _, dff = wg.shape
    row = pl.BlockSpec((tm, d), lambda i, e: (i, 0))
    full = lambda a: pl.BlockSpec(a.shape, lambda i, e: (0, 0))
    return pl.pallas_call(
        functools.partial(_moe_kernel, final_norm=final_norm),
        grid=(m // tm, ne),
        in_specs=[row, full(g), full(wr), full(br),
                  pl.BlockSpec((1, d, dff), lambda i, e: (e, 0, 0)),
                  pl.BlockSpec((1, d, dff), lambda i, e: (e, 0, 0)),
                  pl.BlockSpec((1, dff, d), lambda i, e: (e, 0, 0)),
                  full(gf)],
        out_specs=row,
        out_shape=jax.ShapeDtypeStruct((m, d), F32),
        scratch_shapes=[pltpu.VMEM((tm, d), BF16), pltpu.VMEM((tm, LANES), F32), pltpu.VMEM((tm, d), F32)],
        compiler_params=_cparams(("parallel", "arbitrary")),
        name="moe",
    )(x, g, wr, br, wg, wu, wd, gf)


MOE_WINDOW = 320


def _split3(a):
    hi = a.astype(BF16)
    r1 = a - hi.astype(F32)
    mid = r1.astype(BF16)
    return hi, mid, (r1 - mid.astype(F32)).astype(BF16)


def _moe_sorted_kernel(x_ref, g_ref, wr_ref, br_ref, wg_ref, wu_ref, wd_ref, gf_ref, o_ref,
                       xs_ref, cmbs_ref, acc_ref, pt_ref, rng_ref, *, final_norm, win):
    e = pl.program_id(1)
    tm, d = x_ref.shape

    @pl.when(e == 0)
    def _():
        xn = _rms(x_ref[...], g_ref[...])
        z = jnp.dot(xn, wr_ref[...], preferred_element_type=F32, precision=lax.Precision.HIGHEST) + br_ref[...]
        cmb, g_idx = _route(z)
        lane = lax.broadcasted_iota(jnp.int32, (tm, LANES), 1)
        onehot = jnp.where(lane == g_idx, 1.0, 0.0)
        r_i = lax.broadcasted_iota(jnp.int32, (tm, tm), 0)
        c_i = lax.broadcasted_iota(jnp.int32, (tm, tm), 1)
        rows = lax.broadcasted_iota(jnp.int32, (tm, LANES), 0)
        before = onehot
        step = 1
        while step < tm:
            before = before + jnp.where(rows >= step, pltpu.roll(before, step, 0), 0.0)
            step *= 2
        before = before - onehot
        counts = jnp.sum(onehot, axis=0, keepdims=True)
        starts = _lane_cumsum(counts) - counts
        pos = jnp.sum(jnp.where(lane == g_idx, before + starts, 0.0), axis=-1, keepdims=True)
        pos_row = jnp.broadcast_to(pos, (tm, LANES)).T[0:1].astype(jnp.int32)
        perm = jnp.where(r_i == pos_row, 1.0, 0.0).astype(BF16)
        pt_ref[...] = jnp.where(c_i == pos.astype(jnp.int32), 1.0, 0.0).astype(BF16)
        xs_ref[...] = jnp.dot(perm, xn.astype(BF16), preferred_element_type=F32).astype(BF16)
        cmbs_ref[...] = sum(jnp.dot(perm, t, preferred_element_type=F32) for t in _split3(cmb))
        acc_ref[...] = jnp.zeros(acc_ref.shape, F32)
        lane1 = lax.broadcasted_iota(jnp.int32, (1, LANES), 1)
        for grp in range(N_GROUPS):
            rng_ref[grp] = jnp.sum(jnp.where(lane1 == grp, starts, 0.0)).astype(jnp.int32)
            rng_ref[N_GROUPS + grp] = jnp.sum(jnp.where(lane1 == grp, counts, 0.0)).astype(jnp.int32)

    grp = e // E_PER_GROUP
    start = rng_ref[grp]
    end = start + rng_ref[N_GROUPS + grp]
    w0 = (start // 16) * 16
    lane = lax.broadcasted_iota(jnp.int32, (win, LANES), 1)
    row = lax.broadcasted_iota(jnp.int32, (win, 1), 0)

    def window(i, _):
        lo = w0 + i * win
        ws = pl.multiple_of(jnp.minimum(lo, tm - win), 16)
        xc = xs_ref[pl.ds(ws, win), :]
        gate = jnp.dot(xc, wg_ref[0], preferred_element_type=F32)
        up = jnp.dot(xc, wu_ref[0], preferred_element_type=F32)
        hid = (gate * jax.nn.sigmoid(gate) * up).astype(BF16)
        y = jnp.dot(hid, wd_ref[0], preferred_element_type=F32)
        col = jnp.sum(jnp.where(lane == e + N_GROUPS, cmbs_ref[pl.ds(ws, win), :], 0.0), axis=-1, keepdims=True)
        srt = ws + row
        mine = jnp.logical_and(srt >= jnp.maximum(lo, start), srt < jnp.minimum(lo + win, end))
        acc_ref[pl.ds(ws, win), :] = acc_ref[pl.ds(ws, win), :] + jnp.where(mine, col, 0.0) * y
        return 0

    lax.fori_loop(0, (end - w0 + win - 1) // win, window, 0)

    @pl.when(e == pl.num_programs(1) - 1)
    def _():
        acc = acc_ref[...]
        hi = acc.astype(BF16)
        lo = (acc - hi.astype(F32)).astype(BF16)
        moe = jnp.dot(pt_ref[...], hi, preferred_element_type=F32) + jnp.dot(pt_ref[...], lo, preferred_element_type=F32)
        out = x_ref[...] + moe
        o_ref[...] = _rms(out, gf_ref[...]) if final_norm else out


def _moe_sorted(x, g, wr, br, wg, wu, wd, gf, tm, final_norm):
    m, d = x.shape
    ne, _, dff = wg.shape
    row = pl.BlockSpec((tm, d), lambda i, e: (i, 0))
    full = lambda a: pl.BlockSpec(a.shape, lambda i, e: (0, 0))
    return pl.pallas_call(
        functools.partial(_moe_sorted_kernel, final_norm=final_norm, win=MOE_WINDOW),
        grid=(m // tm, ne),
        in_specs=[row, full(g), full(wr), full(br),
                  pl.BlockSpec((1, d, dff), lambda i, e: (e, 0, 0)),
                  pl.BlockSpec((1, d, dff), lambda i, e: (e, 0, 0)),
                  pl.BlockSpec((1, dff, d), lambda i, e: (e, 0, 0)),
                  full(gf)],
        out_specs=row,
        out_shape=jax.ShapeDtypeStruct((m, d), F32),
        scratch_shapes=[pltpu.VMEM((tm, d), BF16), pltpu.VMEM((tm, LANES), F32), pltpu.VMEM((tm, d), F32),
                        pltpu.VMEM((tm, tm), BF16), pltpu.SMEM((2 * N_GROUPS,), jnp.int32)],
        compiler_params=_cparams(("parallel", "arbitrary")),
        name="moe_sorted",
    )(x, g, wr, br, wg, wu, wd, gf)


def _select_topk(sc_ref, nc, n_sel, n_valid, key_axis):
    _, d0, d1 = sc_ref.shape
    static = isinstance(nc, int)
    chunk_keys = (d0, d1)[key_axis]
    kf = jnp.float32(n_sel)
    inf = jnp.inf
    key_off = lax.broadcasted_iota(jnp.int32, (d0, d1), key_axis)
    stat_shape = (1, d1) if key_axis == 0 else (d0, 1)
    acc_rows = 4 * SUBLANES if d0 % (4 * SUBLANES) == 0 else SUBLANES
    reducers = {"sum": (jnp.add, jnp.sum, 0.0), "min": (jnp.minimum, jnp.min, inf), "max": (jnp.maximum, jnp.max, -inf)}

    def fold(fn, kind):
        comb, red, init = reducers[kind]

        def narrow(a):
            if key_axis == 0:
                return red(a.reshape(d0 // acc_rows, acc_rows, d1), axis=0)
            out = a[:, :LANES]
            for b in range(1, d1 // LANES):
                out = comb(out, a[:, b * LANES:(b + 1) * LANES])
            return out

        if static:
            ch = lax.broadcasted_iota(jnp.int32, (nc, d0, d1), 0)
            part = narrow(red(fn(sc_ref[0:nc], ch), axis=0))
        else:
            part = lax.fori_loop(0, nc, lambda ch, a: comb(a, narrow(fn(sc_ref[ch], ch))),
                                 jnp.full((acc_rows, d1) if key_axis == 0 else (d0, LANES), init, F32))
        return red(part, axis=key_axis, keepdims=True)

    def count(pred):
        return fold(lambda x, ch: jnp.where(pred(x), 1.0, 0.0), "sum")

    def any_query(flag):
        return jnp.max(jnp.where(flag, 1.0, 0.0)) > 0.5

    mx = fold(lambda x, ch: x, "max")
    mn = fold(lambda x, ch: jnp.where(x > -inf, x, inf), "min")
    take_all = n_valid < kf

    def bisect(_, lh):
        lo, hi = lh
        mid = 0.5 * (lo + hi)
        ge = count(lambda x: x >= mid) >= kf
        return jnp.where(ge, mid, lo), jnp.where(ge, hi, mid)

    lo, _ = lax.fori_loop(0, BISECT_ITERS, bisect, (mn, mx))

    cand = fold(lambda x, ch: jnp.where(x >= lo, x, inf), "min")
    n_gt = count(lambda x: x > cand)

    def unsettled(state):
        return any_query(jnp.logical_and(state[1] >= kf, jnp.logical_not(take_all)))

    def walk(state):
        cd, g = state
        nxt = fold(lambda x, ch: jnp.where(x > cd, x, inf), "min")
        cd = jnp.where(g >= kf, nxt, cd)
        return cd, count(lambda x: x > cd)

    cand, n_gt = lax.while_loop(unsettled, walk, (cand, n_gt))
    thr = jnp.where(take_all, -inf, cand)
    n_gt = jnp.where(take_all, n_valid, n_gt)
    need = kf - n_gt
    n_eq = count(lambda x: x == thr)
    has_excess = any_query(jnp.logical_and(n_eq > need, jnp.logical_not(take_all)))

    def tie_cut():
        def bit_step(b, ans):
            cnd = ans + jnp.left_shift(jnp.int32(1), POS_BITS - 1 - b)
            in_front = lambda x, ch: jnp.where(x == thr, jnp.where((ch * chunk_keys + key_off) < cnd, 1.0, 0.0), 0.0)
            return jnp.where(fold(in_front, "sum") < need, cnd, ans)
        return lax.fori_loop(0, POS_BITS, bit_step, jnp.zeros(stat_shape, jnp.int32))

    cut = lax.cond(has_excess, tie_cut, lambda: jnp.full(stat_shape, (1 << POS_BITS) - 1, jnp.int32))

    def bias_of(x, ch):
        tie = jnp.where(x == thr, jnp.where((ch * chunk_keys + key_off) <= cut, 0.0, MASK_VALUE), MASK_VALUE)
        return jnp.where(x > -inf, jnp.where(x > thr, 0.0, tie), MASK_VALUE)

    if static:
        sc_ref[0:nc] = bias_of(sc_ref[0:nc], lax.broadcasted_iota(jnp.int32, (nc, d0, d1), 0))
    else:
        def write(ch, _):
            sc_ref[ch] = bias_of(sc_ref[ch], ch)
            return 0
        lax.fori_loop(0, nc, write, 0)


def _dsa_prompt_kernel(qit_ref, wit_ref, kidx_ref, qt_ref, k_ref, vt_ref, o_ref, sc_ref, m_ref, acc_ref, sa_ref, sb_ref,
                       *, tq, kc, n_sel, rep):
    i = pl.program_id(1)
    nc = ((i + 1) * tq + kc - 1) // kc
    nh = qt_ref.shape[1]
    nkv = vt_ref.shape[1]
    q_pos = i * tq + lax.broadcasted_iota(jnp.int32, (kc, tq), 1)
    k_iota = lax.broadcasted_iota(jnp.int32, (kc, tq), 0)
    qi_all = jnp.concatenate([qit_ref[0, h] for h in range(H_IDX)], axis=1)
    wi_all = jnp.concatenate([wit_ref[0, h:h + 1, :] for h in range(H_IDX)], axis=1)

    def score_chunk(ch, _):
        start = pl.multiple_of(ch * kc, kc)
        d = jnp.dot(kidx_ref[0, pl.ds(start, kc), :].astype(BF16), qi_all, preferred_element_type=F32)
        r = jnp.maximum(d, 0.0) * wi_all
        acc = r[:, :tq]
        for h in range(1, H_IDX):
            acc = acc + r[:, h * tq:(h + 1) * tq]
        sc_ref[ch] = jnp.where(start + k_iota <= q_pos, acc, -jnp.inf)
        return 0

    lax.fori_loop(0, nc, score_chunk, 0)
    n_valid = (i * tq + 1 + lax.broadcasted_iota(jnp.int32, (1, tq), 1)).astype(F32)
    _select_topk(sc_ref, nc, n_sel, n_valid, 0)

    m_ref[...] = jnp.full(m_ref.shape, MASK_VALUE, F32)
    acc_ref[...] = jnp.zeros(acc_ref.shape, F32)

    sub = kc // 2
    q_groups = lambda g: jnp.concatenate([qt_ref[0, g * rep + r] for r in range(rep)], axis=1)

    def logits(ch, half, s_ref):
        start = pl.multiple_of(ch * kc + half * sub, sub)
        for g in range(nkv):
            kblk = k_ref[0, pl.ds(start, sub), (g // 2) * LANES:(g // 2 + 1) * LANES].astype(BF16)
            s_ref[g] = jnp.dot(kblk, q_groups(g), preferred_element_type=F32)

    def softmax_pv(ch, half, s_ref):
        start = pl.multiple_of(ch * kc + half * sub, sub)
        bias = jnp.tile(sc_ref[ch, half * sub:(half + 1) * sub, :], (1, rep))
        for g in range(nkv):
            _flash_step_t(s_ref[g] + bias, vt_ref[0, g, :, pl.ds(start, sub)], m_ref, acc_ref, g)

    logits(0, 0, sa_ref)

    def attend(ch, _):
        logits(ch, 1, sb_ref)
        softmax_pv(ch, 0, sa_ref)
        logits(jnp.minimum(ch + 1, nc - 1), 0, sa_ref)
        softmax_pv(ch, 1, sb_ref)
        return 0

    lax.fori_loop(0, nc, attend, 0)
    for h in range(0, nh, 2):
        pair = []
        for hh in (h, h + 1):
            g, r = hh // rep, hh % rep
            cols = slice(r * tq, (r + 1) * tq)
            pair.append(acc_ref[g, :HEAD_DIM, cols] / acc_ref[g, HEAD_DIM:HEAD_DIM + 1, cols])
        o_ref[0, :, h * HEAD_DIM:(h + 2) * HEAD_DIM] = jnp.concatenate(pair, axis=0).T


def _dsa_prompt(qit, wit, kidx, qt, k, vt, tq, kc, n_sel):
    n, nh, _, t = qt.shape
    nkv = vt.shape[1]
    rep = nh // nkv
    vrows = vt.shape[2]
    arows = vrows + DENOM_ROWS
    return pl.pallas_call(
        functools.partial(_dsa_prompt_kernel, tq=tq, kc=kc, n_sel=n_sel, rep=rep),
        grid=(n, t // tq),
        in_specs=[pl.BlockSpec((1, H_IDX, 2 * D_IDX, tq), lambda b, i: (b, 0, 0, i)),
                  pl.BlockSpec((1, H_IDX, tq), lambda b, i: (b, 0, i)),
                  pl.BlockSpec((1, t, 2 * D_IDX), lambda b, i: (b, 0, 0)),
                  pl.BlockSpec((1, nh, 2 * HEAD_DIM, tq), lambda b, i: (b, 0, 0, i)),
                  pl.BlockSpec((1, t, nkv * HEAD_DIM), lambda b, i: (b, 0, 0)),
                  pl.BlockSpec((1, nkv, vrows, t), lambda b, i: (b, 0, 0, 0))],
        out_specs=pl.BlockSpec((1, tq, nh * HEAD_DIM), lambda b, i: (b, i, 0)),
        out_shape=jax.ShapeDtypeStruct((n, t, nh * HEAD_DIM), F32),
        scratch_shapes=[pltpu.VMEM((t // kc, kc, tq), F32),
                        pltpu.VMEM((nkv, 1, rep * tq), F32), pltpu.VMEM((nkv, arows, rep * tq), F32),
                        pltpu.VMEM((nkv, kc // 2, rep * tq), F32), pltpu.VMEM((nkv, kc // 2, rep * tq), F32)],
        compiler_params=_cparams(("parallel", "arbitrary")),
        name="dsa_prompt",
    )(qit, wit, kidx, qt, k, vt)


def _dsa_sample_score_kernel(pt_ref, qi_ref, wi_ref, *refs, pp, n_steps, page, tq, n_sel):
    del pt_ref
    kp = refs[:pp]
    kn_ref, bp_ref, bn_ref, sc_ref = refs[pp:]
    j = pl.program_id(1)
    n_pages = n_steps * pp
    q_off = lax.broadcasted_iota(jnp.int32, (tq, page), 0)
    k_off = lax.broadcasted_iota(jnp.int32, (tq, page), 1)

    def score(kpage_t, pg, new):
        d = jnp.dot(qi_ref[0], kpage_t.astype(BF16), preferred_element_type=F32)
        r = (jnp.maximum(d, 0.0) * wi_ref[0]).reshape(H_IDX, tq, page)
        acc = r[0]
        for h in range(1, H_IDX):
            acc = acc + r[h]
        sc_ref[pg] = jnp.where(k_off <= q_off, acc, -jnp.inf) if new else acc

    @pl.when(j < n_steps)
    def _():
        for p in range(pp):
            score(kp[p][0, 0], j * pp + p, False)

    @pl.when(j == n_steps)
    def _():
        score(kn_ref[0], n_pages, True)
        n_valid = (n_pages * page + 1 + lax.broadcasted_iota(jnp.int32, (tq, 1), 0)).astype(F32)
        _select_topk(sc_ref, n_pages + 1, n_sel, n_valid, 1)
        bp_ref[0] = sc_ref[0:n_pages]
        bn_ref[0] = sc_ref[n_pages]


def _dsa_sample_scores(page_table, qi, wi, kidx_cache_t, kidx_new_t, tq, n_sel, pp):
    nb, n_pages = page_table.shape
    page = kidx_cache_t.shape[3]
    n_steps = n_pages // pp
    rows = qi.shape[1]
    seq3 = lambda b, j, pt: (b, 0, 0)
    grid_spec = pltpu.PrefetchScalarGridSpec(
        num_scalar_prefetch=1,
        grid=(nb, n_steps + 1),
        in_specs=[pl.BlockSpec((1, rows, D_IDX), seq3), pl.BlockSpec((1, rows, 1), seq3)]
                 + _paged_specs(pp, n_steps, (1, 1, D_IDX, page), lambda pg: (0, pg, 0, 0))
                 + [pl.BlockSpec((1, D_IDX, page), seq3)],
        out_specs=[pl.BlockSpec((1, n_pages, tq, page), lambda b, j, pt: (b, 0, 0, 0)),
                   pl.BlockSpec((1, tq, page), seq3)],
        scratch_shapes=[pltpu.VMEM((n_pages + 1, tq, page), F32)],
    )
    return pl.pallas_call(
        functools.partial(_dsa_sample_score_kernel, pp=pp, n_steps=n_steps, page=page, tq=tq, n_sel=n_sel),
        grid_spec=grid_spec,
        out_shape=[jax.ShapeDtypeStruct((nb, n_pages, tq, page), F32), jax.ShapeDtypeStruct((nb, tq, page), F32)],
        compiler_params=_cparams(("parallel", "arbitrary")),
        name="dsa_sample_scores",
    )(page_table, qi, wi, *([kidx_cache_t] * pp), kidx_new_t)


def _dsa_sample_attn_kernel(pt_ref, wq_ref, *refs, pp, n_steps, tq, nh, rep):
    del pt_ref
    kp, vp = refs[:pp], refs[pp:2 * pp]
    kn_ref, vn_ref, bp_ref, bn_ref, o_ref, m_ref, l_ref, acc_ref = refs[2 * pp:]
    j = pl.program_id(1)

    @pl.when(j == 0)
    def _():
        _init_softmax(m_ref, l_ref, acc_ref)

    ckv = acc_ref.shape[1]

    def logits(kt, bias):
        s = jnp.dot(wq_ref[0], kt.reshape(ckv, -1).astype(BF16), preferred_element_type=F32)
        return s + jnp.tile(bias, (nh, 1))

    @pl.when(j < n_steps)
    def _():
        ss = [logits(kp[p][0, 0], bp_ref[0, p]) for p in range(pp)]
        vts = [vp[p][0, 0].reshape(ckv, -1).astype(BF16) for p in range(pp)]
        _softmax_update(ss, vts, m_ref, l_ref, acc_ref)

    @pl.when(j == n_steps)
    def _():
        _softmax_update([logits(kn_ref[0], bn_ref[0])], [vn_ref[0].astype(BF16)], m_ref, l_ref, acc_ref)
        out = acc_ref[...] / l_ref[...]
        for h in range(nh):
            g = h // rep
            o_ref[0, :, h * HEAD_DIM:(h + 1) * HEAD_DIM] = out[h * tq:(h + 1) * tq, g * HEAD_DIM:(g + 1) * HEAD_DIM]


def _dsa_sample_attn(page_table, wq, kcache_t, vcache_t, knew_t, vnew_t, bias_pages, bias_new, tq, nh, pp):
    nb, n_pages = page_table.shape
    _, _, nkv, hd, page = kcache_t.shape
    ckv = nkv * hd
    n_steps = n_pages // pp
    rows = wq.shape[1]
    seq3 = lambda b, j, pt: (b, 0, 0)
    kv_specs = lambda: _paged_specs(pp, n_steps, (1, 1, nkv, hd, page), lambda pg: (0, pg, 0, 0, 0))
    grid_spec = pltpu.PrefetchScalarGridSpec(
        num_scalar_prefetch=1,
        grid=(nb, n_steps + 1),
        in_specs=[pl.BlockSpec((1, rows, ckv), seq3)] + kv_specs() + kv_specs()
                 + [pl.BlockSpec((1, ckv, page), seq3), pl.BlockSpec((1, ckv, page), seq3),
                    pl.BlockSpec((1, pp, tq, page), lambda b, j, pt: (b, jnp.minimum(j, n_steps - 1), 0, 0)),
                    pl.BlockSpec((1, tq, page), seq3)],
        out_specs=pl.BlockSpec((1, tq, nh * HEAD_DIM), seq3),
        scratch_shapes=[pltpu.VMEM((rows, 1), F32), pltpu.VMEM((rows, 1), F32), pltpu.VMEM((rows, ckv), F32)],
    )
    return pl.pallas_call(
        functools.partial(_dsa_sample_attn_kernel, pp=pp, n_steps=n_steps, tq=tq, nh=nh, rep=nh * HEAD_DIM // ckv),
        grid_spec=grid_spec,
        out_shape=jax.ShapeDtypeStruct((nb, tq, nh * HEAD_DIM), F32),
        compiler_params=_cparams(("parallel", "arbitrary")),
        name="dsa_sample_attn",
    )(page_table, wq, *([kcache_t] * pp), *([vcache_t] * pp), knew_t, vnew_t, bias_pages, bias_new)


def _pad_cols(a, n):
    return jnp.pad(a, [(0, 0)] * (a.ndim - 1) + [(0, n - a.shape[-1])])


def _rope_tables(pos):
    half = HEAD_DIM // 2
    inv = ROPE_THETA ** (-jnp.arange(half, dtype=F32) / half)
    ang = pos.astype(F32)[:, None] * inv[None, :]
    cos, sin = jnp.cos(ang), jnp.sin(ang)
    return jnp.tile(cos, (1, LANES // half)), jnp.tile(jnp.concatenate([-sin, sin], axis=1), (1, LANES // HEAD_DIM))


def _block_diag_queries(q, n_heads, n_kv):
    b, t, _ = q.shape
    qh = q.reshape(b, t, n_heads, HEAD_DIM).transpose(0, 2, 1, 3)
    onehot = jax.nn.one_hot(jnp.arange(n_heads) // (n_heads // n_kv), n_kv, dtype=q.dtype)
    w = qh[:, :, :, None, :] * onehot[None, :, None, :, None]
    return w.reshape(b, n_heads * t, n_kv * HEAD_DIM).astype(BF16)


def _tile(m, pref):
    for t in pref:
        if m % t == 0:
            return t
    return m


def kernel(x_prompt, x_sample, state_conv, cache_fox_k, cache_fox_v, cache_fox_logf, cache_dsa_k, cache_dsa_v,
           cache_dsa_kidx, page_table, norm_mix_even, w_in_even, b_forget, w_dw, b_dw, ln_conv_g, ln_conv_b,
           w_out_even, norm_mix_odd, w_in_odd, w_out_odd, norm_ffn, w_group, b_group, w_router, b_router, w_gate,
           w_up, w_down, norm_final):
    n_p, t_p, d = x_prompt.shape
    n_s, t_s, _ = x_sample.shape
    depth = norm_ffn.shape[0]
    page = cache_fox_k.shape[2]
    n_pages = page_table.shape[1]
    past = n_pages * page
    c_a = w_dw.shape[2]
    h_b = b_forget.shape[1]
    c_b = h_b * HEAD_DIM
    kv_c = cache_dsa_k.shape[3]
    h_c = w_out_odd.shape[1] // HEAD_DIM
    nq, nkv, nqi = h_c * HEAD_DIM, kv_c * HEAD_DIM, H_IDX * D_IDX
    assert t_s == SUBLANES and depth == 2 and w_in_even.shape[0] == 1 and w_in_odd.shape[0] == 1

    mp, ms = n_p * t_p, n_s * t_s
    hp = x_prompt.reshape(mp, d)
    hs = x_sample.reshape(ms, d)
    tm_p = _tile(mp, (512, 256, 128))
    row = lambda a: a.reshape(1, -1)

    def moe_layer(h, layer, tm, final):
        wr = _pad_cols(jnp.concatenate([w_group[layer], w_router[layer]], axis=1), LANES)
        br = _pad_cols(jnp.concatenate([b_group[layer], b_router[layer]]).reshape(1, -1), LANES)
        sort_tokens = tm >= 2 * MOE_WINDOW and (tm - MOE_WINDOW) % 16 == 0
        return (_moe_sorted if sort_tokens else _moe)(
            h, row(norm_ffn[layer]), wr, br, w_gate[layer].astype(BF16), w_up[layer].astype(BF16),
            w_down[layer].astype(BF16), row(norm_final), tm, final)

    tm_seq = _tile(t_p, (512, 256, 128))
    w_e = w_in_even[0]
    w_e = jnp.concatenate([w_e[:, :2 * c_a + 3 * c_b], _pad_cols(w_e[:, 2 * c_a + 3 * c_b:], LANES)], axis=1).astype(BF16)
    bf = _pad_cols(b_forget[0].reshape(1, -1), LANES)
    g_e = row(norm_mix_even[0])
    ag_p, k_p, qt_p, kt_p, vt_p, lft_p = _proj_even(hp, g_e, w_e, bf, c_a, c_b, tm_seq, t_p)
    ag_s, q_s, k_s, v_s, lf_s = _proj_even(hs, g_e, w_e, bf, c_a, c_b, ms)
    logf_s = lf_s[:, :h_b].reshape(n_s, t_s, h_b)

    conv_args = (w_dw[0], row(b_dw[0]), row(ln_conv_g[0]), row(ln_conv_b[0]))
    a_p, cst_p = _conv_module(ag_p.reshape(n_p, t_p, 2 * c_a), jnp.zeros((n_p, HIST, c_a), F32), *conv_args, tm_seq)
    a_s, cst_s = _conv_module(ag_s.reshape(n_s, t_s, 2 * c_a), state_conv[0], *conv_args, t_s)

    b_p = _fox_prompt(qt_p, k_p.reshape(n_p, t_p, c_b), vt_p, _seq_cumsum_columns(lft_p),
                      _tile(t_p, (512, 256, 128)), _tile(t_p, (256, 128)))

    n_pool = cache_fox_k.shape[1]
    pp = _tile(n_pages, (PAGES_PER_STEP, 4, 2, 1))
    keys_minor = lambda a: jnp.moveaxis(a, 2, -1)
    new_t = lambda a, f: _pad_cols(a.reshape(n_s, t_s, f).transpose(0, 2, 1), page)
    cpool = _page_cumsum(keys_minor(cache_fox_logf)[0], _tile(n_pool, (64, 32, 16, 8, 4, 2)))
    cnew = _page_cumsum(new_t(logf_s, h_b), _tile(n_s, (32, 16, 8, 4, 2)))
    b_s = _fox_sample(page_table, _block_diag_queries(q_s.reshape(n_s, t_s, c_b), h_b, h_b),
                      keys_minor(cache_fox_k), keys_minor(cache_fox_v), cpool,
                      new_t(k_s, c_b), new_t(v_s, c_b), cnew, t_s, pp)

    w_oe = w_out_even[0].astype(BF16)
    hp = _outproj([a_p.reshape(mp, c_a), b_p.reshape(mp, c_b)], w_oe, hp, tm_p)
    hs = _outproj([a_s.reshape(ms, c_a), b_s.reshape(ms, c_b)], w_oe, hs, ms)
    hp = moe_layer(hp, 0, _tile(mp, (1024, 512, 256, 128)), False)
    hs = moe_layer(hs, 0, ms, False)

    w_o = w_in_odd[0]
    s4 = nq + 2 * nkv + nqi
    w_o = jnp.concatenate([w_o[:, :s4], _pad_cols(w_o[:, s4:s4 + D_IDX], LANES), _pad_cols(w_o[:, s4 + D_IDX:], LANES)],
                          axis=1).astype(BF16)
    g_o = row(norm_mix_odd[0])
    wi_scale = (H_IDX ** -0.5) * (D_IDX ** -0.5)
    cos_p, sin_p = _rope_tables(jnp.arange(t_p))
    cos_s, sin_s = _rope_tables(jnp.tile(past + jnp.arange(t_s), n_s))
    dqt_p, dk_p, dkt_p, dvt_p, dqit_p, dki_p, dkit_p, dwit_p = _proj_odd(hp, g_o, w_o, cos_p, sin_p, nq, nkv, nqi,
                                                                         wi_scale, tm_seq, t_p)
    dq_s, dk_s, dv_s, dqi_s, dki_s, dwi_s = _proj_odd(hs, g_o, w_o, cos_s, sin_s, nq, nkv, nqi, wi_scale, ms)
    dki_s = dki_s[:, :D_IDX]

    n_sel_p = min(TOPK_MAX, t_p // 4)
    o_p = _dsa_prompt(dqit_p, dwit_p, dki_p.reshape(n_p, t_p, LANES), dqt_p, dk_p.reshape(n_p, t_p, nkv), dvt_p,
                      _tile(t_p, (128,)), _tile(t_p, (256, 128)), n_sel_p)

    n_sel_s = min(TOPK_MAX, (past + t_s) // 4)
    heads = lambda a, n, nh: a.reshape(n, -1, nh, HEAD_DIM).transpose(0, 2, 1, 3).astype(BF16)
    qi_rows = heads(dqi_s, n_s, H_IDX).reshape(n_s, H_IDX * t_s, D_IDX)
    wi_rows = dwi_s[:, :H_IDX].reshape(n_s, t_s, H_IDX).transpose(0, 2, 1).reshape(n_s, H_IDX * t_s, 1)
    bias_pages, bias_new = _dsa_sample_scores(page_table, qi_rows, wi_rows, keys_minor(cache_dsa_kidx),
                                              new_t(dki_s, D_IDX), t_s, n_sel_s, pp)
    o_s = _dsa_sample_attn(page_table, _block_diag_queries(dq_s.reshape(n_s, t_s, nq), h_c, kv_c),
                           keys_minor(cache_dsa_k), keys_minor(cache_dsa_v), new_t(dk_s, nkv), new_t(dv_s, nkv),
                           bias_pages, bias_new, t_s, h_c, pp)

    w_oo = w_out_odd[0].astype(BF16)
    hp = _outproj([o_p.reshape(mp, nq)], w_oo, hp, tm_p)
    hs = _outproj([o_s.reshape(ms, nq)], w_oo, hs, ms)
    hp = moe_layer(hp, 1, _tile(mp, (1024, 512, 256, 128)), True)
    hs = moe_layer(hs, 1, ms, True)

    t_minor = lambda a: jnp.moveaxis(a, -1, 1)[None]
    return (hp.reshape(n_p, t_p, d), hs.reshape(n_s, t_s, d),
            cst_p[None], cst_s[None],
            t_minor(kt_p), k_s.reshape(1, n_s, t_s, h_b, HEAD_DIM),
            t_minor(vt_p), v_s.reshape(1, n_s, t_s, h_b, HEAD_DIM),
            t_minor(lft_p), logf_s[None],
            t_minor(dkt_p), dk_s.reshape(1, n_s, t_s, kv_c, HEAD_DIM),
            t_minor(dvt_p), dv_s.reshape(1, n_s, t_s, kv_c, HEAD_DIM),
            t_minor(dkit_p), dki_s.reshape(1, n_s, t_s, D_IDX))
```

```python
import functools

import jax
import jax.numpy as jnp
from jax import lax
from jax.experimental import pallas as pl
from jax.experimental.pallas import tpu as pltpu

F32 = jnp.float32
BF16 = jnp.bfloat16

HEAD_DIM = 64
CONV_W = 31
H_IDX = 8
D_IDX = 64
TOPK_MAX = 256
N_GROUPS = 4
E_PER_GROUP = 4
N_EXPERTS = N_GROUPS * E_PER_GROUP
ROPE_THETA = 10000.0
RMS_EPS = 1e-6
LN_EPS = 1e-5

LANES = 128
SUBLANES = 8
VMEM_LIMIT = 56 * 1024 * 1024

MASK_VALUE = -1e30
HIST = CONV_W - 1
HIST_PAD = 32
BISECT_ITERS = 20
POS_BITS = 14
PAGES_PER_STEP = 16
LOG2E = 1.4426950408889634

_NT = (((1,), (1,)), ((), ()))


def _cparams(sem):
    return pltpu.CompilerParams(dimension_semantics=sem, vmem_limit_bytes=VMEM_LIMIT)


def _rms(x, g):
    ms = jnp.mean(x * x, axis=-1, keepdims=True)
    return x * lax.rsqrt(ms + RMS_EPS) * g


def _rope128(xb, cos, sin_signed):
    lane = lax.broadcasted_iota(jnp.int32, xb.shape, 1)
    first_half = jnp.bitwise_and(lane, HEAD_DIM - 1) < (HEAD_DIM // 2)
    rot = jnp.where(first_half, pltpu.roll(xb, LANES - HEAD_DIM // 2, 1), pltpu.roll(xb, HEAD_DIM // 2, 1))
    return xb * cos + rot * sin_signed


def _split_heads_t(tile_t):
    return tile_t.reshape(2, HEAD_DIM, tile_t.shape[1])


def _zero_padded_pair(tile_t, first_slot):
    top, bot = tile_t[:HEAD_DIM], tile_t[HEAD_DIM:]
    zero = jnp.zeros_like(top)
    place = lambda x, slot: jnp.concatenate([x, zero] if slot == 0 else [zero, x], axis=0).astype(BF16)
    return place(top, first_slot[0]), place(bot, first_slot[1])


def _proj_even_kernel(x_ref, g_ref, w_ref, bf_ref, *out_refs, c_a, c_b, lane_major):
    h = _rms(x_ref[...], g_ref[...]).astype(BF16)

    def seg(a, b):
        return jnp.dot(h, w_ref[:, a:b], preferred_element_type=F32)

    o = 2 * c_a
    z = seg(o + 3 * c_b, o + 3 * c_b + LANES) + bf_ref[...]
    logf = jnp.minimum(z, 0.0) - jnp.log1p(jnp.exp(-jnp.abs(z)))
    scale = HEAD_DIM ** -0.5
    if not lane_major:
        ag_ref, q_ref, k_ref, v_ref, lf_ref = out_refs
        ag_ref[...] = seg(0, o)
        q_ref[...] = seg(o, o + c_b) * scale
        k_ref[...] = seg(o + c_b, o + 2 * c_b)
        v_ref[...] = seg(o + 2 * c_b, o + 3 * c_b)
        lf_ref[...] = logf
        return
    ag_ref, k_ref, qt_ref, kt_ref, vt_ref, lft_ref = out_refs
    ag_ref[...] = seg(0, o)
    for b in range(c_b // LANES):
        cols = lambda base: (base + b * LANES, base + (b + 1) * LANES)
        qa, qb = _zero_padded_pair((seg(*cols(o)) * (scale * LOG2E)).T, (0, 1))
        qt_ref[0, 2 * b] = qa
        qt_ref[0, 2 * b + 1] = qb
        k = seg(*cols(o + c_b))
        k_ref[:, b * LANES:(b + 1) * LANES] = k
        kt_ref[0, 2 * b:2 * b + 2] = _split_heads_t(k.T)
        vt_ref[0, 2 * b:2 * b + 2] = _split_heads_t(seg(*cols(o + 2 * c_b)).T)
    lft_ref[0] = logf.T[:lft_ref.shape[1]]


def _proj_even(x, g, w, bf, c_a, c_b, tm, seq_len=None):
    m, d = x.shape
    nh = c_b // HEAD_DIM
    row = lambda n: pl.BlockSpec((tm, n), lambda i: (i, 0))
    full = lambda a: pl.BlockSpec(a.shape, lambda i: (0, 0))
    if seq_len is None:
        out_specs = [row(2 * c_a), row(c_b), row(c_b), row(c_b), row(LANES)]
        out_shape = [jax.ShapeDtypeStruct((m, n), F32) for n in (2 * c_a, c_b, c_b, c_b, LANES)]
    else:
        nper, n = seq_len // tm, m // seq_len
        lm = lambda rows: pl.BlockSpec((1, nh, rows, tm), lambda i: (i // nper, 0, 0, i % nper))
        out_specs = [row(2 * c_a), row(c_b), lm(2 * HEAD_DIM), lm(HEAD_DIM), lm(HEAD_DIM),
                     pl.BlockSpec((1, nh, tm), lambda i: (i // nper, 0, i % nper))]
        out_shape = [jax.ShapeDtypeStruct((m, 2 * c_a), F32), jax.ShapeDtypeStruct((m, c_b), F32),
                     jax.ShapeDtypeStruct((n, nh, 2 * HEAD_DIM, seq_len), BF16),
                     jax.ShapeDtypeStruct((n, nh, HEAD_DIM, seq_len), F32),
                     jax.ShapeDtypeStruct((n, nh, HEAD_DIM, seq_len), F32),
                     jax.ShapeDtypeStruct((n, nh, seq_len), F32)]
    return pl.pallas_call(
        functools.partial(_proj_even_kernel, c_a=c_a, c_b=c_b, lane_major=seq_len is not None),
        grid=(m // tm,),
        in_specs=[row(d), full(g), full(w), full(bf)],
        out_specs=out_specs, out_shape=out_shape,
        compiler_params=_cparams(("parallel",)),
        name="proj_even",
    )(x, g, w, bf)


def _proj_odd_kernel(x_ref, g_ref, w_ref, cos_ref, sin_ref, *out_refs, nq, nkv, nqi, wi_scale, lane_major):
    h = _rms(x_ref[...], g_ref[...]).astype(BF16)
    cos = cos_ref[...]
    sin = sin_ref[...]
    scale = HEAD_DIM ** -0.5

    def seg(a, b):
        return jnp.dot(h, w_ref[:, a:b], preferred_element_type=F32)

    def roped(a, c):
        return _rope128(seg(a + c * LANES, a + (c + 1) * LANES), cos, sin)

    o_k, o_v, o_qi = nq, nq + nkv, nq + 2 * nkv
    o_ki = o_qi + nqi
    wi = seg(o_ki + LANES, o_ki + 2 * LANES) * wi_scale
    if not lane_major:
        q_ref, k_ref, v_ref, qi_ref, ki_ref, wi_ref = out_refs
        for c in range(nq // LANES):
            q_ref[:, c * LANES:(c + 1) * LANES] = roped(0, c) * scale
        for c in range(nkv // LANES):
            k_ref[:, c * LANES:(c + 1) * LANES] = roped(o_k, c)
        v_ref[...] = seg(o_v, o_v + nkv)
        for c in range(nqi // LANES):
            qi_ref[:, c * LANES:(c + 1) * LANES] = roped(o_qi, c)
        ki_ref[...] = roped(o_ki, 0)
        wi_ref[...] = wi
        return
    qt_ref, k_ref, kt_ref, vt_ref, qit_ref, ki_ref, kit_ref, wit_ref = out_refs
    rep = nq // nkv
    for c in range(nq // LANES):
        slot = ((2 * c) // rep) % 2
        qa, qb = _zero_padded_pair((roped(0, c) * (scale * LOG2E)).T, (slot, slot))
        qt_ref[0, 2 * c] = qa
        qt_ref[0, 2 * c + 1] = qb
    for c in range(nkv // LANES):
        k = roped(o_k, c)
        k_ref[:, c * LANES:(c + 1) * LANES] = k
        kt_ref[0, 2 * c:2 * c + 2] = _split_heads_t(k.T)
        vt_ref[0, 2 * c:2 * c + 2] = _split_heads_t(seg(o_v + c * LANES, o_v + (c + 1) * LANES).T)
    for c in range(nqi // LANES):
        qa, qb = _zero_padded_pair(roped(o_qi, c).T, (0, 0))
        qit_ref[0, 2 * c] = qa
        qit_ref[0, 2 * c + 1] = qb
    ki = roped(o_ki, 0)
    ki_ref[...] = ki
    kit_ref[0] = ki.T[:D_IDX]
    wit_ref[0] = wi.T[:wit_ref.shape[1]]


def _proj_odd(x, g, w, cos, sin, nq, nkv, nqi, wi_scale, tm, seq_len=None):
    m, d = x.shape
    nper = cos.shape[0] // tm
    row = lambda n: pl.BlockSpec((tm, n), lambda i: (i, 0))
    full = lambda a: pl.BlockSpec(a.shape, lambda i: (0, 0))
    tab = pl.BlockSpec((tm, LANES), lambda i: (i % nper, 0))
    if seq_len is None:
        widths = (nq, nkv, nkv, nqi, LANES, LANES)
        out_specs = [row(n) for n in widths]
        out_shape = [jax.ShapeDtypeStruct((m, n), F32) for n in widths]
    else:
        n = m // seq_len
        lm = lambda nh, rows: pl.BlockSpec((1, nh, rows, tm), lambda i: (i // nper, 0, 0, i % nper))
        lm3 = lambda rows: pl.BlockSpec((1, rows, tm), lambda i: (i // nper, 0, i % nper))
        nhq, nhk, nhi = nq // HEAD_DIM, nkv // HEAD_DIM, nqi // D_IDX
        out_specs = [lm(nhq, 2 * HEAD_DIM), row(nkv), lm(nhk, HEAD_DIM), lm(nhk, HEAD_DIM), lm(nhi, 2 * D_IDX),
                     row(LANES), lm3(D_IDX), lm3(H_IDX)]
        out_shape = [jax.ShapeDtypeStruct((n, nhq, 2 * HEAD_DIM, seq_len), BF16), jax.ShapeDtypeStruct((m, nkv), F32),
                     jax.ShapeDtypeStruct((n, nhk, HEAD_DIM, seq_len), F32),
                     jax.ShapeDtypeStruct((n, nhk, HEAD_DIM, seq_len), F32),
                     jax.ShapeDtypeStruct((n, nhi, 2 * D_IDX, seq_len), BF16), jax.ShapeDtypeStruct((m, LANES), F32),
                     jax.ShapeDtypeStruct((n, D_IDX, seq_len), F32), jax.ShapeDtypeStruct((n, H_IDX, seq_len), F32)]
    return pl.pallas_call(
        functools.partial(_proj_odd_kernel, nq=nq, nkv=nkv, nqi=nqi, wi_scale=wi_scale, lane_major=seq_len is not None),
        grid=(m // tm,),
        in_specs=[row(d), full(g), full(w), tab, tab],
        out_specs=out_specs, out_shape=out_shape,
        compiler_params=_cparams(("parallel",)),
        name="proj_odd",
    )(x, g, w, cos, sin)


def _conv_kernel(ag_ref, st_ref, wdw_ref, bdw_ref, lng_ref, lnb_ref, out_ref, nst_ref, buf_ref, *, tt, c_a, rc):
    t = pl.program_id(1)
    lo = HIST_PAD - HIST

    @pl.when(t == 0)
    def _():
        buf_ref[lo:HIST_PAD, :] = st_ref[0]

    @pl.when(t > 0)
    def _():
        buf_ref[lo:HIST_PAD, :] = buf_ref[tt + lo:tt + HIST_PAD, :]

    ag = ag_ref[0]
    buf_ref[HIST_PAD:HIST_PAD + tt, :] = ag[:, :c_a] * jax.nn.sigmoid(ag[:, c_a:])

    bdw = bdw_ref[...]
    lng = lng_ref[...]
    lnb = lnb_ref[...]
    for r in range(tt // rc):
        acc = jnp.zeros((rc, c_a), F32) + bdw
        for j in range(CONV_W):
            s = r * rc + lo + j
            acc = acc + wdw_ref[j:j + 1, :] * buf_ref[s:s + rc, :]
        mu = jnp.mean(acc, axis=-1, keepdims=True)
        cen = acc - mu
        var = jnp.mean(cen * cen, axis=-1, keepdims=True)
        cn = cen * lax.rsqrt(var + LN_EPS) * lng + lnb
        out_ref[0, r * rc:(r + 1) * rc, :] = cn * jax.nn.sigmoid(cn)

    @pl.when(t == pl.num_programs(1) - 1)
    def _():
        nst_ref[0] = buf_ref[tt + lo:tt + HIST_PAD, :]


def _conv_module(ag, state, wdw, bdw, lng, lnb, tt):
    n, t, two_ca = ag.shape
    c_a = two_ca // 2
    rc = min(tt, 64)
    vec = lambda a: pl.BlockSpec(a.shape, lambda i, j: (0, 0))
    return pl.pallas_call(
        functools.partial(_conv_kernel, tt=tt, c_a=c_a, rc=rc),
        grid=(n, t // tt),
        in_specs=[pl.BlockSpec((1, tt, two_ca), lambda i, j: (i, j, 0)),
                  pl.BlockSpec((1, HIST, c_a), lambda i, j: (i, 0, 0)),
                  vec(wdw), vec(bdw), vec(lng), vec(lnb)],
        out_specs=[pl.BlockSpec((1, tt, c_a), lambda i, j: (i, j, 0)),
                   pl.BlockSpec((1, HIST, c_a), lambda i, j: (i, 0, 0))],
        out_shape=[jax.ShapeDtypeStruct((n, t, c_a), F32), jax.ShapeDtypeStruct((n, HIST, c_a), F32)],
        scratch_shapes=[pltpu.VMEM((HIST_PAD + tt, c_a), F32)],
        compiler_params=_cparams(("parallel", "arbitrary")),
        name="conv_module",
    )(ag, state, wdw, bdw, lng, lnb)


def _lane_cumsum(x):
    lane = lax.broadcasted_iota(jnp.int32, x.shape, 1)
    s = 1
    while s < LANES:
        x = x + jnp.where(lane >= s, pltpu.roll(x, s, 1), 0.0)
        s *= 2
    return x


def _page_cumsum_kernel(x_ref, o_ref):
    nb, h, _ = x_ref.shape
    o_ref[...] = _lane_cumsum(x_ref[...].reshape(nb * h, LANES)).reshape(nb, h, LANES)


def _page_cumsum(x, nb):
    b, h, _ = x.shape
    spec = pl.BlockSpec((nb, h, LANES), lambda i: (i, 0, 0))
    return pl.pallas_call(
        _page_cumsum_kernel,
        grid=(b // nb,),
        in_specs=[spec], out_specs=spec,
        out_shape=jax.ShapeDtypeStruct(x.shape, F32),
        compiler_params=_cparams(("parallel",)),
        name="page_cumsum",
    )(x)


def _seq_cumsum_kernel(x_ref, o_ref):
    t = x_ref.shape[3]
    nb = t // LANES
    x = x_ref[0, 0]
    loc = _lane_cumsum(jnp.concatenate([x[:, b * LANES:(b + 1) * LANES] for b in range(nb)], axis=0))
    off = jnp.zeros((1, LANES), F32)
    for b in range(nb):
        cur = loc[b:b + 1] + off
        o_ref[0, 0, b * LANES:(b + 1) * LANES, :] = jnp.broadcast_to(cur * LOG2E, (LANES, LANES)).T
        off = jnp.broadcast_to(cur[:, LANES - 1:LANES], (1, LANES))


def _seq_cumsum_columns(x):
    n, h, t = x.shape
    return pl.pallas_call(
        _seq_cumsum_kernel,
        grid=(n, h),
        in_specs=[pl.BlockSpec((1, 1, 1, t), lambda i, j: (i, j, 0, 0))],
        out_specs=pl.BlockSpec((1, 1, t, LANES), lambda i, j: (i, j, 0, 0)),
        out_shape=jax.ShapeDtypeStruct((n, h, t, LANES), F32),
        compiler_params=_cparams(("parallel", "parallel")),
        name="seq_cumsum",
    )(x.reshape(n, h, 1, t))


DENOM_ROWS = 16


def _flash_step_t(s_t, v_t, m_ref, acc_ref, idx):
    m_old = m_ref[idx]
    m_new = jnp.maximum(m_old, jnp.max(s_t, axis=0, keepdims=True))
    alpha = jnp.exp2(m_old - m_new)
    p = jnp.exp2(s_t - m_new).astype(BF16)
    kc = v_t.shape[1]
    ones_row = jnp.where(lax.broadcasted_iota(jnp.int32, (DENOM_ROWS, kc), 0) == 0, 1.0, 0.0).astype(BF16)
    v_aug = jnp.concatenate([v_t.astype(BF16), ones_row], axis=0)
    acc_ref[idx] = acc_ref[idx] * alpha + jnp.dot(v_aug, p, preferred_element_type=F32)
    m_ref[idx] = m_new


def _fox_prompt_kernel(qt_ref, k_ref, vt_ref, c_ref, o_ref, m_ref, acc_ref, sa_ref, sb_ref, *, tq, kc):
    i = pl.program_id(2)
    m_ref[...] = jnp.full(m_ref.shape, MASK_VALUE, F32)
    acc_ref[...] = jnp.zeros(acc_ref.shape, F32)
    q_pos = i * tq + lax.broadcasted_iota(jnp.int32, (kc // 2, tq), 1)
    k_iota = lax.broadcasted_iota(jnp.int32, (kc // 2, tq), 0)

    sub = kc // 2

    def logits(ch, half, s_ref):
        start = pl.multiple_of(ch * kc + half * sub, sub)
        k = k_ref[0, pl.ds(start, sub), :].astype(BF16)
        for hh in range(2):
            s_ref[hh] = jnp.dot(k, qt_ref[0, hh], preferred_element_type=F32)

    def softmax_pv(ch, half, s_ref, masked):
        start = pl.multiple_of(ch * kc + half * sub, sub)
        for hh in range(2):
            s = s_ref[hh] - jnp.tile(c_ref[0, hh, pl.ds(start, sub), :], (1, tq // LANES))
            if masked:
                s = jnp.where(start + k_iota <= q_pos, s, MASK_VALUE)
            _flash_step_t(s, vt_ref[0, hh, :, pl.ds(start, sub)], m_ref, acc_ref, hh)

    def run(lo, hi, masked):
        def body(ch, _):
            logits(ch, 1, sb_ref)
            softmax_pv(ch, 0, sa_ref, masked)
            logits(jnp.minimum(ch + 1, hi - 1), 0, sa_ref)
            softmax_pv(ch, 1, sb_ref, masked)
            return 0

        @pl.when(lo < hi)
        def _():
            logits(lo, 0, sa_ref)
            lax.fori_loop(lo, hi, body, 0)

    n_full = (i * tq) // kc
    run(0, n_full, False)
    run(n_full, n_full + tq // kc, True)
    out_t = jnp.concatenate([acc_ref[hh, :HEAD_DIM, :] / acc_ref[hh, HEAD_DIM:HEAD_DIM + 1, :] for hh in range(2)],
                            axis=0)
    for b in range(tq // LANES):
        o_ref[0, b * LANES:(b + 1) * LANES, :] = out_t[:, b * LANES:(b + 1) * LANES].T


def _fox_prompt(qt, k, vt, c_b, tq, kc):
    n, t, cb = k.shape
    hp = cb // LANES
    vrows = vt.shape[2]
    arows = vrows + DENOM_ROWS
    return pl.pallas_call(
        functools.partial(_fox_prompt_kernel, tq=tq, kc=kc),
        grid=(n, hp, t // tq),
        in_specs=[pl.BlockSpec((1, 2, LANES, tq), lambda b, h, i: (b, h, 0, i)),
                  pl.BlockSpec((1, t, LANES), lambda b, h, i: (b, 0, h)),
                  pl.BlockSpec((1, 2, vrows, t), lambda b, h, i: (b, h, 0, 0)),
                  pl.BlockSpec((1, 2, t, LANES), lambda b, h, i: (b, h, 0, 0))],
        out_specs=pl.BlockSpec((1, tq, LANES), lambda b, h, i: (b, i, h)),
        out_shape=jax.ShapeDtypeStruct((n, t, cb), F32),
        scratch_shapes=[pltpu.VMEM((2, 1, tq), F32), pltpu.VMEM((2, arows, tq), F32),
                        pltpu.VMEM((2, kc // 2, tq), F32), pltpu.VMEM((2, kc // 2, tq), F32)],
        compiler_params=_cparams(("parallel", "parallel", "arbitrary")),
        name="fox_prompt",
    )(qt, k, vt, c_b)


def _init_softmax(m_ref, l_ref, acc_ref):
    m_ref[...] = jnp.full(m_ref.shape, MASK_VALUE, F32)
    l_ref[...] = jnp.zeros(l_ref.shape, F32)
    acc_ref[...] = jnp.zeros(acc_ref.shape, F32)


def _softmax_update(ss, vts, m_ref, l_ref, acc_ref):
    m_old = m_ref[...]
    smax = ss[0]
    for s in ss[1:]:
        smax = jnp.maximum(smax, s)
    m_new = jnp.maximum(m_old, jnp.max(smax, axis=-1, keepdims=True))
    alpha = jnp.exp(m_old - m_new)
    acc = alpha * acc_ref[...]
    psum = None
    for s, vt in zip(ss, vts):
        p = jnp.exp(s - m_new)
        psum = p if psum is None else psum + p
        acc = acc + lax.dot_general(p.astype(BF16), vt, _NT, preferred_element_type=F32)
    l_new = alpha * l_ref[...] + jnp.sum(psum, axis=-1, keepdims=True)
    m_ref[...] = m_new
    l_ref[...] = l_new
    acc_ref[...] = acc


def _paged_specs(pp, n_steps, block, index_tail):
    def spec(p):
        return pl.BlockSpec(block, lambda b, j, pt: index_tail(pt[b, jnp.minimum(j, n_steps - 1) * pp + p]))
    return [spec(p) for p in range(pp)]


def _fox_sample_kernel(pt_ref, wq_ref, *refs, pp, n_steps, tq, nh):
    del pt_ref
    kp, vp, cp = refs[:pp], refs[pp:2 * pp], refs[2 * pp:3 * pp]
    kn_ref, vn_ref, cn_ref, o_ref, m_ref, l_ref, acc_ref, off_ref = refs[3 * pp:]
    j = pl.program_id(1)
    rows, feat = acc_ref.shape
    page = off_ref.shape[1]

    @pl.when(j == 0)
    def _():
        _init_softmax(m_ref, l_ref, acc_ref)
        off_ref[...] = jnp.zeros(off_ref.shape, F32)

    def logits(kt, c):
        s = jnp.dot(wq_ref[0], kt.reshape(feat, page).astype(BF16), preferred_element_type=F32)
        return s - jnp.broadcast_to(c[:, None, :], (nh, tq, page)).reshape(rows, page)

    @pl.when(j < n_steps)
    def _():
        off = off_ref[...]
        ss, vs = [], []
        for p in range(pp):
            cloc = cp[p][0]
            ss.append(logits(kp[p][0, 0], cloc + off))
            vs.append(vp[p][0, 0].reshape(feat, page).astype(BF16))
            off = off + jnp.broadcast_to(cloc[:, page - 1:page], off.shape)
        off_ref[...] = off
        _softmax_update(ss, vs, m_ref, l_ref, acc_ref)

    @pl.when(j == n_steps)
    def _():
        s = logits(kn_ref[0], cn_ref[0] + off_ref[...])
        qi = jnp.bitwise_and(lax.broadcasted_iota(jnp.int32, (rows, page), 0), tq - 1)
        ki = lax.broadcasted_iota(jnp.int32, (rows, page), 1)
        s = jnp.where(ki <= qi, s, MASK_VALUE)
        _softmax_update([s], [vn_ref[0].astype(BF16)], m_ref, l_ref, acc_ref)
        out = acc_ref[...] / l_ref[...]
        for h in range(nh):
            o_ref[0, :, h * HEAD_DIM:(h + 1) * HEAD_DIM] = out[h * tq:(h + 1) * tq, h * HEAD_DIM:(h + 1) * HEAD_DIM]


def _fox_sample(page_table, wq, kcache_t, vcache_t, cpool, knew_t, vnew_t, cnew, tq, pp):
    nb, n_pages = page_table.shape
    _, _, nh, hd, page = kcache_t.shape
    feat = nh * hd
    n_steps = n_pages // pp
    rows = wq.shape[1]
    seq3 = lambda b, j, pt: (b, 0, 0)
    kv_specs = lambda: _paged_specs(pp, n_steps, (1, 1, nh, hd, page), lambda pg: (0, pg, 0, 0, 0))
    grid_spec = pltpu.PrefetchScalarGridSpec(
        num_scalar_prefetch=1,
        grid=(nb, n_steps + 1),
        in_specs=[pl.BlockSpec((1, rows, feat), seq3)] + kv_specs() + kv_specs()
                 + _paged_specs(pp, n_steps, (1, nh, page), lambda pg: (pg, 0, 0))
                 + [pl.BlockSpec((1, feat, page), seq3), pl.BlockSpec((1, feat, page), seq3),
                    pl.BlockSpec((1, nh, page), seq3)],
        out_specs=pl.BlockSpec((1, tq, feat), seq3),
        scratch_shapes=[pltpu.VMEM((rows, 1), F32), pltpu.VMEM((rows, 1), F32), pltpu.VMEM((rows, feat), F32),
                        pltpu.VMEM((nh, page), F32)],
    )
    return pl.pallas_call(
        functools.partial(_fox_sample_kernel, pp=pp, n_steps=n_steps, tq=tq, nh=nh),
        grid_spec=grid_spec,
        out_shape=jax.ShapeDtypeStruct((nb, tq, feat), F32),
        compiler_params=_cparams(("parallel", "arbitrary")),
        name="fox_sample",
    )(page_table, wq, *([kcache_t] * pp), *([vcache_t] * pp), *([cpool] * pp), knew_t, vnew_t, cnew)


def _outproj_kernel(*refs, n_in):
    ins, w_ref, res_ref, o_ref = refs[:n_in], refs[n_in], refs[n_in + 1], refs[n_in + 2]
    acc = res_ref[...]
    off = 0
    for r in ins:
        width = r.shape[1]
        acc = acc + jnp.dot(r[...].astype(BF16), w_ref[off:off + width, :], preferred_element_type=F32)
        off += width
    o_ref[...] = acc


def _outproj(ins, w, res, tm):
    m, d = res.shape
    row = lambda n: pl.BlockSpec((tm, n), lambda i: (i, 0))
    return pl.pallas_call(
        functools.partial(_outproj_kernel, n_in=len(ins)),
        grid=(m // tm,),
        in_specs=[row(a.shape[1]) for a in ins] + [pl.BlockSpec(w.shape, lambda i: (0, 0)), row(d)],
        out_specs=row(d),
        out_shape=jax.ShapeDtypeStruct((m, d), F32),
        compiler_params=_cparams(("parallel",)),
        name="outproj",
    )(*ins, w, res)


def _route(z):
    lane = lax.broadcasted_iota(jnp.int32, z.shape, 1).astype(F32)
    big = jnp.float32(1 << 20)
    neg = -jnp.inf
    rmax = lambda a: jnp.max(a, axis=-1, keepdims=True)
    rmin = lambda a: jnp.min(a, axis=-1, keepdims=True)
    is_g = lane < N_GROUPS
    zg = jnp.where(is_g, z, neg)
    gmax = rmax(zg)
    g_idx = rmin(jnp.where(zg == gmax, lane, big))
    g_w = 1.0 / jnp.sum(jnp.where(is_g, jnp.exp(zg - gmax), 0.0), axis=-1, keepdims=True)
    first = N_GROUPS + g_idx * E_PER_GROUP
    in_grp = jnp.logical_and(lane >= first, lane < first + E_PER_GROUP)
    v1 = jnp.where(in_grp, z, neg)
    top1 = rmax(v1)
    i1 = rmin(jnp.where(v1 == top1, lane, big))
    v2 = jnp.where(lane == i1, neg, v1)
    top2 = rmax(v2)
    i2 = rmin(jnp.where(v2 == top2, lane, big))
    e2 = jnp.exp(top2 - top1)
    den = 1.0 + e2
    return jnp.where(lane == i1, g_w / den, jnp.where(lane == i2, g_w * e2 / den, 0.0)), g_idx.astype(jnp.int32)


def _moe_kernel(x_ref, g_ref, wr_ref, br_ref, wg_ref, wu_ref, wd_ref, gf_ref, o_ref, xn_ref, cmb_ref, acc_ref,
                *, final_norm):
    e = pl.program_id(1)

    @pl.when(e == 0)
    def _():
        xn = _rms(x_ref[...], g_ref[...])
        xn_ref[...] = xn.astype(BF16)
        z = jnp.dot(xn, wr_ref[...], preferred_element_type=F32, precision=lax.Precision.HIGHEST) + br_ref[...]
        cmb_ref[...] = _route(z)[0]
        acc_ref[...] = jnp.zeros(acc_ref.shape, F32)

    xn = xn_ref[...]
    gate = jnp.dot(xn, wg_ref[0].astype(BF16), preferred_element_type=F32)
    up = jnp.dot(xn, wu_ref[0].astype(BF16), preferred_element_type=F32)
    hid = (gate * jax.nn.sigmoid(gate) * up).astype(BF16)
    y = jnp.dot(hid, wd_ref[0].astype(BF16), preferred_element_type=F32)
    lane = lax.broadcasted_iota(jnp.int32, cmb_ref.shape, 1)
    col = jnp.sum(jnp.where(lane == e + N_GROUPS, cmb_ref[...], 0.0), axis=-1, keepdims=True)
    acc_ref[...] = acc_ref[...] + col * y

    @pl.when(e == pl.num_programs(1) - 1)
    def _():
        out = x_ref[...] + acc_ref[...]
        o_ref[...] = _rms(out, gf_ref[...]) if final_norm else out


def _moe(x, g, wr, br, wg, wu, wd, gf, tm, final_norm):
    m, d = x.shape
    ne, _, dff = wg.shape
    row = pl.BlockSpec((tm, d), lambda i, e: (i, 0))
    full = lambda a: pl.BlockSpec(a.shape, lambda i, e: (0, 0))
    return pl.pallas_call(
        functools.partial(_moe_kernel, final_norm=final_norm),
        grid=(m // tm, ne),
        in_specs=[row, full(g), full(wr), full(br),
                  pl.BlockSpec((1, d, dff), lambda i, e: (e, 0, 0)),
                  pl.BlockSpec((1, d, dff), lambda i, e: (e, 0, 0)),
                  pl.BlockSpec((1, dff, d), lambda i, e: (e, 0, 0)),
                  full(gf)],
        out_specs=row,
        out_shape=jax.ShapeDtypeStruct((m, d), F32),
        scratch_shapes=[pltpu.VMEM((tm, d), BF16), pltpu.VMEM((tm, LANES), F32), pltpu.VMEM((tm, d), F32)],
        compiler_params=_cparams(("parallel", "arbitrary")),
        name="moe",
    )(x, g, wr, br, wg, wu, wd, gf)


MOE_WINDOW = 320


def _split3(a):
    hi = a.astype(BF16)
    r1 = a - hi.astype(F32)
    mid = r1.astype(BF16)
    return hi, mid, (r1 - mid.astype(F32)).astype(BF16)


def _moe_sorted_kernel(x_ref, g_ref, wr_ref, br_ref, wg_ref, wu_ref, wd_ref, gf_ref, o_ref,
                       xs_ref, cmbs_ref, acc_ref, pt_ref, rng_ref, *, final_norm, win):
    e = pl.program_id(1)
    tm, d = x_ref.shape

    @pl.when(e == 0)
    def _():
        xn = _rms(x_ref[...], g_ref[...])
        z = jnp.dot(xn, wr_ref[...], preferred_element_type=F32, precision=lax.Precision.HIGHEST) + br_ref[...]
        cmb, g_idx = _route(z)
        lane = lax.broadcasted_iota(jnp.int32, (tm, LANES), 1)
        onehot = jnp.where(lane == g_idx, 1.0, 0.0)
        r_i = lax.broadcasted_iota(jnp.int32, (tm, tm), 0)
        c_i = lax.broadcasted_iota(jnp.int32, (tm, tm), 1)
        rows = lax.broadcasted_iota(jnp.int32, (tm, LANES), 0)
        before = onehot
        step = 1
        while step < tm:
            before = before + jnp.where(rows >= step, pltpu.roll(before, step, 0), 0.0)
            step *= 2
        before = before - onehot
        counts = jnp.sum(onehot, axis=0, keepdims=True)
        starts = _lane_cumsum(counts) - counts
        pos = jnp.sum(jnp.where(lane == g_idx, before + starts, 0.0), axis=-1, keepdims=True)
        pos_row = jnp.broadcast_to(pos, (tm, LANES)).T[0:1].astype(jnp.int32)
        perm = jnp.where(r_i == pos_row, 1.0, 0.0).astype(BF16)
        pt_ref[...] = jnp.where(c_i == pos.astype(jnp.int32), 1.0, 0.0).astype(BF16)
        xs_ref[...] = jnp.dot(perm, xn.astype(BF16), preferred_element_type=F32).astype(BF16)
        cmbs_ref[...] = sum(jnp.dot(perm, t, preferred_element_type=F32) for t in _split3(cmb))
        acc_ref[...] = jnp.zeros(acc_ref.shape, F32)
        lane1 = lax.broadcasted_iota(jnp.int32, (1, LANES), 1)
        for grp in range(N_GROUPS):
            rng_ref[grp] = jnp.sum(jnp.where(lane1 == grp, starts, 0.0)).astype(jnp.int32)
            rng_ref[N_GROUPS + grp] = jnp.sum(jnp.where(lane1 == grp, counts, 0.0)).astype(jnp.int32)

    grp = e // E_PER_GROUP
    start = rng_ref[grp]
    end = start + rng_ref[N_GROUPS + grp]
    w0 = (start // 16) * 16
    lane = lax.broadcasted_iota(jnp.int32, (win, LANES), 1)
    row = lax.broadcasted_iota(jnp.int32, (win, 1), 0)

    wg, wu, wd = wg_ref[0].astype(BF16), wu_ref[0].astype(BF16), wd_ref[0].astype(BF16)

    def window(i, _):
        lo = w0 + i * win
        ws = pl.multiple_of(jnp.minimum(lo, tm - win), 16)
        xc = xs_ref[pl.ds(ws, win), :]
        gate = jnp.dot(xc, wg, preferred_element_type=F32)
        up = jnp.dot(xc, wu, preferred_element_type=F32)
        hid = (gate * jax.nn.sigmoid(gate) * up).astype(BF16)
        y = jnp.dot(hid, wd, preferred_element_type=F32)
        col = jnp.sum(jnp.where(lane == e + N_GROUPS, cmbs_ref[pl.ds(ws, win), :], 0.0), axis=-1, keepdims=True)
        srt = ws + row
        mine = jnp.logical_and(srt >= jnp.maximum(lo, start), srt < jnp.minimum(lo + win, end))
        acc_ref[pl.ds(ws, win), :] = acc_ref[pl.ds(ws, win), :] + jnp.where(mine, col, 0.0) * y
        return 0

    lax.fori_loop(0, (end - w0 + win - 1) // win, window, 0)

    @pl.when(e == pl.num_programs(1) - 1)
    def _():
        acc = acc_ref[...]
        hi = acc.astype(BF16)
        lo = (acc - hi.astype(F32)).astype(BF16)
        moe = jnp.dot(pt_ref[...], hi, preferred_element_type=F32) + jnp.dot(pt_ref[...], lo, preferred_element_type=F32)
        out = x_ref[...] + moe
        o_ref[...] = _rms(out, gf_ref[...]) if final_norm else out


def _moe_sorted(x, g, wr, br, wg, wu, wd, gf, tm, final_norm):
    m, d = x.shape
    ne, _, dff = wg.shape
    row = pl.BlockSpec((tm, d), lambda i, e: (i, 0))
    full = lambda a: pl.BlockSpec(a.shape, lambda i, e: (0, 0))
    return pl.pallas_call(
        functools.partial(_moe_sorted_kernel, final_norm=final_norm, win=MOE_WINDOW),
        grid=(m // tm, ne),
        in_specs=[row, full(g), full(wr), full(br),
                  pl.BlockSpec((1, d, dff), lambda i, e: (e, 0, 0)),
                  pl.BlockSpec((1, d, dff), lambda i, e: (e, 0, 0)),
                  pl.BlockSpec((1, dff, d), lambda i, e: (e, 0, 0)),
                  full(gf)],
        out_specs=row,
        out_shape=jax.ShapeDtypeStruct((m, d), F32),
        scratch_shapes=[pltpu.VMEM((tm, d), BF16), pltpu.VMEM((tm, LANES), F32), pltpu.VMEM((tm, d), F32),
                        pltpu.VMEM((tm, tm), BF16), pltpu.SMEM((2 * N_GROUPS,), jnp.int32)],
        compiler_params=_cparams(("parallel", "arbitrary")),
        name="moe_sorted",
    )(x, g, wr, br, wg, wu, wd, gf)


def _select_topk(sc_ref, nc, n_sel, n_valid, key_axis):
    _, d0, d1 = sc_ref.shape
    static = isinstance(nc, int)
    chunk_keys = (d0, d1)[key_axis]
    kf = jnp.float32(n_sel)
    inf = jnp.inf
    key_off = lax.broadcasted_iota(jnp.int32, (d0, d1), key_axis)
    stat_shape = (1, d1) if key_axis == 0 else (d0, 1)
    acc_rows = 4 * SUBLANES if d0 % (4 * SUBLANES) == 0 else SUBLANES
    reducers = {"sum": (jnp.add, jnp.sum, 0.0), "min": (jnp.minimum, jnp.min, inf), "max": (jnp.maximum, jnp.max, -inf)}

    def fold(fn, kind):
        comb, red, init = reducers[kind]

        def narrow(a):
            if key_axis == 0:
                return red(a.reshape(d0 // acc_rows, acc_rows, d1), axis=0)
            out = a[:, :LANES]
            for b in range(1, d1 // LANES):
                out = comb(out, a[:, b * LANES:(b + 1) * LANES])
            return out

        if static:
            ch = lax.broadcasted_iota(jnp.int32, (nc, d0, d1), 0)
            part = narrow(red(fn(sc_ref[0:nc], ch), axis=0))
        else:
            part = lax.fori_loop(0, nc, lambda ch, a: comb(a, narrow(fn(sc_ref[ch], ch))),
                                 jnp.full((acc_rows, d1) if key_axis == 0 else (d0, LANES), init, F32))
        return red(part, axis=key_axis, keepdims=True)

    def count(pred):
        return fold(lambda x, ch: jnp.where(pred(x), 1.0, 0.0), "sum")

    def any_query(flag):
        return jnp.max(jnp.where(flag, 1.0, 0.0)) > 0.5

    mx = fold(lambda x, ch: x, "max")
    mn = fold(lambda x, ch: jnp.where(x > -inf, x, inf), "min")
    take_all = n_valid < kf

    def bisect(_, lh):
        lo, hi = lh
        mid = 0.5 * (lo + hi)
        ge = count(lambda x: x >= mid) >= kf
        return jnp.where(ge, mid, lo), jnp.where(ge, hi, mid)

    lo, _ = lax.fori_loop(0, BISECT_ITERS, bisect, (mn, mx))

    cand = fold(lambda x, ch: jnp.where(x >= lo, x, inf), "min")
    n_gt = count(lambda x: x > cand)

    def unsettled(state):
        return any_query(jnp.logical_and(state[1] >= kf, jnp.logical_not(take_all)))

    def walk(state):
        cd, g = state
        nxt = fold(lambda x, ch: jnp.where(x > cd, x, inf), "min")
        cd = jnp.where(g >= kf, nxt, cd)
        return cd, count(lambda x: x > cd)

    cand, n_gt = lax.while_loop(unsettled, walk, (cand, n_gt))
    thr = jnp.where(take_all, -inf, cand)
    n_gt = jnp.where(take_all, n_valid, n_gt)
    need = kf - n_gt
    n_eq = count(lambda x: x == thr)
    has_excess = any_query(jnp.logical_and(n_eq > need, jnp.logical_not(take_all)))

    def tie_cut():
        def bit_step(b, ans):
            cnd = ans + jnp.left_shift(jnp.int32(1), POS_BITS - 1 - b)
            in_front = lambda x, ch: jnp.where(x == thr, jnp.where((ch * chunk_keys + key_off) < cnd, 1.0, 0.0), 0.0)
            return jnp.where(fold(in_front, "sum") < need, cnd, ans)
        return lax.fori_loop(0, POS_BITS, bit_step, jnp.zeros(stat_shape, jnp.int32))

    cut = lax.cond(has_excess, tie_cut, lambda: jnp.full(stat_shape, (1 << POS_BITS) - 1, jnp.int32))

    def bias_of(x, ch):
        tie = jnp.where(x == thr, jnp.where((ch * chunk_keys + key_off) <= cut, 0.0, MASK_VALUE), MASK_VALUE)
        return jnp.where(x > -inf, jnp.where(x > thr, 0.0, tie), MASK_VALUE)

    if static:
        sc_ref[0:nc] = bias_of(sc_ref[0:nc], lax.broadcasted_iota(jnp.int32, (nc, d0, d1), 0))
    else:
        def write(ch, _):
            sc_ref[ch] = bias_of(sc_ref[ch], ch)
            return 0
        lax.fori_loop(0, nc, write, 0)


def _dsa_prompt_kernel(qit_ref, wit_ref, kidx_ref, qt_ref, k_ref, vt_ref, o_ref, sc_ref, m_ref, acc_ref, sa_ref, sb_ref,
                       *, tq, kc, n_sel, rep):
    i = pl.program_id(1)
    nc = ((i + 1) * tq + kc - 1) // kc
    nh = qt_ref.shape[1]
    nkv = vt_ref.shape[1]
    q_pos = i * tq + lax.broadcasted_iota(jnp.int32, (kc, tq), 1)
    k_iota = lax.broadcasted_iota(jnp.int32, (kc, tq), 0)
    qi_all = jnp.concatenate([qit_ref[0, h] for h in range(H_IDX)], axis=1)
    wi_all = jnp.concatenate([wit_ref[0, h:h + 1, :] for h in range(H_IDX)], axis=1)

    def score_chunk(ch, _):
        start = pl.multiple_of(ch * kc, kc)
        d = jnp.dot(kidx_ref[0, pl.ds(start, kc), :].astype(BF16), qi_all, preferred_element_type=F32)
        r = jnp.maximum(d, 0.0) * wi_all
        acc = r[:, :tq]
        for h in range(1, H_IDX):
            acc = acc + r[:, h * tq:(h + 1) * tq]
        sc_ref[ch] = jnp.where(start + k_iota <= q_pos, acc, -jnp.inf)
        return 0

    lax.fori_loop(0, nc, score_chunk, 0)
    n_valid = (i * tq + 1 + lax.broadcasted_iota(jnp.int32, (1, tq), 1)).astype(F32)
    _select_topk(sc_ref, nc, n_sel, n_valid, 0)

    m_ref[...] = jnp.full(m_ref.shape, MASK_VALUE, F32)
    acc_ref[...] = jnp.zeros(acc_ref.shape, F32)

    sub = kc // 2
    q_groups = lambda g: jnp.concatenate([qt_ref[0, g * rep + r] for r in range(rep)], axis=1)

    def logits(ch, half, s_ref):
        start = pl.multiple_of(ch * kc + half * sub, sub)
        for g in range(nkv):
            kblk = k_ref[0, pl.ds(start, sub), (g // 2) * LANES:(g // 2 + 1) * LANES].astype(BF16)
            s_ref[g] = jnp.dot(kblk, q_groups(g), preferred_element_type=F32)

    def softmax_pv(ch, half, s_ref):
        start = pl.multiple_of(ch * kc + half * sub, sub)
        bias = jnp.tile(sc_ref[ch, half * sub:(half + 1) * sub, :], (1, rep))
        for g in range(nkv):
            _flash_step_t(s_ref[g] + bias, vt_ref[0, g, :, pl.ds(start, sub)], m_ref, acc_ref, g)

    logits(0, 0, sa_ref)

    def attend(ch, _):
        logits(ch, 1, sb_ref)
        softmax_pv(ch, 0, sa_ref)
        logits(jnp.minimum(ch + 1, nc - 1), 0, sa_ref)
        softmax_pv(ch, 1, sb_ref)
        return 0

    lax.fori_loop(0, nc, attend, 0)
    for h in range(0, nh, 2):
        pair = []
        for hh in (h, h + 1):
            g, r = hh // rep, hh % rep
            cols = slice(r * tq, (r + 1) * tq)
            pair.append(acc_ref[g, :HEAD_DIM, cols] / acc_ref[g, HEAD_DIM:HEAD_DIM + 1, cols])
        o_ref[0, :, h * HEAD_DIM:(h + 2) * HEAD_DIM] = jnp.concatenate(pair, axis=0).T


def _dsa_prompt(qit, wit, kidx, qt, k, vt, tq, kc, n_sel):
    n, nh, _, t = qt.shape
    nkv = vt.shape[1]
    rep = nh // nkv
    vrows = vt.shape[2]
    arows = vrows + DENOM_ROWS
    return pl.pallas_call(
        functools.partial(_dsa_prompt_kernel, tq=tq, kc=kc, n_sel=n_sel, rep=rep),
        grid=(n, t // tq),
        in_specs=[pl.BlockSpec((1, H_IDX, 2 * D_IDX, tq), lambda b, i: (b, 0, 0, i)),
                  pl.BlockSpec((1, H_IDX, tq), lambda b, i: (b, 0, i)),
                  pl.BlockSpec((1, t, 2 * D_IDX), lambda b, i: (b, 0, 0)),
                  pl.BlockSpec((1, nh, 2 * HEAD_DIM, tq), lambda b, i: (b, 0, 0, i)),
                  pl.BlockSpec((1, t, nkv * HEAD_DIM), lambda b, i: (b, 0, 0)),
                  pl.BlockSpec((1, nkv, vrows, t), lambda b, i: (b, 0, 0, 0))],
        out_specs=pl.BlockSpec((1, tq, nh * HEAD_DIM), lambda b, i: (b, i, 0)),
        out_shape=jax.ShapeDtypeStruct((n, t, nh * HEAD_DIM), F32),
        scratch_shapes=[pltpu.VMEM((t // kc, kc, tq), F32),
                        pltpu.VMEM((nkv, 1, rep * tq), F32), pltpu.VMEM((nkv, arows, rep * tq), F32),
                        pltpu.VMEM((nkv, kc // 2, rep * tq), F32), pltpu.VMEM((nkv, kc // 2, rep * tq), F32)],
        compiler_params=_cparams(("parallel", "arbitrary")),
        name="dsa_prompt",
    )(qit, wit, kidx, qt, k, vt)


def _dsa_sample_score_kernel(pt_ref, qi_ref, wi_ref, *refs, pp, n_steps, page, tq, n_sel):
    del pt_ref
    kp = refs[:pp]
    kn_ref, bp_ref, bn_ref, sc_ref = refs[pp:]
    j = pl.program_id(1)
    n_pages = n_steps * pp
    q_off = lax.broadcasted_iota(jnp.int32, (tq, page), 0)
    k_off = lax.broadcasted_iota(jnp.int32, (tq, page), 1)

    def score(kpage_t, pg, new):
        d = jnp.dot(qi_ref[0], kpage_t.astype(BF16), preferred_element_type=F32)
        r = (jnp.maximum(d, 0.0) * wi_ref[0]).reshape(H_IDX, tq, page)
        acc = r[0]
        for h in range(1, H_IDX):
            acc = acc + r[h]
        sc_ref[pg] = jnp.where(k_off <= q_off, acc, -jnp.inf) if new else acc

    @pl.when(j < n_steps)
    def _():
        for p in range(pp):
            score(kp[p][0, 0], j * pp + p, False)

    @pl.when(j == n_steps)
    def _():
        score(kn_ref[0], n_pages, True)
        n_valid = (n_pages * page + 1 + lax.broadcasted_iota(jnp.int32, (tq, 1), 0)).astype(F32)
        _select_topk(sc_ref, n_pages + 1, n_sel, n_valid, 1)
        bp_ref[0] = sc_ref[0:n_pages]
        bn_ref[0] = sc_ref[n_pages]


def _dsa_sample_scores(page_table, qi, wi, kidx_cache_t, kidx_new_t, tq, n_sel, pp):
    nb, n_pages = page_table.shape
    page = kidx_cache_t.shape[3]
    n_steps = n_pages // pp
    rows = qi.shape[1]
    seq3 = lambda b, j, pt: (b, 0, 0)
    grid_spec = pltpu.PrefetchScalarGridSpec(
        num_scalar_prefetch=1,
        grid=(nb, n_steps + 1),
        in_specs=[pl.BlockSpec((1, rows, D_IDX), seq3), pl.BlockSpec((1, rows, 1), seq3)]
                 + _paged_specs(pp, n_steps, (1, 1, D_IDX, page), lambda pg: (0, pg, 0, 0))
                 + [pl.BlockSpec((1, D_IDX, page), seq3)],
        out_specs=[pl.BlockSpec((1, n_pages, tq, page), lambda b, j, pt: (b, 0, 0, 0)),
                   pl.BlockSpec((1, tq, page), seq3)],
        scratch_shapes=[pltpu.VMEM((n_pages + 1, tq, page), F32)],
    )
    return pl.pallas_call(
        functools.partial(_dsa_sample_score_kernel, pp=pp, n_steps=n_steps, page=page, tq=tq, n_sel=n_sel),
        grid_spec=grid_spec,
        out_shape=[jax.ShapeDtypeStruct((nb, n_pages, tq, page), F32), jax.ShapeDtypeStruct((nb, tq, page), F32)],
        compiler_params=_cparams(("parallel", "arbitrary")),
        name="dsa_sample_scores",
    )(page_table, qi, wi, *([kidx_cache_t] * pp), kidx_new_t)


def _dsa_sample_attn_kernel(pt_ref, wq_ref, *refs, pp, n_steps, tq, nh, rep):
    del pt_ref
    kp, vp = refs[:pp], refs[pp:2 * pp]
    kn_ref, vn_ref, bp_ref, bn_ref, o_ref, m_ref, l_ref, acc_ref = refs[2 * pp:]
    j = pl.program_id(1)

    @pl.when(j == 0)
    def _():
        _init_softmax(m_ref, l_ref, acc_ref)

    ckv = acc_ref.shape[1]

    def logits(kt, bias):
        s = jnp.dot(wq_ref[0], kt.reshape(ckv, -1).astype(BF16), preferred_element_type=F32)
        return s + jnp.tile(bias, (nh, 1))

    @pl.when(j < n_steps)
    def _():
        ss = [logits(kp[p][0, 0], bp_ref[0, p]) for p in range(pp)]
        vts = [vp[p][0, 0].reshape(ckv, -1).astype(BF16) for p in range(pp)]
        _softmax_update(ss, vts, m_ref, l_ref, acc_ref)

    @pl.when(j == n_steps)
    def _():
        _softmax_update([logits(kn_ref[0], bn_ref[0])], [vn_ref[0].astype(BF16)], m_ref, l_ref, acc_ref)
        out = acc_ref[...] / l_ref[...]
        for h in range(nh):
            g = h // rep
            o_ref[0, :, h * HEAD_DIM:(h + 1) * HEAD_DIM] = out[h * tq:(h + 1) * tq, g * HEAD_DIM:(g + 1) * HEAD_DIM]


def _dsa_sample_attn(page_table, wq, kcache_t, vcache_t, knew_t, vnew_t, bias_pages, bias_new, tq, nh, pp):
    nb, n_pages = page_table.shape
    _, _, nkv, hd, page = kcache_t.shape
    ckv = nkv * hd
    n_steps = n_pages // pp
    rows = wq.shape[1]
    seq3 = lambda b, j, pt: (b, 0, 0)
    kv_specs = lambda: _paged_specs(pp, n_steps, (1, 1, nkv, hd, page), lambda pg: (0, pg, 0, 0, 0))
    grid_spec = pltpu.PrefetchScalarGridSpec(
        num_scalar_prefetch=1,
        grid=(nb, n_steps + 1),
        in_specs=[pl.BlockSpec((1, rows, ckv), seq3)] + kv_specs() + kv_specs()
                 + [pl.BlockSpec((1, ckv, page), seq3), pl.BlockSpec((1, ckv, page), seq3),
                    pl.BlockSpec((1, pp, tq, page), lambda b, j, pt: (b, jnp.minimum(j, n_steps - 1), 0, 0)),
                    pl.BlockSpec((1, tq, page), seq3)],
        out_specs=pl.BlockSpec((1, tq, nh * HEAD_DIM), seq3),
        scratch_shapes=[pltpu.VMEM((rows, 1), F32), pltpu.VMEM((rows, 1), F32), pltpu.VMEM((rows, ckv), F32)],
    )
    return pl.pallas_call(
        functools.partial(_dsa_sample_attn_kernel, pp=pp, n_steps=n_steps, tq=tq, nh=nh, rep=nh * HEAD_DIM // ckv),
        grid_spec=grid_spec,
        out_shape=jax.ShapeDtypeStruct((nb, tq, nh * HEAD_DIM), F32),
        compiler_params=_cparams(("parallel", "arbitrary")),
        name="dsa_sample_attn",
    )(page_table, wq, *([kcache_t] * pp), *([vcache_t] * pp), knew_t, vnew_t, bias_pages, bias_new)


def _pad_cols(a, n):
    return jnp.pad(a, [(0, 0)] * (a.ndim - 1) + [(0, n - a.shape[-1])])


def _rope_tables(pos):
    half = HEAD_DIM // 2
    inv = ROPE_THETA ** (-jnp.arange(half, dtype=F32) / half)
    ang = pos.astype(F32)[:, None] * inv[None, :]
    cos, sin = jnp.cos(ang), jnp.sin(ang)
    return jnp.tile(cos, (1, LANES // half)), jnp.tile(jnp.concatenate([-sin, sin], axis=1), (1, LANES // HEAD_DIM))


def _block_diag_queries(q, n_heads, n_kv):
    b, t, _ = q.shape
    qh = q.reshape(b, t, n_heads, HEAD_DIM).transpose(0, 2, 1, 3)
    onehot = jax.nn.one_hot(jnp.arange(n_heads) // (n_heads // n_kv), n_kv, dtype=q.dtype)
    w = qh[:, :, :, None, :] * onehot[None, :, None, :, None]
    return w.reshape(b, n_heads * t, n_kv * HEAD_DIM).astype(BF16)


def _tile(m, pref):
    for t in pref:
        if m % t == 0:
            return t
    return m


def kernel(x_prompt, x_sample, state_conv, cache_fox_k, cache_fox_v, cache_fox_logf, cache_dsa_k, cache_dsa_v,
           cache_dsa_kidx, page_table, norm_mix_even, w_in_even, b_forget, w_dw, b_dw, ln_conv_g, ln_conv_b,
           w_out_even, norm_mix_odd, w_in_odd, w_out_odd, norm_ffn, w_group, b_group, w_router, b_router, w_gate,
           w_up, w_down, norm_final):
    n_p, t_p, d = x_prompt.shape
    n_s, t_s, _ = x_sample.shape
    depth = norm_ffn.shape[0]
    page = cache_fox_k.shape[2]
    n_pages = page_table.shape[1]
    past = n_pages * page
    c_a = w_dw.shape[2]
    h_b = b_forget.shape[1]
    c_b = h_b * HEAD_DIM
    kv_c = cache_dsa_k.shape[3]
    h_c = w_out_odd.shape[1] // HEAD_DIM
    nq, nkv, nqi = h_c * HEAD_DIM, kv_c * HEAD_DIM, H_IDX * D_IDX
    assert t_s == SUBLANES and depth == 2 and w_in_even.shape[0] == 1 and w_in_odd.shape[0] == 1

    mp, ms = n_p * t_p, n_s * t_s
    hp = x_prompt.reshape(mp, d)
    hs = x_sample.reshape(ms, d)
    tm_p = _tile(mp, (512, 256, 128))
    row = lambda a: a.reshape(1, -1)

    def moe_layer(h, layer, tm, final):
        wr = _pad_cols(jnp.concatenate([w_group[layer], w_router[layer]], axis=1), LANES)
        br = _pad_cols(jnp.concatenate([b_group[layer], b_router[layer]]).reshape(1, -1), LANES)
        sort_tokens = tm >= 2 * MOE_WINDOW and (tm - MOE_WINDOW) % 16 == 0
        return (_moe_sorted if sort_tokens else _moe)(
            h, row(norm_ffn[layer]), wr, br, w_gate[layer], w_up[layer], w_down[layer], row(norm_final), tm, final)

    tm_seq = _tile(t_p, (512, 256, 128))
    w_e = w_in_even[0]
    w_e = jnp.concatenate([w_e[:, :2 * c_a + 3 * c_b], _pad_cols(w_e[:, 2 * c_a + 3 * c_b:], LANES)], axis=1).astype(BF16)
    bf = _pad_cols(b_forget[0].reshape(1, -1), LANES)
    g_e = row(norm_mix_even[0])
    ag_p, k_p, qt_p, kt_p, vt_p, lft_p = _proj_even(hp, g_e, w_e, bf, c_a, c_b, tm_seq, t_p)
    ag_s, q_s, k_s, v_s, lf_s = _proj_even(hs, g_e, w_e, bf, c_a, c_b, ms)
    logf_s = lf_s[:, :h_b].reshape(n_s, t_s, h_b)

    conv_args = (w_dw[0], row(b_dw[0]), row(ln_conv_g[0]), row(ln_conv_b[0]))
    a_p, cst_p = _conv_module(ag_p.reshape(n_p, t_p, 2 * c_a), jnp.zeros((n_p, HIST, c_a), F32), *conv_args, tm_seq)
    a_s, cst_s = _conv_module(ag_s.reshape(n_s, t_s, 2 * c_a), state_conv[0], *conv_args, t_s)

    b_p = _fox_prompt(qt_p, k_p.reshape(n_p, t_p, c_b), vt_p, _seq_cumsum_columns(lft_p),
                      _tile(t_p, (512, 256, 128)), _tile(t_p, (256, 128)))

    n_pool = cache_fox_k.shape[1]
    pp = _tile(n_pages, (PAGES_PER_STEP, 4, 2, 1))
    keys_minor = lambda a: jnp.moveaxis(a, 2, -1)
    new_t = lambda a, f: _pad_cols(a.reshape(n_s, t_s, f).transpose(0, 2, 1), page)
    cpool = _page_cumsum(keys_minor(cache_fox_logf)[0], _tile(n_pool, (64, 32, 16, 8, 4, 2)))
    cnew = _page_cumsum(new_t(logf_s, h_b), _tile(n_s, (32, 16, 8, 4, 2)))
    b_s = _fox_sample(page_table, _block_diag_queries(q_s.reshape(n_s, t_s, c_b), h_b, h_b),
                      keys_minor(cache_fox_k), keys_minor(cache_fox_v), cpool,
                      new_t(k_s, c_b), new_t(v_s, c_b), cnew, t_s, pp)

    w_oe = w_out_even[0].astype(BF16)
    hp = _outproj([a_p.reshape(mp, c_a), b_p.reshape(mp, c_b)], w_oe, hp, tm_p)
    hs = _outproj([a_s.reshape(ms, c_a), b_s.reshape(ms, c_b)], w_oe, hs, ms)
    hp = moe_layer(hp, 0, _tile(mp, (1024, 512, 256, 128)), False)
    hs = moe_layer(hs, 0, ms, False)

    w_o = w_in_odd[0]
    s4 = nq + 2 * nkv + nqi
    w_o = jnp.concatenate([w_o[:, :s4], _pad_cols(w_o[:, s4:s4 + D_IDX], LANES), _pad_cols(w_o[:, s4 + D_IDX:], LANES)],
                          axis=1).astype(BF16)
    g_o = row(norm_mix_odd[0])
    wi_scale = (H_IDX ** -0.5) * (D_IDX ** -0.5)
    cos_p, sin_p = _rope_tables(jnp.arange(t_p))
    cos_s, sin_s = _rope_tables(jnp.tile(past + jnp.arange(t_s), n_s))
    dqt_p, dk_p, dkt_p, dvt_p, dqit_p, dki_p, dkit_p, dwit_p = _proj_odd(hp, g_o, w_o, cos_p, sin_p, nq, nkv, nqi,
                                                                         wi_scale, tm_seq, t_p)
    dq_s, dk_s, dv_s, dqi_s, dki_s, dwi_s = _proj_odd(hs, g_o, w_o, cos_s, sin_s, nq, nkv, nqi, wi_scale, ms)
    dki_s = dki_s[:, :D_IDX]

    n_sel_p = min(TOPK_MAX, t_p // 4)
    o_p = _dsa_prompt(dqit_p, dwit_p, dki_p.reshape(n_p, t_p, LANES), dqt_p, dk_p.reshape(n_p, t_p, nkv), dvt_p,
                      _tile(t_p, (256, 128)), _tile(t_p, (256, 128)), n_sel_p)

    n_sel_s = min(TOPK_MAX, (past + t_s) // 4)
    heads = lambda a, n, nh: a.reshape(n, -1, nh, HEAD_DIM).transpose(0, 2, 1, 3).astype(BF16)
    qi_rows = heads(dqi_s, n_s, H_IDX).reshape(n_s, H_IDX * t_s, D_IDX)
    wi_rows = dwi_s[:, :H_IDX].reshape(n_s, t_s, H_IDX).transpose(0, 2, 1).reshape(n_s, H_IDX * t_s, 1)
    bias_pages, bias_new = _dsa_sample_scores(page_table, qi_rows, wi_rows, keys_minor(cache_dsa_kidx),
                                              new_t(dki_s, D_IDX), t_s, n_sel_s, pp)
    o_s = _dsa_sample_attn(page_table, _block_diag_queries(dq_s.reshape(n_s, t_s, nq), h_c, kv_c),
                           keys_minor(cache_dsa_k), keys_minor(cache_dsa_v), new_t(dk_s, nkv), new_t(dv_s, nkv),
                           bias_pages, bias_new, t_s, h_c, pp)

    w_oo = w_out_odd[0].astype(BF16)
    hp = _outproj([o_p.reshape(mp, nq)], w_oo, hp, tm_p)
    hs = _outproj([o_s.reshape(ms, nq)], w_oo, hs, ms)
    hp = moe_layer(hp, 1, _tile(mp, (1024, 512, 256, 128)), True)
    hs = moe_layer(hs, 1, ms, True)

    t_minor = lambda a: jnp.moveaxis(a, -1, 1)[None]
    return (hp.reshape(n_p, t_p, d), hs.reshape(n_s, t_s, d),
            cst_p[None], cst_s[None],
            t_minor(kt_p), k_s.reshape(1, n_s, t_s, h_b, HEAD_DIM),
            t_minor(vt_p), v_s.reshape(1, n_s, t_s, h_b, HEAD_DIM),
            t_minor(lft_p), logf_s[None],
            t_minor(dkt_p), dk_s.reshape(1, n_s, t_s, kv_c, HEAD_DIM),
            t_minor(dvt_p), dv_s.reshape(1, n_s, t_s, kv_c, HEAD_DIM),
            t_minor(dkit_p), dki_s.reshape(1, n_s, t_s, D_IDX))
```

```python
import functools

import jax
import jax.numpy as jnp
from jax import lax
from jax.experimental import pallas as pl
from jax.experimental.pallas import tpu as pltpu

F32 = jnp.float32
BF16 = jnp.bfloat16

HEAD_DIM = 64
CONV_W = 31
H_IDX = 8
D_IDX = 64
TOPK_MAX = 256
N_GROUPS = 4
E_PER_GROUP = 4
N_EXPERTS = N_GROUPS * E_PER_GROUP
ROPE_THETA = 10000.0
RMS_EPS = 1e-6
LN_EPS = 1e-5

LANES = 128
SUBLANES = 8
VMEM_LIMIT = 56 * 1024 * 1024

MASK_VALUE = -1e30
HIST = CONV_W - 1
HIST_PAD = 32
BISECT_ITERS = 20
POS_BITS = 14
PAGES_PER_STEP = 16
LOG2E = 1.4426950408889634

_NT = (((1,), (1,)), ((), ()))


def _cparams(sem):
    return pltpu.CompilerParams(dimension_semantics=sem, vmem_limit_bytes=VMEM_LIMIT)


def _rms(x, g):
    ms = jnp.mean(x * x, axis=-1, keepdims=True)
    return x * lax.rsqrt(ms + RMS_EPS) * g


def _rope128(xb, cos, sin_signed):
    lane = lax.broadcasted_iota(jnp.int32, xb.shape, 1)
    first_half = jnp.bitwise_and(lane, HEAD_DIM - 1) < (HEAD_DIM // 2)
    rot = jnp.where(first_half, pltpu.roll(xb, LANES - HEAD_DIM // 2, 1), pltpu.roll(xb, HEAD_DIM // 2, 1))
    return xb * cos + rot * sin_signed


def _split_heads_t(tile_t):
    return tile_t.reshape(2, HEAD_DIM, tile_t.shape[1])


def _zero_padded_pair(tile_t, first_slot):
    top, bot = tile_t[:HEAD_DIM], tile_t[HEAD_DIM:]
    zero = jnp.zeros_like(top)
    place = lambda x, slot: jnp.concatenate([x, zero] if slot == 0 else [zero, x], axis=0).astype(BF16)
    return place(top, first_slot[0]), place(bot, first_slot[1])


def _proj_even_kernel(x_ref, g_ref, w_ref, bf_ref, *out_refs, c_a, c_b, lane_major):
    h = _rms(x_ref[...], g_ref[...]).astype(BF16)

    def seg(a, b):
        return jnp.dot(h, w_ref[:, a:b], preferred_element_type=F32)

    o = 2 * c_a
    z = seg(o + 3 * c_b, o + 3 * c_b + LANES) + bf_ref[...]
    logf = jnp.minimum(z, 0.0) - jnp.log1p(jnp.exp(-jnp.abs(z)))
    scale = HEAD_DIM ** -0.5
    if not lane_major:
        ag_ref, q_ref, k_ref, v_ref, lf_ref = out_refs
        ag_ref[...] = seg(0, o)
        q_ref[...] = seg(o, o + c_b) * scale
        k_ref[...] = seg(o + c_b, o + 2 * c_b)
        v_ref[...] = seg(o + 2 * c_b, o + 3 * c_b)
        lf_ref[...] = logf
        return
    ag_ref, k_ref, qt_ref, kt_ref, vt_ref, lft_ref = out_refs
    ag_ref[...] = seg(0, o)
    for b in range(c_b // LANES):
        cols = lambda base: (base + b * LANES, base + (b + 1) * LANES)
        qa, qb = _zero_padded_pair((seg(*cols(o)) * (scale * LOG2E)).T, (0, 1))
        qt_ref[0, 2 * b] = qa
        qt_ref[0, 2 * b + 1] = qb
        k = seg(*cols(o + c_b))
        k_ref[:, b * LANES:(b + 1) * LANES] = k
        kt_ref[0, 2 * b:2 * b + 2] = _split_heads_t(k.T)
        vt_ref[0, 2 * b:2 * b + 2] = _split_heads_t(seg(*cols(o + 2 * c_b)).T)
    lft_ref[0] = logf.T[:lft_ref.shape[1]]


def _proj_even(x, g, w, bf, c_a, c_b, tm, seq_len=None):
    m, d = x.shape
    nh = c_b // HEAD_DIM
    row = lambda n: pl.BlockSpec((tm, n), lambda i: (i, 0))
    full = lambda a: pl.BlockSpec(a.shape, lambda i: (0, 0))
    if seq_len is None:
        out_specs = [row(2 * c_a), row(c_b), row(c_b), row(c_b), row(LANES)]
        out_shape = [jax.ShapeDtypeStruct((m, n), F32) for n in (2 * c_a, c_b, c_b, c_b, LANES)]
    else:
        nper, n = seq_len // tm, m // seq_len
        lm = lambda rows: pl.BlockSpec((1, nh, rows, tm), lambda i: (i // nper, 0, 0, i % nper))
        out_specs = [row(2 * c_a), row(c_b), lm(2 * HEAD_DIM), lm(HEAD_DIM), lm(HEAD_DIM),
                     pl.BlockSpec((1, nh, tm), lambda i: (i // nper, 0, i % nper))]
        out_shape = [jax.ShapeDtypeStruct((m, 2 * c_a), F32), jax.ShapeDtypeStruct((m, c_b), F32),
                     jax.ShapeDtypeStruct((n, nh, 2 * HEAD_DIM, seq_len), BF16),
                     jax.ShapeDtypeStruct((n, nh, HEAD_DIM, seq_len), F32),
                     jax.ShapeDtypeStruct((n, nh, HEAD_DIM, seq_len), F32),
                     jax.ShapeDtypeStruct((n, nh, seq_len), F32)]
    return pl.pallas_call(
        functools.partial(_proj_even_kernel, c_a=c_a, c_b=c_b, lane_major=seq_len is not None),
        grid=(m // tm,),
        in_specs=[row(d), full(g), full(w), full(bf)],
        out_specs=out_specs, out_shape=out_shape,
        compiler_params=_cparams(("parallel",)),
        name="proj_even",
    )(x, g, w, bf)


def _proj_odd_kernel(x_ref, g_ref, w_ref, cos_ref, sin_ref, *out_refs, nq, nkv, nqi, wi_scale, lane_major):
    h = _rms(x_ref[...], g_ref[...]).astype(BF16)
    cos = cos_ref[...]
    sin = sin_ref[...]
    scale = HEAD_DIM ** -0.5

    def seg(a, b):
        return jnp.dot(h, w_ref[:, a:b], preferred_element_type=F32)

    def roped(a, c):
        return _rope128(seg(a + c * LANES, a + (c + 1) * LANES), cos, sin)

    o_k, o_v, o_qi = nq, nq + nkv, nq + 2 * nkv
    o_ki = o_qi + nqi
    wi = seg(o_ki + LANES, o_ki + 2 * LANES) * wi_scale
    if not lane_major:
        q_ref, k_ref, v_ref, qi_ref, ki_ref, wi_ref = out_refs
        for c in range(nq // LANES):
            q_ref[:, c * LANES:(c + 1) * LANES] = roped(0, c) * scale
        for c in range(nkv // LANES):
            k_ref[:, c * LANES:(c + 1) * LANES] = roped(o_k, c)
        v_ref[...] = seg(o_v, o_v + nkv)
        for c in range(nqi // LANES):
            qi_ref[:, c * LANES:(c + 1) * LANES] = roped(o_qi, c)
        ki_ref[...] = roped(o_ki, 0)
        wi_ref[...] = wi
        return
    qt_ref, k_ref, kt_ref, vt_ref, qit_ref, ki_ref, kit_ref, wit_ref = out_refs
    rep = nq // nkv
    for c in range(nq // LANES):
        slot = ((2 * c) // rep) % 2
        qa, qb = _zero_padded_pair((roped(0, c) * (scale * LOG2E)).T, (slot, slot))
        qt_ref[0, 2 * c] = qa
        qt_ref[0, 2 * c + 1] = qb
    for c in range(nkv // LANES):
        k = roped(o_k, c)
        k_ref[:, c * LANES:(c + 1) * LANES] = k
        kt_ref[0, 2 * c:2 * c + 2] = _split_heads_t(k.T)
        vt_ref[0, 2 * c:2 * c + 2] = _split_heads_t(seg(o_v + c * LANES, o_v + (c + 1) * LANES).T)
    for c in range(nqi // LANES):
        qa, qb = _zero_padded_pair(roped(o_qi, c).T, (0, 0))
        qit_ref[0, 2 * c] = qa
        qit_ref[0, 2 * c + 1] = qb
    ki = roped(o_ki, 0)
    ki_ref[...] = ki
    kit_ref[0] = ki.T[:D_IDX]
    wit_ref[0] = wi.T[:wit_ref.shape[1]]


def _proj_odd(x, g, w, cos, sin, nq, nkv, nqi, wi_scale, tm, seq_len=None):
    m, d = x.shape
    nper = cos.shape[0] // tm
    row = lambda n: pl.BlockSpec((tm, n), lambda i: (i, 0))
    full = lambda a: pl.BlockSpec(a.shape, lambda i: (0, 0))
    tab = pl.BlockSpec((tm, LANES), lambda i: (i % nper, 0))
    if seq_len is None:
        widths = (nq, nkv, nkv, nqi, LANES, LANES)
        out_specs = [row(n) for n in widths]
        out_shape = [jax.ShapeDtypeStruct((m, n), F32) for n in widths]
    else:
        n = m // seq_len
        lm = lambda nh, rows: pl.BlockSpec((1, nh, rows, tm), lambda i: (i // nper, 0, 0, i % nper))
        lm3 = lambda rows: pl.BlockSpec((1, rows, tm), lambda i: (i // nper, 0, i % nper))
        nhq, nhk, nhi = nq // HEAD_DIM, nkv // HEAD_DIM, nqi // D_IDX
        out_specs = [lm(nhq, 2 * HEAD_DIM), row(nkv), lm(nhk, HEAD_DIM), lm(nhk, HEAD_DIM), lm(nhi, 2 * D_IDX),
                     row(LANES), lm3(D_IDX), lm3(H_IDX)]
        out_shape = [jax.ShapeDtypeStruct((n, nhq, 2 * HEAD_DIM, seq_len), BF16), jax.ShapeDtypeStruct((m, nkv), F32),
                     jax.ShapeDtypeStruct((n, nhk, HEAD_DIM, seq_len), F32),
                     jax.ShapeDtypeStruct((n, nhk, HEAD_DIM, seq_len), F32),
                     jax.ShapeDtypeStruct((n, nhi, 2 * D_IDX, seq_len), BF16), jax.ShapeDtypeStruct((m, LANES), F32),
                     jax.ShapeDtypeStruct((n, D_IDX, seq_len), F32), jax.ShapeDtypeStruct((n, H_IDX, seq_len), F32)]
    return pl.pallas_call(
        functools.partial(_proj_odd_kernel, nq=nq, nkv=nkv, nqi=nqi, wi_scale=wi_scale, lane_major=seq_len is not None),
        grid=(m // tm,),
        in_specs=[row(d), full(g), full(w), tab, tab],
        out_specs=out_specs, out_shape=out_shape,
        compiler_params=_cparams(("parallel",)),
        name="proj_odd",
    )(x, g, w, cos, sin)


def _conv_kernel(ag_ref, st_ref, wdw_ref, bdw_ref, lng_ref, lnb_ref, out_ref, nst_ref, buf_ref, sh_ref, *, tt, c_a, rc):
    t = pl.program_id(1)
    lo = HIST_PAD - HIST

    @pl.when(t == 0)
    def _():
        buf_ref[lo:HIST_PAD, :] = st_ref[0]

    @pl.when(t > 0)
    def _():
        buf_ref[lo:HIST_PAD, :] = buf_ref[tt + lo:tt + HIST_PAD, :]

    ag = ag_ref[0]
    buf_ref[HIST_PAD:HIST_PAD + tt, :] = ag[:, :c_a] * jax.nn.sigmoid(ag[:, c_a:])

    rows = tt + HIST_PAD
    for ph in range(1, SUBLANES):
        sh_ref[ph - 1, 0:rows - ph, :] = buf_ref[ph:rows, :]

    bdw = bdw_ref[...]
    lng = lng_ref[...]
    lnb = lnb_ref[...]
    for r in range(tt // rc):
        acc = jnp.zeros((rc, c_a), F32) + bdw
        for j in range(CONV_W):
            s = r * rc + lo + j
            ph, base = s % SUBLANES, s - s % SUBLANES
            win = buf_ref[base:base + rc, :] if ph == 0 else sh_ref[ph - 1, base:base + rc, :]
            acc = acc + wdw_ref[j:j + 1, :] * win
        mu = jnp.mean(acc, axis=-1, keepdims=True)
        cen = acc - mu
        var = jnp.mean(cen * cen, axis=-1, keepdims=True)
        cn = cen * lax.rsqrt(var + LN_EPS) * lng + lnb
        out_ref[0, r * rc:(r + 1) * rc, :] = cn * jax.nn.sigmoid(cn)

    @pl.when(t == pl.num_programs(1) - 1)
    def _():
        nst_ref[0] = buf_ref[tt + lo:tt + HIST_PAD, :]


def _conv_module(ag, state, wdw, bdw, lng, lnb, tt):
    n, t, two_ca = ag.shape
    c_a = two_ca // 2
    rc = min(tt, 64)
    vec = lambda a: pl.BlockSpec(a.shape, lambda i, j: (0, 0))
    return pl.pallas_call(
        functools.partial(_conv_kernel, tt=tt, c_a=c_a, rc=rc),
        grid=(n, t // tt),
        in_specs=[pl.BlockSpec((1, tt, two_ca), lambda i, j: (i, j, 0)),
                  pl.BlockSpec((1, HIST, c_a), lambda i, j: (i, 0, 0)),
                  vec(wdw), vec(bdw), vec(lng), vec(lnb)],
        out_specs=[pl.BlockSpec((1, tt, c_a), lambda i, j: (i, j, 0)),
                   pl.BlockSpec((1, HIST, c_a), lambda i, j: (i, 0, 0))],
        out_shape=[jax.ShapeDtypeStruct((n, t, c_a), F32), jax.ShapeDtypeStruct((n, HIST, c_a), F32)],
        scratch_shapes=[pltpu.VMEM((HIST_PAD + tt, c_a), F32),
                        pltpu.VMEM((SUBLANES - 1, HIST_PAD + tt, c_a), F32)],
        compiler_params=_cparams(("parallel", "arbitrary")),
        name="conv_module",
    )(ag, state, wdw, bdw, lng, lnb)


def _lane_cumsum(x):
    lane = lax.broadcasted_iota(jnp.int32, x.shape, 1)
    s = 1
    while s < LANES:
        x = x + jnp.where(lane >= s, pltpu.roll(x, s, 1), 0.0)
        s *= 2
    return x


def _page_cumsum_kernel(x_ref, o_ref):
    nb, h, _ = x_ref.shape
    o_ref[...] = _lane_cumsum(x_ref[...].reshape(nb * h, LANES)).reshape(nb, h, LANES)


def _page_cumsum(x, nb):
    b, h, _ = x.shape
    spec = pl.BlockSpec((nb, h, LANES), lambda i: (i, 0, 0))
    return pl.pallas_call(
        _page_cumsum_kernel,
        grid=(b // nb,),
        in_specs=[spec], out_specs=spec,
        out_shape=jax.ShapeDtypeStruct(x.shape, F32),
        compiler_params=_cparams(("parallel",)),
        name="page_cumsum",
    )(x)


def _seq_cumsum_kernel(x_ref, o_ref):
    t = x_ref.shape[3]
    nb = t // LANES
    x = x_ref[0, 0]
    loc = _lane_cumsum(jnp.concatenate([x[:, b * LANES:(b + 1) * LANES] for b in range(nb)], axis=0))
    off = jnp.zeros((1, LANES), F32)
    for b in range(nb):
        cur = loc[b:b + 1] + off
        o_ref[0, 0, b * LANES:(b + 1) * LANES, :] = jnp.broadcast_to(cur * LOG2E, (LANES, LANES)).T
        off = jnp.broadcast_to(cur[:, LANES - 1:LANES], (1, LANES))


def _seq_cumsum_columns(x):
    n, h, t = x.shape
    return pl.pallas_call(
        _seq_cumsum_kernel,
        grid=(n, h),
        in_specs=[pl.BlockSpec((1, 1, 1, t), lambda i, j: (i, j, 0, 0))],
        out_specs=pl.BlockSpec((1, 1, t, LANES), lambda i, j: (i, j, 0, 0)),
        out_shape=jax.ShapeDtypeStruct((n, h, t, LANES), F32),
        compiler_params=_cparams(("parallel", "parallel")),
        name="seq_cumsum",
    )(x.reshape(n, h, 1, t))


DENOM_ROWS = 16


def _flash_step_t(s_t, v_t, m_ref, acc_ref, idx):
    m_old = m_ref[idx]
    m_new = jnp.maximum(m_old, jnp.max(s_t, axis=0, keepdims=True))
    alpha = jnp.exp2(m_old - m_new)
    p = jnp.exp2(s_t - m_new).astype(BF16)
    kc = v_t.shape[1]
    ones_row = jnp.where(lax.broadcasted_iota(jnp.int32, (DENOM_ROWS, kc), 0) == 0, 1.0, 0.0).astype(BF16)
    v_aug = jnp.concatenate([v_t.astype(BF16), ones_row], axis=0)
    acc_ref[idx] = acc_ref[idx] * alpha + jnp.dot(v_aug, p, preferred_element_type=F32)
    m_ref[idx] = m_new


def _fox_prompt_kernel(qt_ref, k_ref, vt_ref, c_ref, o_ref, m_ref, acc_ref, sa_ref, sb_ref, *, tq, kc):
    i = pl.program_id(2)
    m_ref[...] = jnp.full(m_ref.shape, MASK_VALUE, F32)
    acc_ref[...] = jnp.zeros(acc_ref.shape, F32)
    q_pos = i * tq + lax.broadcasted_iota(jnp.int32, (kc // 2, tq), 1)
    k_iota = lax.broadcasted_iota(jnp.int32, (kc // 2, tq), 0)

    sub = kc // 2

    def logits(ch, half, s_ref):
        start = pl.multiple_of(ch * kc + half * sub, sub)
        k = k_ref[0, pl.ds(start, sub), :].astype(BF16)
        for hh in range(2):
            s_ref[hh] = jnp.dot(k, qt_ref[0, hh], preferred_element_type=F32)

    def softmax_pv(ch, half, s_ref, masked):
        start = pl.multiple_of(ch * kc + half * sub, sub)
        for hh in range(2):
            s = s_ref[hh] - jnp.tile(c_ref[0, hh, pl.ds(start, sub), :], (1, tq // LANES))
            if masked:
                s = jnp.where(start + k_iota <= q_pos, s, MASK_VALUE)
            _flash_step_t(s, vt_ref[0, hh, :, pl.ds(start, sub)], m_ref, acc_ref, hh)

    def run(lo, hi, masked):
        def body(ch, _):
            logits(ch, 1, sb_ref)
            softmax_pv(ch, 0, sa_ref, masked)
            logits(jnp.minimum(ch + 1, hi - 1), 0, sa_ref)
            softmax_pv(ch, 1, sb_ref, masked)
            return 0

        @pl.when(lo < hi)
        def _():
            logits(lo, 0, sa_ref)
            lax.fori_loop(lo, hi, body, 0)

    n_full = (i * tq) // kc
    run(0, n_full, False)
    run(n_full, n_full + tq // kc, True)
    out_t = jnp.concatenate([acc_ref[hh, :HEAD_DIM, :] / acc_ref[hh, HEAD_DIM:HEAD_DIM + 1, :] for hh in range(2)],
                            axis=0)
    for b in range(tq // LANES):
        o_ref[0, b * LANES:(b + 1) * LANES, :] = out_t[:, b * LANES:(b + 1) * LANES].T


def _fox_prompt(qt, k, vt, c_b, tq, kc):
    n, t, cb = k.shape
    hp = cb // LANES
    vrows = vt.shape[2]
    arows = vrows + DENOM_ROWS
    return pl.pallas_call(
        functools.partial(_fox_prompt_kernel, tq=tq, kc=kc),
        grid=(n, hp, t // tq),
        in_specs=[pl.BlockSpec((1, 2, LANES, tq), lambda b, h, i: (b, h, 0, i)),
                  pl.BlockSpec((1, t, LANES), lambda b, h, i: (b, 0, h)),
                  pl.BlockSpec((1, 2, vrows, t), lambda b, h, i: (b, h, 0, 0)),
                  pl.BlockSpec((1, 2, t, LANES), lambda b, h, i: (b, h, 0, 0))],
        out_specs=pl.BlockSpec((1, tq, LANES), lambda b, h, i: (b, i, h)),
        out_shape=jax.ShapeDtypeStruct((n, t, cb), F32),
        scratch_shapes=[pltpu.VMEM((2, 1, tq), F32), pltpu.VMEM((2, arows, tq), F32),
                        pltpu.VMEM((2, kc // 2, tq), F32), pltpu.VMEM((2, kc // 2, tq), F32)],
        compiler_params=_cparams(("parallel", "parallel", "arbitrary")),
        name="fox_prompt",
    )(qt, k, vt, c_b)


def _init_softmax(m_ref, l_ref, acc_ref):
    m_ref[...] = jnp.full(m_ref.shape, MASK_VALUE, F32)
    l_ref[...] = jnp.zeros(l_ref.shape, F32)
    acc_ref[...] = jnp.zeros(acc_ref.shape, F32)


def _softmax_update(ss, vts, m_ref, l_ref, acc_ref):
    m_old = m_ref[...]
    smax = ss[0]
    for s in ss[1:]:
        smax = jnp.maximum(smax, s)
    m_new = jnp.maximum(m_old, jnp.max(smax, axis=-1, keepdims=True))
    alpha = jnp.exp(m_old - m_new)
    acc = alpha * acc_ref[...]
    psum = None
    for s, vt in zip(ss, vts):
        p = jnp.exp(s - m_new)
        psum = p if psum is None else psum + p
        acc = acc + lax.dot_general(p.astype(BF16), vt, _NT, preferred_element_type=F32)
    l_new = alpha * l_ref[...] + jnp.sum(psum, axis=-1, keepdims=True)
    m_ref[...] = m_new
    l_ref[...] = l_new
    acc_ref[...] = acc


def _paged_specs(pp, n_steps, block, index_tail):
    def spec(p):
        return pl.BlockSpec(block, lambda b, j, pt: index_tail(pt[b, jnp.minimum(j, n_steps - 1) * pp + p]))
    return [spec(p) for p in range(pp)]


def _fox_sample_kernel(pt_ref, wq_ref, *refs, pp, n_steps, tq, nh):
    del pt_ref
    kp, vp, cp = refs[:pp], refs[pp:2 * pp], refs[2 * pp:3 * pp]
    kn_ref, vn_ref, cn_ref, o_ref, m_ref, l_ref, acc_ref, off_ref = refs[3 * pp:]
    j = pl.program_id(1)
    rows, feat = acc_ref.shape
    page = off_ref.shape[1]

    @pl.when(j == 0)
    def _():
        _init_softmax(m_ref, l_ref, acc_ref)
        off_ref[...] = jnp.zeros(off_ref.shape, F32)

    def logits(kt, c):
        s = jnp.dot(wq_ref[0], kt.reshape(feat, page).astype(BF16), preferred_element_type=F32)
        return s - jnp.broadcast_to(c[:, None, :], (nh, tq, page)).reshape(rows, page)

    @pl.when(j < n_steps)
    def _():
        off = off_ref[...]
        ss, vs = [], []
        for p in range(pp):
            cloc = cp[p][0]
            ss.append(logits(kp[p][0, 0], cloc + off))
            vs.append(vp[p][0, 0].reshape(feat, page).astype(BF16))
            off = off + jnp.broadcast_to(cloc[:, page - 1:page], off.shape)
        off_ref[...] = off
        _softmax_update(ss, vs, m_ref, l_ref, acc_ref)

    @pl.when(j == n_steps)
    def _():
        s = logits(kn_ref[0], cn_ref[0] + off_ref[...])
        qi = jnp.bitwise_and(lax.broadcasted_iota(jnp.int32, (rows, page), 0), tq - 1)
        ki = lax.broadcasted_iota(jnp.int32, (rows, page), 1)
        s = jnp.where(ki <= qi, s, MASK_VALUE)
        _softmax_update([s], [vn_ref[0].astype(BF16)], m_ref, l_ref, acc_ref)
        out = acc_ref[...] / l_ref[...]
        for h in range(nh):
            o_ref[0, :, h * HEAD_DIM:(h + 1) * HEAD_DIM] = out[h * tq:(h + 1) * tq, h * HEAD_DIM:(h + 1) * HEAD_DIM]


def _fox_sample(page_table, wq, kcache_t, vcache_t, cpool, knew_t, vnew_t, cnew, tq, pp):
    nb, n_pages = page_table.shape
    _, _, nh, hd, page = kcache_t.shape
    feat = nh * hd
    n_steps = n_pages // pp
    rows = wq.shape[1]
    seq3 = lambda b, j, pt: (b, 0, 0)
    kv_specs = lambda: _paged_specs(pp, n_steps, (1, 1, nh, hd, page), lambda pg: (0, pg, 0, 0, 0))
    grid_spec = pltpu.PrefetchScalarGridSpec(
        num_scalar_prefetch=1,
        grid=(nb, n_steps + 1),
        in_specs=[pl.BlockSpec((1, rows, feat), seq3)] + kv_specs() + kv_specs()
                 + _paged_specs(pp, n_steps, (1, nh, page), lambda pg: (pg, 0, 0))
                 + [pl.BlockSpec((1, feat, page), seq3), pl.BlockSpec((1, feat, page), seq3),
                    pl.BlockSpec((1, nh, page), seq3)],
        out_specs=pl.BlockSpec((1, tq, feat), seq3),
        scratch_shapes=[pltpu.VMEM((rows, 1), F32), pltpu.VMEM((rows, 1), F32), pltpu.VMEM((rows, feat), F32),
                        pltpu.VMEM((nh, page), F32)],
    )
    return pl.pallas_call(
        functools.partial(_fox_sample_kernel, pp=pp, n_steps=n_steps, tq=tq, nh=nh),
        grid_spec=grid_spec,
        out_shape=jax.ShapeDtypeStruct((nb, tq, feat), F32),
        compiler_params=_cparams(("parallel", "arbitrary")),
        name="fox_sample",
    )(page_table, wq, *([kcache_t] * pp), *([vcache_t] * pp), *([cpool] * pp), knew_t, vnew_t, cnew)


def _outproj_kernel(*refs, n_in):
    ins, w_ref, res_ref, o_ref = refs[:n_in], refs[n_in], refs[n_in + 1], refs[n_in + 2]
    acc = res_ref[...]
    off = 0
    for r in ins:
        width = r.shape[1]
        acc = acc + jnp.dot(r[...].astype(BF16), w_ref[off:off + width, :], preferred_element_type=F32)
        off += width
    o_ref[...] = acc


def _outproj(ins, w, res, tm):
    m, d = res.shape
    row = lambda n: pl.BlockSpec((tm, n), lambda i: (i, 0))
    return pl.pallas_call(
        functools.partial(_outproj_kernel, n_in=len(ins)),
        grid=(m // tm,),
        in_specs=[row(a.shape[1]) for a in ins] + [pl.BlockSpec(w.shape, lambda i: (0, 0)), row(d)],
        out_specs=row(d),
        out_shape=jax.ShapeDtypeStruct((m, d), F32),
        compiler_params=_cparams(("parallel",)),
        name="outproj",
    )(*ins, w, res)


def _route(z):
    lane = lax.broadcasted_iota(jnp.int32, z.shape, 1).astype(F32)
    big = jnp.float32(1 << 20)
    neg = -jnp.inf
    rmax = lambda a: jnp.max(a, axis=-1, keepdims=True)
    rmin = lambda a: jnp.min(a, axis=-1, keepdims=True)
    is_g = lane < N_GROUPS
    zg = jnp.where(is_g, z, neg)
    gmax = rmax(zg)
    g_idx = rmin(jnp.where(zg == gmax, lane, big))
    g_w = 1.0 / jnp.sum(jnp.where(is_g, jnp.exp(zg - gmax), 0.0), axis=-1, keepdims=True)
    first = N_GROUPS + g_idx * E_PER_GROUP
    in_grp = jnp.logical_and(lane >= first, lane < first + E_PER_GROUP)
    v1 = jnp.where(in_grp, z, neg)
    top1 = rmax(v1)
    i1 = rmin(jnp.where(v1 == top1, lane, big))
    v2 = jnp.where(lane == i1, neg, v1)
    top2 = rmax(v2)
    i2 = rmin(jnp.where(v2 == top2, lane, big))
    e2 = jnp.exp(top2 - top1)
    den = 1.0 + e2
    return jnp.where(lane == i1, g_w / den, jnp.where(lane == i2, g_w * e2 / den, 0.0)), g_idx.astype(jnp.int32)


def _moe_kernel(x_ref, g_ref, wr_ref, br_ref, wg_ref, wu_ref, wd_ref, gf_ref, o_ref, xn_ref, cmb_ref, acc_ref,
                *, final_norm):
    e = pl.program_id(1)

    @pl.when(e == 0)
    def _():
        xn = _rms(x_ref[...], g_ref[...])
        xn_ref[...] = xn.astype(BF16)
        z = jnp.dot(xn, wr_ref[...], preferred_element_type=F32, precision=lax.Precision.HIGHEST) + br_ref[...]
        cmb_ref[...] = _route(z)[0]
        acc_ref[...] = jnp.zeros(acc_ref.shape, F32)

    xn = xn_ref[...]
    gate = jnp.dot(xn, wg_ref[0], preferred_element_type=F32)
    up = jnp.dot(xn, wu_ref[0], preferred_element_type=F32)
    hid = (gate * jax.nn.sigmoid(gate) * up).astype(BF16)
    y = jnp.dot(hid, wd_ref[0], preferred_element_type=F32)
    lane = lax.broadcasted_iota(jnp.int32, cmb_ref.shape, 1)
    col = jnp.sum(jnp.where(lane == e + N_GROUPS, cmb_ref[...], 0.0), axis=-1, keepdims=True)
    acc_ref[...] = acc_ref[...] + col * y

    @pl.when(e == pl.num_programs(1) - 1)
    def _():
        out = x_ref[...] + acc_ref[...]
        o_ref[...] = _rms(out, gf_ref[...]) if final_norm else out


def _moe(x, g, wr, br, wg, wu, wd, gf, tm, final_norm):
    m, d = x.shape
    ne, _, dff = wg.shape
    row = pl.BlockSpec((tm, d), lambda i, e: (i, 0))
    full = lambda a: pl.BlockSpec(a.shape, lambda i, e: (0, 0))
    return pl.pallas_call(
        functools.partial(_moe_kernel, final_norm=final_norm),
        grid=(m // tm, ne),
        in_specs=[row, full(g), full(wr), full(br),
                  pl.BlockSpec((1, d, dff), lambda i, e: (e, 0, 0)),
                  pl.BlockSpec((1, d, dff), lambda i, e: (e, 0, 0)),
                  pl.BlockSpec((1, dff, d), lambda i, e: (e, 0, 0)),
                  full(gf)],
        out_specs=row,
        out_shape=jax.ShapeDtypeStruct((m, d), F32),
        scratch_shapes=[pltpu.VMEM((tm, d), BF16), pltpu.VMEM((tm, LANES), F32), pltpu.VMEM((tm, d), F32)],
        compiler_params=_cparams(("parallel", "arbitrary")),
        name="moe",
    )(x, g, wr, br, wg, wu, wd, gf)


MOE_WINDOW = 320


def _split3(a):
    hi = a.astype(BF16)
    r1 = a - hi.astype(F32)
    mid = r1.astype(BF16)
    return hi, mid, (r1 - mid.astype(F32)).astype(BF16)


def _moe_sorted_kernel(x_ref, g_ref, wr_ref, br_ref, wg_ref, wu_ref, wd_ref, gf_ref, o_ref,
                       xs_ref, cmbs_ref, acc_ref, pt_ref, rng_ref, *, final_norm, win):
    e = pl.program_id(1)
    tm, d = x_ref.shape

    @pl.when(e == 0)
    def _():
        xn = _rms(x_ref[...], g_ref[...])
        z = jnp.dot(xn, wr_ref[...], preferred_element_type=F32, precision=lax.Precision.HIGHEST) + br_ref[...]
        cmb, g_idx = _route(z)
        lane = lax.broadcasted_iota(jnp.int32, (tm, LANES), 1)
        onehot = jnp.where(lane == g_idx, 1.0, 0.0)
        r_i = lax.broadcasted_iota(jnp.int32, (tm, tm), 0)
        c_i = lax.broadcasted_iota(jnp.int32, (tm, tm), 1)
        rows = lax.broadcasted_iota(jnp.int32, (tm, LANES), 0)
        before = onehot
        step = 1
        while step < tm:
            before = before + jnp.where(rows >= step, pltpu.roll(before, step, 0), 0.0)
            step *= 2
        before = before - onehot
        counts = jnp.sum(onehot, axis=0, keepdims=True)
        starts = _lane_cumsum(counts) - counts
        pos = jnp.sum(jnp.where(lane == g_idx, before + starts, 0.0), axis=-1, keepdims=True)
        pos_row = jnp.broadcast_to(pos, (tm, LANES)).T[0:1].astype(jnp.int32)
        perm = jnp.where(r_i == pos_row, 1.0, 0.0).astype(BF16)
        pt_ref[...] = jnp.where(c_i == pos.astype(jnp.int32), 1.0, 0.0).astype(BF16)
        xs_ref[...] = jnp.dot(perm, xn.astype(BF16), preferred_element_type=F32).astype(BF16)
        cmbs_ref[...] = sum(jnp.dot(perm, t, preferred_element_type=F32) for t in _split3(cmb))
        acc_ref[...] = jnp.zeros(acc_ref.shape, F32)
        lane1 = lax.broadcasted_iota(jnp.int32, (1, LANES), 1)
        for grp in range(N_GROUPS):
            rng_ref[grp] = jnp.sum(jnp.where(lane1 == grp, starts, 0.0)).astype(jnp.int32)
            rng_ref[N_GROUPS + grp] = jnp.sum(jnp.where(lane1 == grp, counts, 0.0)).astype(jnp.int32)

    grp = e // E_PER_GROUP
    start = rng_ref[grp]
    end = start + rng_ref[N_GROUPS + grp]
    w0 = (start // 16) * 16
    lane = lax.broadcasted_iota(jnp.int32, (win, LANES), 1)
    row = lax.broadcasted_iota(jnp.int32, (win, 1), 0)

    def window(i, _):
        lo = w0 + i * win
        ws = pl.multiple_of(jnp.minimum(lo, tm - win), 16)
        xc = xs_ref[pl.ds(ws, win), :]
        gate = jnp.dot(xc, wg_ref[0], preferred_element_type=F32)
        up = jnp.dot(xc, wu_ref[0], preferred_element_type=F32)
        hid = (gate * jax.nn.sigmoid(gate) * up).astype(BF16)
        y = jnp.dot(hid, wd_ref[0], preferred_element_type=F32)
        col = jnp.sum(jnp.where(lane == e + N_GROUPS, cmbs_ref[pl.ds(ws, win), :], 0.0), axis=-1, keepdims=True)
        srt = ws + row
        mine = jnp.logical_and(srt >= jnp.maximum(lo, start), srt < jnp.minimum(lo + win, end))
        acc_ref[pl.ds(ws, win), :] = acc_ref[pl.ds(ws, win), :] + jnp.where(mine, col, 0.0) * y
        return 0

    lax.fori_loop(0, (end - w0 + win - 1) // win, window, 0)

    @pl.when(e == pl.num_programs(1) - 1)
    def _():
        acc = acc_ref[...]
        hi = acc.astype(BF16)
        lo = (acc - hi.astype(F32)).astype(BF16)
        moe = jnp.dot(pt_ref[...], hi, preferred_element_type=F32) + jnp.dot(pt_ref[...], lo, preferred_element_type=F32)
        out = x_ref[...] + moe
        o_ref[...] = _rms(out, gf_ref[...]) if final_norm else out


def _moe_sorted(x, g, wr, br, wg, wu, wd, gf, tm, final_norm):
    m, d = x.shape
    ne, _, dff = wg.shape
    row = pl.BlockSpec((tm, d), lambda i, e: (i, 0))
    full = lambda a: pl.BlockSpec(a.shape, lambda i, e: (0, 0))
    return pl.pallas_call(
        functools.partial(_moe_sorted_kernel, final_norm=final_norm, win=MOE_WINDOW),
        grid=(m // tm, ne),
        in_specs=[row, full(g), full(wr), full(br),
                  pl.BlockSpec((1, d, dff), lambda i, e: (e, 0, 0)),
                  pl.BlockSpec((1, d, dff), lambda i, e: (e, 0, 0)),
                  pl.BlockSpec((1, dff, d), lambda i, e: (e, 0, 0)),
                  full(gf)],
        out_specs=row,
        out_shape=jax.ShapeDtypeStruct((m, d), F32),
        scratch_shapes=[pltpu.VMEM((tm, d), BF16), pltpu.VMEM((tm, LANES), F32), pltpu.VMEM((tm, d), F32),
                        pltpu.VMEM((tm, tm), BF16), pltpu.SMEM((2 * N_GROUPS,), jnp.int32)],
        compiler_params=_cparams(("parallel", "arbitrary")),
        name="moe_sorted",
    )(x, g, wr, br, wg, wu, wd, gf)


def _select_topk(sc_ref, nc, n_sel, n_valid, key_axis):
    _, d0, d1 = sc_ref.shape
    static = isinstance(nc, int)
    chunk_keys = (d0, d1)[key_axis]
    kf = jnp.float32(n_sel)
    inf = jnp.inf
    key_off = lax.broadcasted_iota(jnp.int32, (d0, d1), key_axis)
    stat_shape = (1, d1) if key_axis == 0 else (d0, 1)
    acc_rows = 4 * SUBLANES if d0 % (4 * SUBLANES) == 0 else SUBLANES
    reducers = {"sum": (jnp.add, jnp.sum, 0.0), "min": (jnp.minimum, jnp.min, inf), "max": (jnp.maximum, jnp.max, -inf)}

    def fold(fn, kind):
        comb, red, init = reducers[kind]

        def narrow(a):
            if key_axis == 0:
                return red(a.reshape(d0 // acc_rows, acc_rows, d1), axis=0)
            out = a[:, :LANES]
            for b in range(1, d1 // LANES):
                out = comb(out, a[:, b * LANES:(b + 1) * LANES])
            return out

        if static:
            ch = lax.broadcasted_iota(jnp.int32, (nc, d0, d1), 0)
            part = narrow(red(fn(sc_ref[0:nc], ch), axis=0))
        else:
            part = lax.fori_loop(0, nc, lambda ch, a: comb(a, narrow(fn(sc_ref[ch], ch))),
                                 jnp.full((acc_rows, d1) if key_axis == 0 else (d0, LANES), init, F32))
        return red(part, axis=key_axis, keepdims=True)

    def count(pred):
        return fold(lambda x, ch: jnp.where(pred(x), 1.0, 0.0), "sum")

    def any_query(flag):
        return jnp.max(jnp.where(flag, 1.0, 0.0)) > 0.5

    mx = fold(lambda x, ch: x, "max")
    mn = fold(lambda x, ch: jnp.where(x > -inf, x, inf), "min")
    take_all = n_valid < kf

    def bisect(_, lh):
        lo, hi = lh
        mid = 0.5 * (lo + hi)
        ge = count(lambda x: x >= mid) >= kf
        return jnp.where(ge, mid, lo), jnp.where(ge, hi, mid)

    lo, _ = lax.fori_loop(0, BISECT_ITERS, bisect, (mn, mx))

    cand = fold(lambda x, ch: jnp.where(x >= lo, x, inf), "min")
    n_gt = count(lambda x: x > cand)

    def unsettled(state):
        return any_query(jnp.logical_and(state[1] >= kf, jnp.logical_not(take_all)))

    def walk(state):
        cd, g = state
        nxt = fold(lambda x, ch: jnp.where(x > cd, x, inf), "min")
        cd = jnp.where(g >= kf, nxt, cd)
        return cd, count(lambda x: x > cd)

    cand, n_gt = lax.while_loop(unsettled, walk, (cand, n_gt))
    thr = jnp.where(take_all, -inf, cand)
    n_gt = jnp.where(take_all, n_valid, n_gt)
    need = kf - n_gt
    n_eq = count(lambda x: x == thr)
    has_excess = any_query(jnp.logical_and(n_eq > need, jnp.logical_not(take_all)))

    def tie_cut():
        def bit_step(b, ans):
            cnd = ans + jnp.left_shift(jnp.int32(1), POS_BITS - 1 - b)
            in_front = lambda x, ch: jnp.where(x == thr, jnp.where((ch * chunk_keys + key_off) < cnd, 1.0, 0.0), 0.0)
            return jnp.where(fold(in_front, "sum") < need, cnd, ans)
        return lax.fori_loop(0, POS_BITS, bit_step, jnp.zeros(stat_shape, jnp.int32))

    cut = lax.cond(has_excess, tie_cut, lambda: jnp.full(stat_shape, (1 << POS_BITS) - 1, jnp.int32))

    def bias_of(x, ch):
        tie = jnp.where(x == thr, jnp.where((ch * chunk_keys + key_off) <= cut, 0.0, MASK_VALUE), MASK_VALUE)
        return jnp.where(x > -inf, jnp.where(x > thr, 0.0, tie), MASK_VALUE)

    if static:
        sc_ref[0:nc] = bias_of(sc_ref[0:nc], lax.broadcasted_iota(jnp.int32, (nc, d0, d1), 0))
    else:
        def write(ch, _):
            sc_ref[ch] = bias_of(sc_ref[ch], ch)
            return 0
        lax.fori_loop(0, nc, write, 0)


def _dsa_prompt_kernel(qit_ref, wit_ref, kidx_ref, qt_ref, k_ref, vt_ref, o_ref, sc_ref, m_ref, acc_ref, sa_ref, sb_ref,
                       *, tq, kc, n_sel, rep):
    i = pl.program_id(1)
    nc = ((i + 1) * tq + kc - 1) // kc
    nh = qt_ref.shape[1]
    nkv = vt_ref.shape[1]
    q_pos = i * tq + lax.broadcasted_iota(jnp.int32, (kc, tq), 1)
    k_iota = lax.broadcasted_iota(jnp.int32, (kc, tq), 0)
    qi_all = jnp.concatenate([qit_ref[0, h] for h in range(H_IDX)], axis=1)
    wi_all = jnp.concatenate([wit_ref[0, h:h + 1, :] for h in range(H_IDX)], axis=1)

    def score_chunk(ch, _):
        start = pl.multiple_of(ch * kc, kc)
        d = jnp.dot(kidx_ref[0, pl.ds(start, kc), :].astype(BF16), qi_all, preferred_element_type=F32)
        r = jnp.maximum(d, 0.0) * wi_all
        acc = r[:, :tq]
        for h in range(1, H_IDX):
            acc = acc + r[:, h * tq:(h + 1) * tq]
        sc_ref[ch] = jnp.where(start + k_iota <= q_pos, acc, -jnp.inf)
        return 0

    lax.fori_loop(0, nc, score_chunk, 0)
    n_valid = (i * tq + 1 + lax.broadcasted_iota(jnp.int32, (1, tq), 1)).astype(F32)
    _select_topk(sc_ref, nc, n_sel, n_valid, 0)

    m_ref[...] = jnp.full(m_ref.shape, MASK_VALUE, F32)
    acc_ref[...] = jnp.zeros(acc_ref.shape, F32)

    sub = kc // 2
    q_groups = lambda g: jnp.concatenate([qt_ref[0, g * rep + r] for r in range(rep)], axis=1)

    def logits(ch, half, s_ref):
        start = pl.multiple_of(ch * kc + half * sub, sub)
        for g in range(nkv):
            kblk = k_ref[0, pl.ds(start, sub), (g // 2) * LANES:(g // 2 + 1) * LANES].astype(BF16)
            s_ref[g] = jnp.dot(kblk, q_groups(g), preferred_element_type=F32)

    def softmax_pv(ch, half, s_ref):
        start = pl.multiple_of(ch * kc + half * sub, sub)
        bias = jnp.tile(sc_ref[ch, half * sub:(half + 1) * sub, :], (1, rep))
        for g in range(nkv):
            _flash_step_t(s_ref[g] + bias, vt_ref[0, g, :, pl.ds(start, sub)], m_ref, acc_ref, g)

    logits(0, 0, sa_ref)

    def attend(ch, _):
        logits(ch, 1, sb_ref)
        softmax_pv(ch, 0, sa_ref)
        logits(jnp.minimum(ch + 1, nc - 1), 0, sa_ref)
        softmax_pv(ch, 1, sb_ref)
        return 0

    lax.fori_loop(0, nc, attend, 0)
    for h in range(0, nh, 2):
        pair = []
        for hh in (h, h + 1):
            g, r = hh // rep, hh % rep
            cols = slice(r * tq, (r + 1) * tq)
            pair.append(acc_ref[g, :HEAD_DIM, cols] / acc_ref[g, HEAD_DIM:HEAD_DIM + 1, cols])
        o_ref[0, :, h * HEAD_DIM:(h + 2) * HEAD_DIM] = jnp.concatenate(pair, axis=0).T


def _dsa_prompt(qit, wit, kidx, qt, k, vt, tq, kc, n_sel):
    n, nh, _, t = qt.shape
    nkv = vt.shape[1]
    rep = nh // nkv
    vrows = vt.shape[2]
    arows = vrows + DENOM_ROWS
    return pl.pallas_call(
        functools.partial(_dsa_prompt_kernel, tq=tq, kc=kc, n_sel=n_sel, rep=rep),
        grid=(n, t // tq),
        in_specs=[pl.BlockSpec((1, H_IDX, 2 * D_IDX, tq), lambda b, i: (b, 0, 0, i)),
                  pl.BlockSpec((1, H_IDX, tq), lambda b, i: (b, 0, i)),
                  pl.BlockSpec((1, t, 2 * D_IDX), lambda b, i: (b, 0, 0)),
                  pl.BlockSpec((1, nh, 2 * HEAD_DIM, tq), lambda b, i: (b, 0, 0, i)),
                  pl.BlockSpec((1, t, nkv * HEAD_DIM), lambda b, i: (b, 0, 0)),
                  pl.BlockSpec((1, nkv, vrows, t), lambda b, i: (b, 0, 0, 0))],
        out_specs=pl.BlockSpec((1, tq, nh * HEAD_DIM), lambda b, i: (b, i, 0)),
        out_shape=jax.ShapeDtypeStruct((n, t, nh * HEAD_DIM), F32),
        scratch_shapes=[pltpu.VMEM((t // kc, kc, tq), F32),
                        pltpu.VMEM((nkv, 1, rep * tq), F32), pltpu.VMEM((nkv, arows, rep * tq), F32),
                        pltpu.VMEM((nkv, kc // 2, rep * tq), F32), pltpu.VMEM((nkv, kc // 2, rep * tq), F32)],
        compiler_params=_cparams(("parallel", "arbitrary")),
        name="dsa_prompt",
    )(qit, wit, kidx, qt, k, vt)


def _dsa_sample_score_kernel(pt_ref, qi_ref, wi_ref, *refs, pp, n_steps, page, tq, n_sel):
    del pt_ref
    kp = refs[:pp]
    kn_ref, bp_ref, bn_ref, sc_ref = refs[pp:]
    j = pl.program_id(1)
    n_pages = n_steps * pp
    q_off = lax.broadcasted_iota(jnp.int32, (tq, page), 0)
    k_off = lax.broadcasted_iota(jnp.int32, (tq, page), 1)

    def score(kpage_t, pg, new):
        d = jnp.dot(qi_ref[0], kpage_t.astype(BF16), preferred_element_type=F32)
        r = (jnp.maximum(d, 0.0) * wi_ref[0]).reshape(H_IDX, tq, page)
        acc = r[0]
        for h in range(1, H_IDX):
            acc = acc + r[h]
        sc_ref[pg] = jnp.where(k_off <= q_off, acc, -jnp.inf) if new else acc

    @pl.when(j < n_steps)
    def _():
        for p in range(pp):
            score(kp[p][0, 0], j * pp + p, False)

    @pl.when(j == n_steps)
    def _():
        score(kn_ref[0], n_pages, True)
        n_valid = (n_pages * page + 1 + lax.broadcasted_iota(jnp.int32, (tq, 1), 0)).astype(F32)
        _select_topk(sc_ref, n_pages + 1, n_sel, n_valid, 1)
        bp_ref[0] = sc_ref[0:n_pages]
        bn_ref[0] = sc_ref[n_pages]


def _dsa_sample_scores(page_table, qi, wi, kidx_cache_t, kidx_new_t, tq, n_sel, pp):
    nb, n_pages = page_table.shape
    page = kidx_cache_t.shape[3]
    n_steps = n_pages // pp
    rows = qi.shape[1]
    seq3 = lambda b, j, pt: (b, 0, 0)
    grid_spec = pltpu.PrefetchScalarGridSpec(
        num_scalar_prefetch=1,
        grid=(nb, n_steps + 1),
        in_specs=[pl.BlockSpec((1, rows, D_IDX), seq3), pl.BlockSpec((1, rows, 1), seq3)]
                 + _paged_specs(pp, n_steps, (1, 1, D_IDX, page), lambda pg: (0, pg, 0, 0))
                 + [pl.BlockSpec((1, D_IDX, page), seq3)],
        out_specs=[pl.BlockSpec((1, n_pages, tq, page), lambda b, j, pt: (b, 0, 0, 0)),
                   pl.BlockSpec((1, tq, page), seq3)],
        scratch_shapes=[pltpu.VMEM((n_pages + 1, tq, page), F32)],
    )
    return pl.pallas_call(
        functools.partial(_dsa_sample_score_kernel, pp=pp, n_steps=n_steps, page=page, tq=tq, n_sel=n_sel),
        grid_spec=grid_spec,
        out_shape=[jax.ShapeDtypeStruct((nb, n_pages, tq, page), F32), jax.ShapeDtypeStruct((nb, tq, page), F32)],
        compiler_params=_cparams(("parallel", "arbitrary")),
        name="dsa_sample_scores",
    )(page_table, qi, wi, *([kidx_cache_t] * pp), kidx_new_t)


def _dsa_sample_attn_kernel(pt_ref, wq_ref, *refs, pp, n_steps, tq, nh, rep):
    del pt_ref
    kp, vp = refs[:pp], refs[pp:2 * pp]
    kn_ref, vn_ref, bp_ref, bn_ref, o_ref, m_ref, l_ref, acc_ref = refs[2 * pp:]
    j = pl.program_id(1)

    @pl.when(j == 0)
    def _():
        _init_softmax(m_ref, l_ref, acc_ref)

    ckv = acc_ref.shape[1]

    def logits(kt, bias):
        s = jnp.dot(wq_ref[0], kt.reshape(ckv, -1).astype(BF16), preferred_element_type=F32)
        return s + jnp.tile(bias, (nh, 1))

    @pl.when(j < n_steps)
    def _():
        ss = [logits(kp[p][0, 0], bp_ref[0, p]) for p in range(pp)]
        vts = [vp[p][0, 0].reshape(ckv, -1).astype(BF16) for p in range(pp)]
        _softmax_update(ss, vts, m_ref, l_ref, acc_ref)

    @pl.when(j == n_steps)
    def _():
        _softmax_update([logits(kn_ref[0], bn_ref[0])], [vn_ref[0].astype(BF16)], m_ref, l_ref, acc_ref)
        out = acc_ref[...] / l_ref[...]
        for h in range(nh):
            g = h // rep
            o_ref[0, :, h * HEAD_DIM:(h + 1) * HEAD_DIM] = out[h * tq:(h + 1) * tq, g * HEAD_DIM:(g + 1) * HEAD_DIM]


def _dsa_sample_attn(page_table, wq, kcache_t, vcache_t, knew_t, vnew_t, bias_pages, bias_new, tq, nh, pp):
    nb, n_pages = page_table.shape
    _, _, nkv, hd, page = kcache_t.shape
    ckv = nkv * hd
    n_steps = n_pages // pp
    rows = wq.shape[1]
    seq3 = lambda b, j, pt: (b, 0, 0)
    kv_specs = lambda: _paged_specs(pp, n_steps, (1, 1, nkv, hd, page), lambda pg: (0, pg, 0, 0, 0))
    grid_spec = pltpu.PrefetchScalarGridSpec(
        num_scalar_prefetch=1,
        grid=(nb, n_steps + 1),
        in_specs=[pl.BlockSpec((1, rows, ckv), seq3)] + kv_specs() + kv_specs()
                 + [pl.BlockSpec((1, ckv, page), seq3), pl.BlockSpec((1, ckv, page), seq3),
                    pl.BlockSpec((1, pp, tq, page), lambda b, j, pt: (b, jnp.minimum(j, n_steps - 1), 0, 0)),
                    pl.BlockSpec((1, tq, page), seq3)],
        out_specs=pl.BlockSpec((1, tq, nh * HEAD_DIM), seq3),
        scratch_shapes=[pltpu.VMEM((rows, 1), F32), pltpu.VMEM((rows, 1), F32), pltpu.VMEM((rows, ckv), F32)],
    )
    return pl.pallas_call(
        functools.partial(_dsa_sample_attn_kernel, pp=pp, n_steps=n_steps, tq=tq, nh=nh, rep=nh * HEAD_DIM // ckv),
        grid_spec=grid_spec,
        out_shape=jax.ShapeDtypeStruct((nb, tq, nh * HEAD_DIM), F32),
        compiler_params=_cparams(("parallel", "arbitrary")),
        name="dsa_sample_attn",
    )(page_table, wq, *([kcache_t] * pp), *([vcache_t] * pp), knew_t, vnew_t, bias_pages, bias_new)


def _pad_cols(a, n):
    return jnp.pad(a, [(0, 0)] * (a.ndim - 1) + [(0, n - a.shape[-1])])


def _rope_tables(pos):
    half = HEAD_DIM // 2
    inv = ROPE_THETA ** (-jnp.arange(half, dtype=F32) / half)
    ang = pos.astype(F32)[:, None] * inv[None, :]
    cos, sin = jnp.cos(ang), jnp.sin(ang)
    return jnp.tile(cos, (1, LANES // half)), jnp.tile(jnp.concatenate([-sin, sin], axis=1), (1, LANES // HEAD_DIM))


def _block_diag_queries(q, n_heads, n_kv):
    b, t, _ = q.shape
    qh = q.reshape(b, t, n_heads, HEAD_DIM).transpose(0, 2, 1, 3)
    onehot = jax.nn.one_hot(jnp.arange(n_heads) // (n_heads // n_kv), n_kv, dtype=q.dtype)
    w = qh[:, :, :, None, :] * onehot[None, :, None, :, None]
    return w.reshape(b, n_heads * t, n_kv * HEAD_DIM).astype(BF16)


def _tile(m, pref):
    for t in pref:
        if m % t == 0:
            return t
    return m


def kernel(x_prompt, x_sample, state_conv, cache_fox_k, cache_fox_v, cache_fox_logf, cache_dsa_k, cache_dsa_v,
           cache_dsa_kidx, page_table, norm_mix_even, w_in_even, b_forget, w_dw, b_dw, ln_conv_g, ln_conv_b,
           w_out_even, norm_mix_odd, w_in_odd, w_out_odd, norm_ffn, w_group, b_group, w_router, b_router, w_gate,
           w_up, w_down, norm_final):
    n_p, t_p, d = x_prompt.shape
    n_s, t_s, _ = x_sample.shape
    depth = norm_ffn.shape[0]
    page = cache_fox_k.shape[2]
    n_pages = page_table.shape[1]
    past = n_pages * page
    c_a = w_dw.shape[2]
    h_b = b_forget.shape[1]
    c_b = h_b * HEAD_DIM
    kv_c = cache_dsa_k.shape[3]
    h_c = w_out_odd.shape[1] // HEAD_DIM
    nq, nkv, nqi = h_c * HEAD_DIM, kv_c * HEAD_DIM, H_IDX * D_IDX
    assert t_s == SUBLANES and depth == 2 and w_in_even.shape[0] == 1 and w_in_odd.shape[0] == 1

    mp, ms = n_p * t_p, n_s * t_s
    hp = x_prompt.reshape(mp, d)
    hs = x_sample.reshape(ms, d)
    tm_p = _tile(mp, (512, 256, 128))
    row = lambda a: a.reshape(1, -1)

    def moe_layer(h, layer, tm, final):
        wr = _pad_cols(jnp.concatenate([w_group[layer], w_router[layer]], axis=1), LANES)
        br = _pad_cols(jnp.concatenate([b_group[layer], b_router[layer]]).reshape(1, -1), LANES)
        sort_tokens = tm >= 2 * MOE_WINDOW and (tm - MOE_WINDOW) % 16 == 0
        return (_moe_sorted if sort_tokens else _moe)(
            h, row(norm_ffn[layer]), wr, br, w_gate[layer].astype(BF16), w_up[layer].astype(BF16),
            w_down[layer].astype(BF16), row(norm_final), tm, final)

    tm_seq = _tile(t_p, (512, 256, 128))
    w_e = w_in_even[0]
    w_e = jnp.concatenate([w_e[:, :2 * c_a + 3 * c_b], _pad_cols(w_e[:, 2 * c_a + 3 * c_b:], LANES)], axis=1).astype(BF16)
    bf = _pad_cols(b_forget[0].reshape(1, -1), LANES)
    g_e = row(norm_mix_even[0])
    ag_p, k_p, qt_p, kt_p, vt_p, lft_p = _proj_even(hp, g_e, w_e, bf, c_a, c_b, tm_seq, t_p)
    ag_s, q_s, k_s, v_s, lf_s = _proj_even(hs, g_e, w_e, bf, c_a, c_b, ms)
    logf_s = lf_s[:, :h_b].reshape(n_s, t_s, h_b)

    conv_args = (w_dw[0], row(b_dw[0]), row(ln_conv_g[0]), row(ln_conv_b[0]))
    a_p, cst_p = _conv_module(ag_p.reshape(n_p, t_p, 2 * c_a), jnp.zeros((n_p, HIST, c_a), F32), *conv_args, tm_seq)
    a_s, cst_s = _conv_module(ag_s.reshape(n_s, t_s, 2 * c_a), state_conv[0], *conv_args, t_s)

    b_p = _fox_prompt(qt_p, k_p.reshape(n_p, t_p, c_b), vt_p, _seq_cumsum_columns(lft_p),
                      _tile(t_p, (512, 256, 128)), _tile(t_p, (256, 128)))

    n_pool = cache_fox_k.shape[1]
    pp = _tile(n_pages, (PAGES_PER_STEP, 4, 2, 1))
    keys_minor = lambda a: jnp.moveaxis(a, 2, -1)
    new_t = lambda a, f: _pad_cols(a.reshape(n_s, t_s, f).transpose(0, 2, 1), page)
    cpool = _page_cumsum(keys_minor(cache_fox_logf)[0], _tile(n_pool, (64, 32, 16, 8, 4, 2)))
    cnew = _page_cumsum(new_t(logf_s, h_b), _tile(n_s, (32, 16, 8, 4, 2)))
    b_s = _fox_sample(page_table, _block_diag_queries(q_s.reshape(n_s, t_s, c_b), h_b, h_b),
                      keys_minor(cache_fox_k), keys_minor(cache_fox_v), cpool,
                      new_t(k_s, c_b), new_t(v_s, c_b), cnew, t_s, pp)

    w_oe = w_out_even[0].astype(BF16)
    hp = _outproj([a_p.reshape(mp, c_a), b_p.reshape(mp, c_b)], w_oe, hp, tm_p)
    hs = _outproj([a_s.reshape(ms, c_a), b_s.reshape(ms, c_b)], w_oe, hs, ms)
    hp = moe_layer(hp, 0, _tile(mp, (1024, 512, 256, 128)), False)
    hs = moe_layer(hs, 0, ms, False)

    w_o = w_in_odd[0]
    s4 = nq + 2 * nkv + nqi
    w_o = jnp.concatenate([w_o[:, :s4], _pad_cols(w_o[:, s4:s4 + D_IDX], LANES), _pad_cols(w_o[:, s4 + D_IDX:], LANES)],
                          axis=1).astype(BF16)
    g_o = row(norm_mix_odd[0])
    wi_scale = (H_IDX ** -0.5) * (D_IDX ** -0.5)
    cos_p, sin_p = _rope_tables(jnp.arange(t_p))
    cos_s, sin_s = _rope_tables(jnp.tile(past + jnp.arange(t_s), n_s))
    dqt_p, dk_p, dkt_p, dvt_p, dqit_p, dki_p, dkit_p, dwit_p = _proj_odd(hp, g_o, w_o, cos_p, sin_p, nq, nkv, nqi,
                                                                         wi_scale, tm_seq, t_p)
    dq_s, dk_s, dv_s, dqi_s, dki_s, dwi_s = _proj_odd(hs, g_o, w_o, cos_s, sin_s, nq, nkv, nqi, wi_scale, ms)
    dki_s = dki_s[:, :D_IDX]

    n_sel_p = min(TOPK_MAX, t_p // 4)
    o_p = _dsa_prompt(dqit_p, dwit_p, dki_p.reshape(n_p, t_p, LANES), dqt_p, dk_p.reshape(n_p, t_p, nkv), dvt_p,
                      _tile(t_p, (256, 128)), _tile(t_p, (256, 128)), n_sel_p)

    n_sel_s = min(TOPK_MAX, (past + t_s) // 4)
    heads = lambda a, n, nh: a.reshape(n, -1, nh, HEAD_DIM).transpose(0, 2, 1, 3).astype(BF16)
    qi_rows = heads(dqi_s, n_s, H_IDX).reshape(n_s, H_IDX * t_s, D_IDX)
    wi_rows = dwi_s[:, :H_IDX].reshape(n_s, t_s, H_IDX).transpose(0, 2, 1).reshape(n_s, H_IDX * t_s, 1)
    bias_pages, bias_new = _dsa_sample_scores(page_table, qi_rows, wi_rows, keys_minor(cache_dsa_kidx),
                                              new_t(dki_s, D_IDX), t_s, n_sel_s, pp)
    o_s = _dsa_sample_attn(page_table, _block_diag_queries(dq_s.reshape(n_s, t_s, nq), h_c, kv_c),
                           keys_minor(cache_dsa_k), keys_minor(cache_dsa_v), new_t(dk_s, nkv), new_t(dv_s, nkv),
                           bias_pages, bias_new, t_s, h_c, pp)

    w_oo = w_out_odd[0].astype(BF16)
    hp = _outproj([o_p.reshape(mp, nq)], w_oo, hp, tm_p)
    hs = _outproj([o_s.reshape(ms, nq)], w_oo, hs, ms)
    hp = moe_layer(hp, 1, _tile(mp, (1024, 512, 256, 128)), True)
    hs = moe_layer(hs, 1, ms, True)

    t_minor = lambda a: jnp.moveaxis(a, -1, 1)[None]
    return (hp.reshape(n_p, t_p, d), hs.reshape(n_s, t_s, d),
            cst_p[None], cst_s[None],
            t_minor(kt_p), k_s.reshape(1, n_s, t_s, h_b, HEAD_DIM),
            t_minor(vt_p), v_s.reshape(1, n_s, t_s, h_b, HEAD_DIM),
            t_minor(lft_p), logf_s[None],
            t_minor(dkt_p), dk_s.reshape(1, n_s, t_s, kv_c, HEAD_DIM),
            t_minor(dvt_p), dv_s.reshape(1, n_s, t_s, kv_c, HEAD_DIM),
            t_minor(dkit_p), dki_s.reshape(1, n_s, t_s, D_IDX))
```

```python
import functools

import jax
import jax.numpy as jnp
from jax import lax
from jax.experimental import pallas as pl
from jax.experimental.pallas import tpu as pltpu

F32 = jnp.float32
BF16 = jnp.bfloat16

HEAD_DIM = 64
CONV_W = 31
H_IDX = 8
D_IDX = 64
TOPK_MAX = 256
N_GROUPS = 4
E_PER_GROUP = 4
N_EXPERTS = N_GROUPS * E_PER_GROUP
ROPE_THETA = 10000.0
RMS_EPS = 1e-6
LN_EPS = 1e-5

LANES = 128
SUBLANES = 8
VMEM_LIMIT = 56 * 1024 * 1024

MASK_VALUE = -1e30
HIST = CONV_W - 1
HIST_PAD = 32
BISECT_ITERS = 20
POS_BITS = 14
PAGES_PER_STEP = 16
LOG2E = 1.4426950408889634

_NT = (((1,), (1,)), ((), ()))


def _cparams(sem):
    return pltpu.CompilerParams(dimension_semantics=sem, vmem_limit_bytes=VMEM_LIMIT)


def _rms(x, g):
    ms = jnp.mean(x * x, axis=-1, keepdims=True)
    return x * lax.rsqrt(ms + RMS_EPS) * g


def _rope128(xb, cos, sin_signed):
    lane = lax.broadcasted_iota(jnp.int32, xb.shape, 1)
    first_half = jnp.bitwise_and(lane, HEAD_DIM - 1) < (HEAD_DIM // 2)
    rot = jnp.where(first_half, pltpu.roll(xb, LANES - HEAD_DIM // 2, 1), pltpu.roll(xb, HEAD_DIM // 2, 1))
    return xb * cos + rot * sin_signed


def _split_heads_t(tile_t):
    return tile_t.reshape(2, HEAD_DIM, tile_t.shape[1])


def _zero_padded_pair(tile_t, first_slot):
    top, bot = tile_t[:HEAD_DIM], tile_t[HEAD_DIM:]
    zero = jnp.zeros_like(top)
    place = lambda x, slot: jnp.concatenate([x, zero] if slot == 0 else [zero, x], axis=0).astype(BF16)
    return place(top, first_slot[0]), place(bot, first_slot[1])


def _proj_even_kernel(x_ref, g_ref, w_ref, bf_ref, *out_refs, c_a, c_b, lane_major):
    h = _rms(x_ref[...], g_ref[...]).astype(BF16)

    def seg(a, b):
        return jnp.dot(h, w_ref[:, a:b], preferred_element_type=F32)

    o = 2 * c_a
    z = seg(o + 3 * c_b, o + 3 * c_b + LANES) + bf_ref[...]
    logf = jnp.minimum(z, 0.0) - jnp.log1p(jnp.exp(-jnp.abs(z)))
    scale = HEAD_DIM ** -0.5
    if not lane_major:
        ag_ref, q_ref, k_ref, v_ref, lf_ref = out_refs
        ag_ref[...] = seg(0, o)
        q_ref[...] = seg(o, o + c_b) * scale
        k_ref[...] = seg(o + c_b, o + 2 * c_b)
        v_ref[...] = seg(o + 2 * c_b, o + 3 * c_b)
        lf_ref[...] = logf
        return
    ag_ref, k_ref, qt_ref, kt_ref, vt_ref, lft_ref = out_refs
    ag_ref[...] = seg(0, o)
    for b in range(c_b // LANES):
        cols = lambda base: (base + b * LANES, base + (b + 1) * LANES)
        qa, qb = _zero_padded_pair((seg(*cols(o)) * (scale * LOG2E)).T, (0, 1))
        qt_ref[0, 2 * b] = qa
        qt_ref[0, 2 * b + 1] = qb
        k = seg(*cols(o + c_b))
        k_ref[:, b * LANES:(b + 1) * LANES] = k
        kt_ref[0, 2 * b:2 * b + 2] = _split_heads_t(k.T)
        vt_ref[0, 2 * b:2 * b + 2] = _split_heads_t(seg(*cols(o + 2 * c_b)).T)
    lft_ref[0] = logf.T[:lft_ref.shape[1]]


def _proj_even(x, g, w, bf, c_a, c_b, tm, seq_len=None):
    m, d = x.shape
    nh = c_b // HEAD_DIM
    row = lambda n: pl.BlockSpec((tm, n), lambda i: (i, 0))
    full = lambda a: pl.BlockSpec(a.shape, lambda i: (0, 0))
    if seq_len is None:
        out_specs = [row(2 * c_a), row(c_b), row(c_b), row(c_b), row(LANES)]
        out_shape = [jax.ShapeDtypeStruct((m, n), F32) for n in (2 * c_a, c_b, c_b, c_b, LANES)]
    else:
        nper, n = seq_len // tm, m // seq_len
        lm = lambda rows: pl.BlockSpec((1, nh, rows, tm), lambda i: (i // nper, 0, 0, i % nper))
        out_specs = [row(2 * c_a), row(c_b), lm(2 * HEAD_DIM), lm(HEAD_DIM), lm(HEAD_DIM),
                     pl.BlockSpec((1, nh, tm), lambda i: (i // nper, 0, i % nper))]
        out_shape = [jax.ShapeDtypeStruct((m, 2 * c_a), F32), jax.ShapeDtypeStruct((m, c_b), F32),
                     jax.ShapeDtypeStruct((n, nh, 2 * HEAD_DIM, seq_len), BF16),
                     jax.ShapeDtypeStruct((n, nh, HEAD_DIM, seq_len), F32),
                     jax.ShapeDtypeStruct((n, nh, HEAD_DIM, seq_len), F32),
                     jax.ShapeDtypeStruct((n, nh, seq_len), F32)]
    return pl.pallas_call(
        functools.partial(_proj_even_kernel, c_a=c_a, c_b=c_b, lane_major=seq_len is not None),
        grid=(m // tm,),
        in_specs=[row(d), full(g), full(w), full(bf)],
        out_specs=out_specs, out_shape=out_shape,
        compiler_params=_cparams(("parallel",)),
        name="proj_even",
    )(x, g, w, bf)


def _proj_odd_kernel(x_ref, g_ref, w_ref, cos_ref, sin_ref, *out_refs, nq, nkv, nqi, wi_scale, lane_major):
    h = _rms(x_ref[...], g_ref[...]).astype(BF16)
    cos = cos_ref[...]
    sin = sin_ref[...]
    scale = HEAD_DIM ** -0.5

    def seg(a, b):
        return jnp.dot(h, w_ref[:, a:b], preferred_element_type=F32)

    def roped(a, c):
        return _rope128(seg(a + c * LANES, a + (c + 1) * LANES), cos, sin)

    o_k, o_v, o_qi = nq, nq + nkv, nq + 2 * nkv
    o_ki = o_qi + nqi
    wi = seg(o_ki + LANES, o_ki + 2 * LANES) * wi_scale
    if not lane_major:
        q_ref, k_ref, v_ref, qi_ref, ki_ref, wi_ref = out_refs
        for c in range(nq // LANES):
            q_ref[:, c * LANES:(c + 1) * LANES] = roped(0, c) * scale
        for c in range(nkv // LANES):
            k_ref[:, c * LANES:(c + 1) * LANES] = roped(o_k, c)
        v_ref[...] = seg(o_v, o_v + nkv)
        for c in range(nqi // LANES):
            qi_ref[:, c * LANES:(c + 1) * LANES] = roped(o_qi, c)
        ki_ref[...] = roped(o_ki, 0)
        wi_ref[...] = wi
        return
    qt_ref, k_ref, kt_ref, vt_ref, qit_ref, ki_ref, kit_ref, wit_ref = out_refs
    rep = nq // nkv
    for c in range(nq // LANES):
        slot = ((2 * c) // rep) % 2
        qa, qb = _zero_padded_pair((roped(0, c) * (scale * LOG2E)).T, (slot, slot))
        qt_ref[0, 2 * c] = qa
        qt_ref[0, 2 * c + 1] = qb
    for c in range(nkv // LANES):
        k = roped(o_k, c)
        k_ref[:, c * LANES:(c + 1) * LANES] = k
        kt_ref[0, 2 * c:2 * c + 2] = _split_heads_t(k.T)
        vt_ref[0, 2 * c:2 * c + 2] = _split_heads_t(seg(o_v + c * LANES, o_v + (c + 1) * LANES).T)
    for c in range(nqi // LANES):
        qa, qb = _zero_padded_pair(roped(o_qi, c).T, (0, 0))
        qit_ref[0, 2 * c] = qa
        qit_ref[0, 2 * c + 1] = qb
    ki = roped(o_ki, 0)
    ki_ref[...] = ki
    kit_ref[0] = ki.T[:D_IDX]
    wit_ref[0] = wi.T[:wit_ref.shape[1]]


def _proj_odd(x, g, w, cos, sin, nq, nkv, nqi, wi_scale, tm, seq_len=None):
    m, d = x.shape
    nper = cos.shape[0] // tm
    row = lambda n: pl.BlockSpec((tm, n), lambda i: (i, 0))
    full = lambda a: pl.BlockSpec(a.shape, lambda i: (0, 0))
    tab = pl.BlockSpec((tm, LANES), lambda i: (i % nper, 0))
    if seq_len is None:
        widths = (nq, nkv, nkv, nqi, LANES, LANES)
        out_specs = [row(n) for n in widths]
        out_shape = [jax.ShapeDtypeStruct((m, n), F32) for n in widths]
    else:
        n = m // seq_len
        lm = lambda nh, rows: pl.BlockSpec((1, nh, rows, tm), lambda i: (i // nper, 0, 0, i % nper))
        lm3 = lambda rows: pl.BlockSpec((1, rows, tm), lambda i: (i // nper, 0, i % nper))
        nhq, nhk, nhi = nq // HEAD_DIM, nkv // HEAD_DIM, nqi // D_IDX
        out_specs = [lm(nhq, 2 * HEAD_DIM), row(nkv), lm(nhk, HEAD_DIM), lm(nhk, HEAD_DIM), lm(nhi, 2 * D_IDX),
                     row(LANES), lm3(D_IDX), lm3(H_IDX)]
        out_shape = [jax.ShapeDtypeStruct((n, nhq, 2 * HEAD_DIM, seq_len), BF16), jax.ShapeDtypeStruct((m, nkv), F32),
                     jax.ShapeDtypeStruct((n, nhk, HEAD_DIM, seq_len), F32),
                     jax.ShapeDtypeStruct((n, nhk, HEAD_DIM, seq_len), F32),
                     jax.ShapeDtypeStruct((n, nhi, 2 * D_IDX, seq_len), BF16), jax.ShapeDtypeStruct((m, LANES), F32),
                     jax.ShapeDtypeStruct((n, D_IDX, seq_len), F32), jax.ShapeDtypeStruct((n, H_IDX, seq_len), F32)]
    return pl.pallas_call(
        functools.partial(_proj_odd_kernel, nq=nq, nkv=nkv, nqi=nqi, wi_scale=wi_scale, lane_major=seq_len is not None),
        grid=(m // tm,),
        in_specs=[row(d), full(g), full(w), tab, tab],
        out_specs=out_specs, out_shape=out_shape,
        compiler_params=_cparams(("parallel",)),
        name="proj_odd",
    )(x, g, w, cos, sin)


def _conv_kernel(ag_ref, st_ref, wdw_ref, bdw_ref, lng_ref, lnb_ref, out_ref, nst_ref, buf_ref, sh_ref, *, tt, c_a, rc):
    t = pl.program_id(1)
    lo = HIST_PAD - HIST

    @pl.when(t == 0)
    def _():
        buf_ref[lo:HIST_PAD, :] = st_ref[0]

    @pl.when(t > 0)
    def _():
        buf_ref[lo:HIST_PAD, :] = buf_ref[tt + lo:tt + HIST_PAD, :]

    ag = ag_ref[0]
    buf_ref[HIST_PAD:HIST_PAD + tt, :] = ag[:, :c_a] * jax.nn.sigmoid(ag[:, c_a:])

    rows = tt + HIST_PAD
    for ph in range(1, SUBLANES):
        sh_ref[ph - 1, 0:rows - ph, :] = buf_ref[ph:rows, :]

    bdw = bdw_ref[...]
    lng = lng_ref[...]
    lnb = lnb_ref[...]
    for r in range(tt // rc):
        acc = jnp.zeros((rc, c_a), F32) + bdw
        for j in range(CONV_W):
            s = r * rc + lo + j
            ph, base = s % SUBLANES, s - s % SUBLANES
            win = buf_ref[base:base + rc, :] if ph == 0 else sh_ref[ph - 1, base:base + rc, :]
            acc = acc + wdw_ref[j:j + 1, :] * win
        mu = jnp.mean(acc, axis=-1, keepdims=True)
        cen = acc - mu
        var = jnp.mean(cen * cen, axis=-1, keepdims=True)
        cn = cen * lax.rsqrt(var + LN_EPS) * lng + lnb
        out_ref[0, r * rc:(r + 1) * rc, :] = cn * jax.nn.sigmoid(cn)

    @pl.when(t == pl.num_programs(1) - 1)
    def _():
        nst_ref[0] = buf_ref[tt + lo:tt + HIST_PAD, :]


def _conv_module(ag, state, wdw, bdw, lng, lnb, tt):
    n, t, two_ca = ag.shape
    c_a = two_ca // 2
    rc = min(tt, 64)
    vec = lambda a: pl.BlockSpec(a.shape, lambda i, j: (0, 0))
    return pl.pallas_call(
        functools.partial(_conv_kernel, tt=tt, c_a=c_a, rc=rc),
        grid=(n, t // tt),
        in_specs=[pl.BlockSpec((1, tt, two_ca), lambda i, j: (i, j, 0)),
                  pl.BlockSpec((1, HIST, c_a), lambda i, j: (i, 0, 0)),
                  vec(wdw), vec(bdw), vec(lng), vec(lnb)],
        out_specs=[pl.BlockSpec((1, tt, c_a), lambda i, j: (i, j, 0)),
                   pl.BlockSpec((1, HIST, c_a), lambda i, j: (i, 0, 0))],
        out_shape=[jax.ShapeDtypeStruct((n, t, c_a), F32), jax.ShapeDtypeStruct((n, HIST, c_a), F32)],
        scratch_shapes=[pltpu.VMEM((HIST_PAD + tt, c_a), F32),
                        pltpu.VMEM((SUBLANES - 1, HIST_PAD + tt, c_a), F32)],
        compiler_params=_cparams(("parallel", "arbitrary")),
        name="conv_module",
    )(ag, state, wdw, bdw, lng, lnb)


def _lane_cumsum(x):
    lane = lax.broadcasted_iota(jnp.int32, x.shape, 1)
    s = 1
    while s < LANES:
        x = x + jnp.where(lane >= s, pltpu.roll(x, s, 1), 0.0)
        s *= 2
    return x


def _page_cumsum_kernel(x_ref, o_ref):
    nb, h, _ = x_ref.shape
    o_ref[...] = _lane_cumsum(x_ref[...].reshape(nb * h, LANES)).reshape(nb, h, LANES)


def _page_cumsum(x, nb):
    b, h, _ = x.shape
    spec = pl.BlockSpec((nb, h, LANES), lambda i: (i, 0, 0))
    return pl.pallas_call(
        _page_cumsum_kernel,
        grid=(b // nb,),
        in_specs=[spec], out_specs=spec,
        out_shape=jax.ShapeDtypeStruct(x.shape, F32),
        compiler_params=_cparams(("parallel",)),
        name="page_cumsum",
    )(x)


def _seq_cumsum_kernel(x_ref, o_ref):
    t = x_ref.shape[3]
    nb = t // LANES
    x = x_ref[0, 0]
    loc = _lane_cumsum(jnp.concatenate([x[:, b * LANES:(b + 1) * LANES] for b in range(nb)], axis=0))
    tot = jnp.broadcast_to(loc[:, LANES - 1:LANES], (nb, LANES))
    row = lax.broadcasted_iota(jnp.int32, (nb, LANES), 0)
    upto = tot
    step = 1
    while step < nb:
        upto = upto + jnp.where(row >= step, pltpu.roll(upto, step, 0), 0.0)
        step *= 2
    cur = (loc + (upto - tot)) * LOG2E
    for b in range(nb):
        o_ref[0, 0, b * LANES:(b + 1) * LANES, :] = jnp.broadcast_to(cur[b:b + 1], (LANES, LANES)).T


def _seq_cumsum_columns(x):
    n, h, t = x.shape
    return pl.pallas_call(
        _seq_cumsum_kernel,
        grid=(n, h),
        in_specs=[pl.BlockSpec((1, 1, 1, t), lambda i, j: (i, j, 0, 0))],
        out_specs=pl.BlockSpec((1, 1, t, LANES), lambda i, j: (i, j, 0, 0)),
        out_shape=jax.ShapeDtypeStruct((n, h, t, LANES), F32),
        compiler_params=_cparams(("parallel", "parallel")),
        name="seq_cumsum",
    )(x.reshape(n, h, 1, t))


DENOM_ROWS = 16


def _flash_step_t(s_t, v_t, m_ref, acc_ref, idx):
    m_old = m_ref[idx]
    m_new = jnp.maximum(m_old, jnp.max(s_t, axis=0, keepdims=True))
    alpha = jnp.exp2(m_old - m_new)
    p = jnp.exp2(s_t - m_new).astype(BF16)
    kc = v_t.shape[1]
    ones_row = jnp.where(lax.broadcasted_iota(jnp.int32, (DENOM_ROWS, kc), 0) == 0, 1.0, 0.0).astype(BF16)
    v_aug = jnp.concatenate([v_t.astype(BF16), ones_row], axis=0)
    acc_ref[idx] = acc_ref[idx] * alpha + jnp.dot(v_aug, p, preferred_element_type=F32)
    m_ref[idx] = m_new


def _fox_prompt_kernel(qt_ref, k_ref, vt_ref, c_ref, o_ref, m_ref, acc_ref, sa_ref, sb_ref, *, tq, kc):
    i = pl.program_id(2)
    m_ref[...] = jnp.full(m_ref.shape, MASK_VALUE, F32)
    acc_ref[...] = jnp.zeros(acc_ref.shape, F32)
    q_pos = i * tq + lax.broadcasted_iota(jnp.int32, (kc // 2, tq), 1)
    k_iota = lax.broadcasted_iota(jnp.int32, (kc // 2, tq), 0)

    sub = kc // 2

    def logits(ch, half, s_ref):
        start = pl.multiple_of(ch * kc + half * sub, sub)
        k = k_ref[0, pl.ds(start, sub), :].astype(BF16)
        for hh in range(2):
            s_ref[hh] = jnp.dot(k, qt_ref[0, hh], preferred_element_type=F32)

    def softmax_pv(ch, half, s_ref, masked):
        start = pl.multiple_of(ch * kc + half * sub, sub)
        for hh in range(2):
            s = s_ref[hh] - jnp.tile(c_ref[0, hh, pl.ds(start, sub), :], (1, tq // LANES))
            if masked:
                s = jnp.where(start + k_iota <= q_pos, s, MASK_VALUE)
            _flash_step_t(s, vt_ref[0, hh, :, pl.ds(start, sub)], m_ref, acc_ref, hh)

    def run(lo, hi, masked):
        def body(ch, _):
            logits(ch, 1, sb_ref)
            softmax_pv(ch, 0, sa_ref, masked)
            logits(jnp.minimum(ch + 1, hi - 1), 0, sa_ref)
            softmax_pv(ch, 1, sb_ref, masked)
            return 0

        @pl.when(lo < hi)
        def _():
            logits(lo, 0, sa_ref)
            lax.fori_loop(lo, hi, body, 0)

    n_full = (i * tq) // kc
    run(0, n_full, False)
    run(n_full, n_full + tq // kc, True)
    out_t = jnp.concatenate([acc_ref[hh, :HEAD_DIM, :] / acc_ref[hh, HEAD_DIM:HEAD_DIM + 1, :] for hh in range(2)],
                            axis=0)
    for b in range(tq // LANES):
        o_ref[0, b * LANES:(b + 1) * LANES, :] = out_t[:, b * LANES:(b + 1) * LANES].T


def _fox_prompt(qt, k, vt, c_b, tq, kc):
    n, t, cb = k.shape
    hp = cb // LANES
    vrows = vt.shape[2]
    arows = vrows + DENOM_ROWS
    return pl.pallas_call(
        functools.partial(_fox_prompt_kernel, tq=tq, kc=kc),
        grid=(n, hp, t // tq),
        in_specs=[pl.BlockSpec((1, 2, LANES, tq), lambda b, h, i: (b, h, 0, i)),
                  pl.BlockSpec((1, t, LANES), lambda b, h, i: (b, 0, h)),
                  pl.BlockSpec((1, 2, vrows, t), lambda b, h, i: (b, h, 0, 0)),
                  pl.BlockSpec((1, 2, t, LANES), lambda b, h, i: (b, h, 0, 0))],
        out_specs=pl.BlockSpec((1, tq, LANES), lambda b, h, i: (b, i, h)),
        out_shape=jax.ShapeDtypeStruct((n, t, cb), F32),
        scratch_shapes=[pltpu.VMEM((2, 1, tq), F32), pltpu.VMEM((2, arows, tq), F32),
                        pltpu.VMEM((2, kc // 2, tq), F32), pltpu.VMEM((2, kc // 2, tq), F32)],
        compiler_params=_cparams(("parallel", "parallel", "arbitrary")),
        name="fox_prompt",
    )(qt, k, vt, c_b)


def _init_softmax(m_ref, l_ref, acc_ref):
    m_ref[...] = jnp.full(m_ref.shape, MASK_VALUE, F32)
    l_ref[...] = jnp.zeros(l_ref.shape, F32)
    acc_ref[...] = jnp.zeros(acc_ref.shape, F32)


def _softmax_update(ss, vts, m_ref, l_ref, acc_ref):
    m_old = m_ref[...]
    smax = ss[0]
    for s in ss[1:]:
        smax = jnp.maximum(smax, s)
    m_new = jnp.maximum(m_old, jnp.max(smax, axis=-1, keepdims=True))
    alpha = jnp.exp(m_old - m_new)
    acc = alpha * acc_ref[...]
    psum = None
    for s, vt in zip(ss, vts):
        p = jnp.exp(s - m_new)
        psum = p if psum is None else psum + p
        acc = acc + lax.dot_general(p.astype(BF16), vt, _NT, preferred_element_type=F32)
    l_new = alpha * l_ref[...] + jnp.sum(psum, axis=-1, keepdims=True)
    m_ref[...] = m_new
    l_ref[...] = l_new
    acc_ref[...] = acc


def _paged_specs(pp, n_steps, block, index_tail):
    def spec(p):
        return pl.BlockSpec(block, lambda b, j, pt: index_tail(pt[b, jnp.minimum(j, n_steps - 1) * pp + p]))
    return [spec(p) for p in range(pp)]


def _fox_sample_kernel(pt_ref, wq_ref, *refs, pp, n_steps, tq, nh):
    del pt_ref
    kp, vp, cp = refs[:pp], refs[pp:2 * pp], refs[2 * pp:3 * pp]
    kn_ref, vn_ref, cn_ref, o_ref, m_ref, l_ref, acc_ref, off_ref = refs[3 * pp:]
    j = pl.program_id(1)
    rows, feat = acc_ref.shape
    page = off_ref.shape[1]

    @pl.when(j == 0)
    def _():
        _init_softmax(m_ref, l_ref, acc_ref)
        off_ref[...] = jnp.zeros(off_ref.shape, F32)

    def logits(kt, c):
        s = jnp.dot(wq_ref[0], kt.reshape(feat, page).astype(BF16), preferred_element_type=F32)
        return s - jnp.broadcast_to(c[:, None, :], (nh, tq, page)).reshape(rows, page)

    @pl.when(j < n_steps)
    def _():
        off = off_ref[...]
        ss, vs = [], []
        for p in range(pp):
            cloc = cp[p][0]
            ss.append(logits(kp[p][0, 0], cloc + off))
            vs.append(vp[p][0, 0].reshape(feat, page).astype(BF16))
            off = off + jnp.broadcast_to(cloc[:, page - 1:page], off.shape)
        off_ref[...] = off
        _softmax_update(ss, vs, m_ref, l_ref, acc_ref)

    @pl.when(j == n_steps)
    def _():
        s = logits(kn_ref[0], cn_ref[0] + off_ref[...])
        qi = jnp.bitwise_and(lax.broadcasted_iota(jnp.int32, (rows, page), 0), tq - 1)
        ki = lax.broadcasted_iota(jnp.int32, (rows, page), 1)
        s = jnp.where(ki <= qi, s, MASK_VALUE)
        _softmax_update([s], [vn_ref[0].astype(BF16)], m_ref, l_ref, acc_ref)
        out = acc_ref[...] / l_ref[...]
        for h in range(nh):
            o_ref[0, :, h * HEAD_DIM:(h + 1) * HEAD_DIM] = out[h * tq:(h + 1) * tq, h * HEAD_DIM:(h + 1) * HEAD_DIM]


def _fox_sample(page_table, wq, kcache_t, vcache_t, cpool, knew_t, vnew_t, cnew, tq, pp):
    nb, n_pages = page_table.shape
    _, _, nh, hd, page = kcache_t.shape
    feat = nh * hd
    n_steps = n_pages // pp
    rows = wq.shape[1]
    seq3 = lambda b, j, pt: (b, 0, 0)
    kv_specs = lambda: _paged_specs(pp, n_steps, (1, 1, nh, hd, page), lambda pg: (0, pg, 0, 0, 0))
    grid_spec = pltpu.PrefetchScalarGridSpec(
        num_scalar_prefetch=1,
        grid=(nb, n_steps + 1),
        in_specs=[pl.BlockSpec((1, rows, feat), seq3)] + kv_specs() + kv_specs()
                 + _paged_specs(pp, n_steps, (1, nh, page), lambda pg: (pg, 0, 0))
                 + [pl.BlockSpec((1, feat, page), seq3), pl.BlockSpec((1, feat, page), seq3),
                    pl.BlockSpec((1, nh, page), seq3)],
        out_specs=pl.BlockSpec((1, tq, feat), seq3),
        scratch_shapes=[pltpu.VMEM((rows, 1), F32), pltpu.VMEM((rows, 1), F32), pltpu.VMEM((rows, feat), F32),
                        pltpu.VMEM((nh, page), F32)],
    )
    return pl.pallas_call(
        functools.partial(_fox_sample_kernel, pp=pp, n_steps=n_steps, tq=tq, nh=nh),
        grid_spec=grid_spec,
        out_shape=jax.ShapeDtypeStruct((nb, tq, feat), F32),
        compiler_params=_cparams(("parallel", "arbitrary")),
        name="fox_sample",
    )(page_table, wq, *([kcache_t] * pp), *([vcache_t] * pp), *([cpool] * pp), knew_t, vnew_t, cnew)


def _outproj_kernel(*refs, n_in):
    ins, w_ref, res_ref, o_ref = refs[:n_in], refs[n_in], refs[n_in + 1], refs[n_in + 2]
    acc = res_ref[...]
    off = 0
    for r in ins:
        width = r.shape[1]
        acc = acc + jnp.dot(r[...].astype(BF16), w_ref[off:off + width, :], preferred_element_type=F32)
        off += width
    o_ref[...] = acc


def _outproj(ins, w, res, tm):
    m, d = res.shape
    row = lambda n: pl.BlockSpec((tm, n), lambda i: (i, 0))
    return pl.pallas_call(
        functools.partial(_outproj_kernel, n_in=len(ins)),
        grid=(m // tm,),
        in_specs=[row(a.shape[1]) for a in ins] + [pl.BlockSpec(w.shape, lambda i: (0, 0)), row(d)],
        out_specs=row(d),
        out_shape=jax.ShapeDtypeStruct((m, d), F32),
        compiler_params=_cparams(("parallel",)),
        name="outproj",
    )(*ins, w, res)


def _route(z):
    lane = lax.broadcasted_iota(jnp.int32, z.shape, 1).astype(F32)
    big = jnp.float32(1 << 20)
    neg = -jnp.inf
    rmax = lambda a: jnp.max(a, axis=-1, keepdims=True)
    rmin = lambda a: jnp.min(a, axis=-1, keepdims=True)
    is_g = lane < N_GROUPS
    zg = jnp.where(is_g, z, neg)
    gmax = rmax(zg)
    g_idx = rmin(jnp.where(zg == gmax, lane, big))
    g_w = 1.0 / jnp.sum(jnp.where(is_g, jnp.exp(zg - gmax), 0.0), axis=-1, keepdims=True)
    first = N_GROUPS + g_idx * E_PER_GROUP
    in_grp = jnp.logical_and(lane >= first, lane < first + E_PER_GROUP)
    v1 = jnp.where(in_grp, z, neg)
    top1 = rmax(v1)
    i1 = rmin(jnp.where(v1 == top1, lane, big))
    v2 = jnp.where(lane == i1, neg, v1)
    top2 = rmax(v2)
    i2 = rmin(jnp.where(v2 == top2, lane, big))
    e2 = jnp.exp(top2 - top1)
    den = 1.0 + e2
    return jnp.where(lane == i1, g_w / den, jnp.where(lane == i2, g_w * e2 / den, 0.0)), g_idx.astype(jnp.int32)


def _moe_kernel(x_ref, g_ref, wr_ref, br_ref, wg_ref, wu_ref, wd_ref, gf_ref, o_ref, xn_ref, cmb_ref, acc_ref,
                *, final_norm):
    e = pl.program_id(1)

    @pl.when(e == 0)
    def _():
        xn = _rms(x_ref[...], g_ref[...])
        xn_ref[...] = xn.astype(BF16)
        z = jnp.dot(xn, wr_ref[...], preferred_element_type=F32, precision=lax.Precision.HIGHEST) + br_ref[...]
        cmb_ref[...] = _route(z)[0]
        acc_ref[...] = jnp.zeros(acc_ref.shape, F32)

    xn = xn_ref[...]
    gate = jnp.dot(xn, wg_ref[0], preferred_element_type=F32)
    up = jnp.dot(xn, wu_ref[0], preferred_element_type=F32)
    hid = (gate * jax.nn.sigmoid(gate) * up).astype(BF16)
    y = jnp.dot(hid, wd_ref[0], preferred_element_type=F32)
    lane = lax.broadcasted_iota(jnp.int32, cmb_ref.shape, 1)
    col = jnp.sum(jnp.where(lane == e + N_GROUPS, cmb_ref[...], 0.0), axis=-1, keepdims=True)
    acc_ref[...] = acc_ref[...] + col * y

    @pl.when(e == pl.num_programs(1) - 1)
    def _():
        out = x_ref[...] + acc_ref[...]
        o_ref[...] = _rms(out, gf_ref[...]) if final_norm else out


def _moe(x, g, wr, br, wg, wu, wd, gf, tm, final_norm):
    m, d = x.shape
    ne, _, dff = wg.shape
    row = pl.BlockSpec((tm, d), lambda i, e: (i, 0))
    full = lambda a: pl.BlockSpec(a.shape, lambda i, e: (0, 0))
    return pl.pallas_call(
        functools.partial(_moe_kernel, final_norm=final_norm),
        grid=(m // tm, ne),
        in_specs=[row, full(g), full(wr), full(br),
                  pl.BlockSpec((1, d, dff), lambda i, e: (e, 0, 0)),
                  pl.BlockSpec((1, d, dff), lambda i, e: (e, 0, 0)),
                  pl.BlockSpec((1, dff, d), lambda i, e: (e, 0, 0)),
                  full(gf)],
        out_specs=row,
        out_shape=jax.ShapeDtypeStruct((m, d), F32),
        scratch_shapes=[pltpu.VMEM((tm, d), BF16), pltpu.VMEM((tm, LANES), F32), pltpu.VMEM((tm, d), F32)],
        compiler_params=_cparams(("parallel", "arbitrary")),
        name="moe",
    )(x, g, wr, br, wg, wu, wd, gf)


MOE_WINDOW = 320


def _split3(a):
    hi = a.astype(BF16)
    r1 = a - hi.astype(F32)
    mid = r1.astype(BF16)
    return hi, mid, (r1 - mid.astype(F32)).astype(BF16)


def _moe_sorted_kernel(x_ref, g_ref, wr_ref, br_ref, wg_ref, wu_ref, wd_ref, gf_ref, o_ref,
                       xs_ref, cmbs_ref, acc_ref, pt_ref, rng_ref, *, final_norm, win):
    e = pl.program_id(1)
    tm, d = x_ref.shape

    @pl.when(e == 0)
    def _():
        xn = _rms(x_ref[...], g_ref[...])
        z = jnp.dot(xn, wr_ref[...], preferred_element_type=F32, precision=lax.Precision.HIGHEST) + br_ref[...]
        cmb, g_idx = _route(z)
        lane = lax.broadcasted_iota(jnp.int32, (tm, LANES), 1)
        onehot = jnp.where(lane == g_idx, 1.0, 0.0)
        r_i = lax.broadcasted_iota(jnp.int32, (tm, tm), 0)
        c_i = lax.broadcasted_iota(jnp.int32, (tm, tm), 1)
        rows = lax.broadcasted_iota(jnp.int32, (tm, LANES), 0)
        before = onehot
        step = 1
        while step < tm:
            before = before + jnp.where(rows >= step, pltpu.roll(before, step, 0), 0.0)
            step *= 2
        before = before - onehot
        counts = jnp.sum(onehot, axis=0, keepdims=True)
        starts = _lane_cumsum(counts) - counts
        pos = jnp.sum(jnp.where(lane == g_idx, before + starts, 0.0), axis=-1, keepdims=True)
        pos_row = jnp.broadcast_to(pos, (tm, LANES)).T[0:1].astype(jnp.int32)
        perm = jnp.where(r_i == pos_row, 1.0, 0.0).astype(BF16)
        pt_ref[...] = jnp.where(c_i == pos.astype(jnp.int32), 1.0, 0.0).astype(BF16)
        xs_ref[...] = jnp.dot(perm, xn.astype(BF16), preferred_element_type=F32).astype(BF16)
        cmbs_ref[...] = sum(jnp.dot(perm, t, preferred_element_type=F32) for t in _split3(cmb))
        acc_ref[...] = jnp.zeros(acc_ref.shape, F32)
        lane1 = lax.broadcasted_iota(jnp.int32, (1, LANES), 1)
        for grp in range(N_GROUPS):
            rng_ref[grp] = jnp.sum(jnp.where(lane1 == grp, starts, 0.0)).astype(jnp.int32)
            rng_ref[N_GROUPS + grp] = jnp.sum(jnp.where(lane1 == grp, counts, 0.0)).astype(jnp.int32)

    grp = e // E_PER_GROUP
    start = rng_ref[grp]
    end = start + rng_ref[N_GROUPS + grp]
    w0 = (start // 16) * 16
    lane = lax.broadcasted_iota(jnp.int32, (win, LANES), 1)
    row = lax.broadcasted_iota(jnp.int32, (win, 1), 0)

    def window(i, _):
        lo = w0 + i * win
        ws = pl.multiple_of(jnp.minimum(lo, tm - win), 16)
        xc = xs_ref[pl.ds(ws, win), :]
        gate = jnp.dot(xc, wg_ref[0], preferred_element_type=F32)
        up = jnp.dot(xc, wu_ref[0], preferred_element_type=F32)
        hid = (gate * jax.nn.sigmoid(gate) * up).astype(BF16)
        y = jnp.dot(hid, wd_ref[0], preferred_element_type=F32)
        col = jnp.sum(jnp.where(lane == e + N_GROUPS, cmbs_ref[pl.ds(ws, win), :], 0.0), axis=-1, keepdims=True)
        srt = ws + row
        mine = jnp.logical_and(srt >= jnp.maximum(lo, start), srt < jnp.minimum(lo + win, end))
        acc_ref[pl.ds(ws, win), :] = acc_ref[pl.ds(ws, win), :] + jnp.where(mine, col, 0.0) * y
        return 0

    lax.fori_loop(0, (end - w0 + win - 1) // win, window, 0)

    @pl.when(e == pl.num_programs(1) - 1)
    def _():
        acc = acc_ref[...]
        hi = acc.astype(BF16)
        lo = (acc - hi.astype(F32)).astype(BF16)
        moe = jnp.dot(pt_ref[...], hi, preferred_element_type=F32) + jnp.dot(pt_ref[...], lo, preferred_element_type=F32)
        out = x_ref[...] + moe
        o_ref[...] = _rms(out, gf_ref[...]) if final_norm else out


def _moe_sorted(x, g, wr, br, wg, wu, wd, gf, tm, final_norm):
    m, d = x.shape
    ne, _, dff = wg.shape
    row = pl.BlockSpec((tm, d), lambda i, e: (i, 0))
    full = lambda a: pl.BlockSpec(a.shape, lambda i, e: (0, 0))
    return pl.pallas_call(
        functools.partial(_moe_sorted_kernel, final_norm=final_norm, win=MOE_WINDOW),
        grid=(m // tm, ne),
        in_specs=[row, full(g), full(wr), full(br),
                  pl.BlockSpec((1, d, dff), lambda i, e: (e, 0, 0)),
                  pl.BlockSpec((1, d, dff), lambda i, e: (e, 0, 0)),
                  pl.BlockSpec((1, dff, d), lambda i, e: (e, 0, 0)),
                  full(gf)],
        out_specs=row,
        out_shape=jax.ShapeDtypeStruct((m, d), F32),
        scratch_shapes=[pltpu.VMEM((tm, d), BF16), pltpu.VMEM((tm, LANES), F32), pltpu.VMEM((tm, d), F32),
                        pltpu.VMEM((tm, tm), BF16), pltpu.SMEM((2 * N_GROUPS,), jnp.int32)],
        compiler_params=_cparams(("parallel", "arbitrary")),
        name="moe_sorted",
    )(x, g, wr, br, wg, wu, wd, gf)


def _select_topk(sc_ref, nc, n_sel, n_valid, key_axis):
    _, d0, d1 = sc_ref.shape
    static = isinstance(nc, int)
    chunk_keys = (d0, d1)[key_axis]
    kf = jnp.float32(n_sel)
    inf = jnp.inf
    key_off = lax.broadcasted_iota(jnp.int32, (d0, d1), key_axis)
    stat_shape = (1, d1) if key_axis == 0 else (d0, 1)
    acc_rows = 4 * SUBLANES if d0 % (4 * SUBLANES) == 0 else SUBLANES
    reducers = {"sum": (jnp.add, jnp.sum, 0.0), "min": (jnp.minimum, jnp.min, inf), "max": (jnp.maximum, jnp.max, -inf)}

    def fold(fn, kind):
        comb, red, init = reducers[kind]

        def narrow(a):
            if key_axis == 0:
                return red(a.reshape(d0 // acc_rows, acc_rows, d1), axis=0)
            out = a[:, :LANES]
            for b in range(1, d1 // LANES):
                out = comb(out, a[:, b * LANES:(b + 1) * LANES])
            return out

        if static:
            ch = lax.broadcasted_iota(jnp.int32, (nc, d0, d1), 0)
            part = narrow(red(fn(sc_ref[0:nc], ch), axis=0))
        else:
            part = lax.fori_loop(0, nc, lambda ch, a: comb(a, narrow(fn(sc_ref[ch], ch))),
                                 jnp.full((acc_rows, d1) if key_axis == 0 else (d0, LANES), init, F32))
        return red(part, axis=key_axis, keepdims=True)

    def count(pred):
        return fold(lambda x, ch: jnp.where(pred(x), 1.0, 0.0), "sum")

    def any_query(flag):
        return jnp.max(jnp.where(flag, 1.0, 0.0)) > 0.5

    mx = fold(lambda x, ch: x, "max")
    mn = fold(lambda x, ch: jnp.where(x > -inf, x, inf), "min")
    take_all = n_valid < kf

    def bisect(_, lh):
        lo, hi = lh
        mid = 0.5 * (lo + hi)
        ge = count(lambda x: x >= mid) >= kf
        return jnp.where(ge, mid, lo), jnp.where(ge, hi, mid)

    lo, _ = lax.fori_loop(0, BISECT_ITERS, bisect, (mn, mx))

    cand = fold(lambda x, ch: jnp.where(x >= lo, x, inf), "min")
    n_gt = count(lambda x: x > cand)

    def unsettled(state):
        return any_query(jnp.logical_and(state[1] >= kf, jnp.logical_not(take_all)))

    def walk(state):
        cd, g = state
        nxt = fold(lambda x, ch: jnp.where(x > cd, x, inf), "min")
        cd = jnp.where(g >= kf, nxt, cd)
        return cd, count(lambda x: x > cd)

    cand, n_gt = lax.while_loop(unsettled, walk, (cand, n_gt))
    thr = jnp.where(take_all, -inf, cand)
    n_gt = jnp.where(take_all, n_valid, n_gt)
    need = kf - n_gt
    n_eq = count(lambda x: x == thr)
    has_excess = any_query(jnp.logical_and(n_eq > need, jnp.logical_not(take_all)))

    def tie_cut():
        def bit_step(b, ans):
            cnd = ans + jnp.left_shift(jnp.int32(1), POS_BITS - 1 - b)
            in_front = lambda x, ch: jnp.where(x == thr, jnp.where((ch * chunk_keys + key_off) < cnd, 1.0, 0.0), 0.0)
            return jnp.where(fold(in_front, "sum") < need, cnd, ans)
        return lax.fori_loop(0, POS_BITS, bit_step, jnp.zeros(stat_shape, jnp.int32))

    cut = lax.cond(has_excess, tie_cut, lambda: jnp.full(stat_shape, (1 << POS_BITS) - 1, jnp.int32))

    def bias_of(x, ch):
        tie = jnp.where(x == thr, jnp.where((ch * chunk_keys + key_off) <= cut, 0.0, MASK_VALUE), MASK_VALUE)
        return jnp.where(x > -inf, jnp.where(x > thr, 0.0, tie), MASK_VALUE)

    if static:
        sc_ref[0:nc] = bias_of(sc_ref[0:nc], lax.broadcasted_iota(jnp.int32, (nc, d0, d1), 0))
    else:
        def write(ch, _):
            sc_ref[ch] = bias_of(sc_ref[ch], ch)
            return 0
        lax.fori_loop(0, nc, write, 0)


def _dsa_prompt_kernel(qit_ref, wit_ref, kidx_ref, qt_ref, k_ref, vt_ref, o_ref, sc_ref, m_ref, acc_ref, sa_ref, sb_ref,
                       *, tq, kc, n_sel, rep):
    i = pl.program_id(1)
    nc = ((i + 1) * tq + kc - 1) // kc
    nh = qt_ref.shape[1]
    nkv = vt_ref.shape[1]
    q_pos = i * tq + lax.broadcasted_iota(jnp.int32, (kc, tq), 1)
    k_iota = lax.broadcasted_iota(jnp.int32, (kc, tq), 0)
    qi_all = jnp.concatenate([qit_ref[0, h] for h in range(H_IDX)], axis=1)
    wi_all = jnp.concatenate([wit_ref[0, h:h + 1, :] for h in range(H_IDX)], axis=1)

    def score_chunk(ch, _):
        start = pl.multiple_of(ch * kc, kc)
        d = jnp.dot(kidx_ref[0, pl.ds(start, kc), :].astype(BF16), qi_all, preferred_element_type=F32)
        r = jnp.maximum(d, 0.0) * wi_all
        acc = r[:, :tq]
        for h in range(1, H_IDX):
            acc = acc + r[:, h * tq:(h + 1) * tq]
        sc_ref[ch] = jnp.where(start + k_iota <= q_pos, acc, -jnp.inf)
        return 0

    lax.fori_loop(0, nc, score_chunk, 0)
    n_valid = (i * tq + 1 + lax.broadcasted_iota(jnp.int32, (1, tq), 1)).astype(F32)
    _select_topk(sc_ref, nc, n_sel, n_valid, 0)

    m_ref[...] = jnp.full(m_ref.shape, MASK_VALUE, F32)
    acc_ref[...] = jnp.zeros(acc_ref.shape, F32)

    sub = kc // 2
    q_groups = lambda g: jnp.concatenate([qt_ref[0, g * rep + r] for r in range(rep)], axis=1)

    def logits(ch, half, s_ref):
        start = pl.multiple_of(ch * kc + half * sub, sub)
        for g in range(nkv):
            kblk = k_ref[0, pl.ds(start, sub), (g // 2) * LANES:(g // 2 + 1) * LANES].astype(BF16)
            s_ref[g] = jnp.dot(kblk, q_groups(g), preferred_element_type=F32)

    def softmax_pv(ch, half, s_ref):
        start = pl.multiple_of(ch * kc + half * sub, sub)
        bias = jnp.tile(sc_ref[ch, half * sub:(half + 1) * sub, :], (1, rep))
        for g in range(nkv):
            _flash_step_t(s_ref[g] + bias, vt_ref[0, g, :, pl.ds(start, sub)], m_ref, acc_ref, g)

    logits(0, 0, sa_ref)

    def attend(ch, _):
        logits(ch, 1, sb_ref)
        softmax_pv(ch, 0, sa_ref)
        logits(jnp.minimum(ch + 1, nc - 1), 0, sa_ref)
        softmax_pv(ch, 1, sb_ref)
        return 0

    lax.fori_loop(0, nc, attend, 0)
    for h in range(0, nh, 2):
        pair = []
        for hh in (h, h + 1):
            g, r = hh // rep, hh % rep
            cols = slice(r * tq, (r + 1) * tq)
            pair.append(acc_ref[g, :HEAD_DIM, cols] / acc_ref[g, HEAD_DIM:HEAD_DIM + 1, cols])
        o_ref[0, :, h * HEAD_DIM:(h + 2) * HEAD_DIM] = jnp.concatenate(pair, axis=0).T


def _dsa_prompt(qit, wit, kidx, qt, k, vt, tq, kc, n_sel):
    n, nh, _, t = qt.shape
    nkv = vt.shape[1]
    rep = nh // nkv
    vrows = vt.shape[2]
    arows = vrows + DENOM_ROWS
    return pl.pallas_call(
        functools.partial(_dsa_prompt_kernel, tq=tq, kc=kc, n_sel=n_sel, rep=rep),
        grid=(n, t // tq),
        in_specs=[pl.BlockSpec((1, H_IDX, 2 * D_IDX, tq), lambda b, i: (b, 0, 0, i)),
                  pl.BlockSpec((1, H_IDX, tq), lambda b, i: (b, 0, i)),
                  pl.BlockSpec((1, t, 2 * D_IDX), lambda b, i: (b, 0, 0)),
                  pl.BlockSpec((1, nh, 2 * HEAD_DIM, tq), lambda b, i: (b, 0, 0, i)),
                  pl.BlockSpec((1, t, nkv * HEAD_DIM), lambda b, i: (b, 0, 0)),
                  pl.BlockSpec((1, nkv, vrows, t), lambda b, i: (b, 0, 0, 0))],
        out_specs=pl.BlockSpec((1, tq, nh * HEAD_DIM), lambda b, i: (b, i, 0)),
        out_shape=jax.ShapeDtypeStruct((n, t, nh * HEAD_DIM), F32),
        scratch_shapes=[pltpu.VMEM((t // kc, kc, tq), F32),
                        pltpu.VMEM((nkv, 1, rep * tq), F32), pltpu.VMEM((nkv, arows, rep * tq), F32),
                        pltpu.VMEM((nkv, kc // 2, rep * tq), F32), pltpu.VMEM((nkv, kc // 2, rep * tq), F32)],
        compiler_params=_cparams(("parallel", "arbitrary")),
        name="dsa_prompt",
    )(qit, wit, kidx, qt, k, vt)


def _dsa_sample_score_kernel(pt_ref, qi_ref, wi_ref, *refs, pp, n_steps, page, tq, n_sel):
    del pt_ref
    kp = refs[:pp]
    kn_ref, bp_ref, bn_ref, sc_ref = refs[pp:]
    j = pl.program_id(1)
    n_pages = n_steps * pp
    q_off = lax.broadcasted_iota(jnp.int32, (tq, page), 0)
    k_off = lax.broadcasted_iota(jnp.int32, (tq, page), 1)

    def score(kpage_t, pg, new):
        d = jnp.dot(qi_ref[0], kpage_t.astype(BF16), preferred_element_type=F32)
        r = (jnp.maximum(d, 0.0) * wi_ref[0]).reshape(H_IDX, tq, page)
        acc = r[0]
        for h in range(1, H_IDX):
            acc = acc + r[h]
        sc_ref[pg] = jnp.where(k_off <= q_off, acc, -jnp.inf) if new else acc

    @pl.when(j < n_steps)
    def _():
        for p in range(pp):
            score(kp[p][0, 0], j * pp + p, False)

    @pl.when(j == n_steps)
    def _():
        score(kn_ref[0], n_pages, True)
        n_valid = (n_pages * page + 1 + lax.broadcasted_iota(jnp.int32, (tq, 1), 0)).astype(F32)
        _select_topk(sc_ref, n_pages + 1, n_sel, n_valid, 1)
        bp_ref[0] = sc_ref[0:n_pages]
        bn_ref[0] = sc_ref[n_pages]


def _dsa_sample_scores(page_table, qi, wi, kidx_cache_t, kidx_new_t, tq, n_sel, pp):
    nb, n_pages = page_table.shape
    page = kidx_cache_t.shape[3]
    n_steps = n_pages // pp
    rows = qi.shape[1]
    seq3 = lambda b, j, pt: (b, 0, 0)
    grid_spec = pltpu.PrefetchScalarGridSpec(
        num_scalar_prefetch=1,
        grid=(nb, n_steps + 1),
        in_specs=[pl.BlockSpec((1, rows, D_IDX), seq3), pl.BlockSpec((1, rows, 1), seq3)]
                 + _paged_specs(pp, n_steps, (1, 1, D_IDX, page), lambda pg: (0, pg, 0, 0))
                 + [pl.BlockSpec((1, D_IDX, page), seq3)],
        out_specs=[pl.BlockSpec((1, n_pages, tq, page), lambda b, j, pt: (b, 0, 0, 0)),
                   pl.BlockSpec((1, tq, page), seq3)],
        scratch_shapes=[pltpu.VMEM((n_pages + 1, tq, page), F32)],
    )
    return pl.pallas_call(
        functools.partial(_dsa_sample_score_kernel, pp=pp, n_steps=n_steps, page=page, tq=tq, n_sel=n_sel),
        grid_spec=grid_spec,
        out_shape=[jax.ShapeDtypeStruct((nb, n_pages, tq, page), F32), jax.ShapeDtypeStruct((nb, tq, page), F32)],
        compiler_params=_cparams(("parallel", "arbitrary")),
        name="dsa_sample_scores",
    )(page_table, qi, wi, *([kidx_cache_t] * pp), kidx_new_t)


def _dsa_sample_attn_kernel(pt_ref, wq_ref, *refs, pp, n_steps, tq, nh, rep):
    del pt_ref
    kp, vp = refs[:pp], refs[pp:2 * pp]
    kn_ref, vn_ref, bp_ref, bn_ref, o_ref, m_ref, l_ref, acc_ref = refs[2 * pp:]
    j = pl.program_id(1)

    @pl.when(j == 0)
    def _():
        _init_softmax(m_ref, l_ref, acc_ref)

    ckv = acc_ref.shape[1]

    def logits(kt, bias):
        s = jnp.dot(wq_ref[0], kt.reshape(ckv, -1).astype(BF16), preferred_element_type=F32)
        return s + jnp.tile(bias, (nh, 1))

    @pl.when(j < n_steps)
    def _():
        ss = [logits(kp[p][0, 0], bp_ref[0, p]) for p in range(pp)]
        vts = [vp[p][0, 0].reshape(ckv, -1).astype(BF16) for p in range(pp)]
        _softmax_update(ss, vts, m_ref, l_ref, acc_ref)

    @pl.when(j == n_steps)
    def _():
        _softmax_update([logits(kn_ref[0], bn_ref[0])], [vn_ref[0].astype(BF16)], m_ref, l_ref, acc_ref)
        out = acc_ref[...] / l_ref[...]
        for h in range(nh):
            g = h // rep
            o_ref[0, :, h * HEAD_DIM:(h + 1) * HEAD_DIM] = out[h * tq:(h + 1) * tq, g * HEAD_DIM:(g + 1) * HEAD_DIM]


def _dsa_sample_attn(page_table, wq, kcache_t, vcache_t, knew_t, vnew_t, bias_pages, bias_new, tq, nh, pp):
    nb, n_pages = page_table.shape
    _, _, nkv, hd, page = kcache_t.shape
    ckv = nkv * hd
    n_steps = n_pages // pp
    rows = wq.shape[1]
    seq3 = lambda b, j, pt: (b, 0, 0)
    kv_specs = lambda: _paged_specs(pp, n_steps, (1, 1, nkv, hd, page), lambda pg: (0, pg, 0, 0, 0))
    grid_spec = pltpu.PrefetchScalarGridSpec(
        num_scalar_prefetch=1,
        grid=(nb, n_steps + 1),
        in_specs=[pl.BlockSpec((1, rows, ckv), seq3)] + kv_specs() + kv_specs()
                 + [pl.BlockSpec((1, ckv, page), seq3), pl.BlockSpec((1, ckv, page), seq3),
                    pl.BlockSpec((1, pp, tq, page), lambda b, j, pt: (b, jnp.minimum(j, n_steps - 1), 0, 0)),
                    pl.BlockSpec((1, tq, page), seq3)],
        out_specs=pl.BlockSpec((1, tq, nh * HEAD_DIM), seq3),
        scratch_shapes=[pltpu.VMEM((rows, 1), F32), pltpu.VMEM((rows, 1), F32), pltpu.VMEM((rows, ckv), F32)],
    )
    return pl.pallas_call(
        functools.partial(_dsa_sample_attn_kernel, pp=pp, n_steps=n_steps, tq=tq, nh=nh, rep=nh * HEAD_DIM // ckv),
        grid_spec=grid_spec,
        out_shape=jax.ShapeDtypeStruct((nb, tq, nh * HEAD_DIM), F32),
        compiler_params=_cparams(("parallel", "arbitrary")),
        name="dsa_sample_attn",
    )(page_table, wq, *([kcache_t] * pp), *([vcache_t] * pp), knew_t, vnew_t, bias_pages, bias_new)


def _pad_cols(a, n):
    return jnp.pad(a, [(0, 0)] * (a.ndim - 1) + [(0, n - a.shape[-1])])


def _rope_tables(pos):
    half = HEAD_DIM // 2
    inv = ROPE_THETA ** (-jnp.arange(half, dtype=F32) / half)
    ang = pos.astype(F32)[:, None] * inv[None, :]
    cos, sin = jnp.cos(ang), jnp.sin(ang)
    return jnp.tile(cos, (1, LANES // half)), jnp.tile(jnp.concatenate([-sin, sin], axis=1), (1, LANES // HEAD_DIM))


def _block_diag_queries(q, n_heads, n_kv):
    b, t, _ = q.shape
    qh = q.reshape(b, t, n_heads, HEAD_DIM).transpose(0, 2, 1, 3)
    onehot = jax.nn.one_hot(jnp.arange(n_heads) // (n_heads // n_kv), n_kv, dtype=q.dtype)
    w = qh[:, :, :, None, :] * onehot[None, :, None, :, None]
    return w.reshape(b, n_heads * t, n_kv * HEAD_DIM).astype(BF16)


def _tile(m, pref):
    for t in pref:
        if m % t == 0:
            return t
    return m


def kernel(x_prompt, x_sample, state_conv, cache_fox_k, cache_fox_v, cache_fox_logf, cache_dsa_k, cache_dsa_v,
           cache_dsa_kidx, page_table, norm_mix_even, w_in_even, b_forget, w_dw, b_dw, ln_conv_g, ln_conv_b,
           w_out_even, norm_mix_odd, w_in_odd, w_out_odd, norm_ffn, w_group, b_group, w_router, b_router, w_gate,
           w_up, w_down, norm_final):
    n_p, t_p, d = x_prompt.shape
    n_s, t_s, _ = x_sample.shape
    depth = norm_ffn.shape[0]
    page = cache_fox_k.shape[2]
    n_pages = page_table.shape[1]
    past = n_pages * page
    c_a = w_dw.shape[2]
    h_b = b_forget.shape[1]
    c_b = h_b * HEAD_DIM
    kv_c = cache_dsa_k.shape[3]
    h_c = w_out_odd.shape[1] // HEAD_DIM
    nq, nkv, nqi = h_c * HEAD_DIM, kv_c * HEAD_DIM, H_IDX * D_IDX
    assert t_s == SUBLANES and depth == 2 and w_in_even.shape[0] == 1 and w_in_odd.shape[0] == 1

    mp, ms = n_p * t_p, n_s * t_s
    hp = x_prompt.reshape(mp, d)
    hs = x_sample.reshape(ms, d)
    tm_p = _tile(mp, (512, 256, 128))
    row = lambda a: a.reshape(1, -1)

    def moe_layer(h, layer, tm, final):
        wr = _pad_cols(jnp.concatenate([w_group[layer], w_router[layer]], axis=1), LANES)
        br = _pad_cols(jnp.concatenate([b_group[layer], b_router[layer]]).reshape(1, -1), LANES)
        sort_tokens = tm >= 2 * MOE_WINDOW and (tm - MOE_WINDOW) % 16 == 0
        return (_moe_sorted if sort_tokens else _moe)(
            h, row(norm_ffn[layer]), wr, br, w_gate[layer].astype(BF16), w_up[layer].astype(BF16),
            w_down[layer].astype(BF16), row(norm_final), tm, final)

    tm_seq = _tile(t_p, (512, 256, 128))
    w_e = w_in_even[0]
    w_e = jnp.concatenate([w_e[:, :2 * c_a + 3 * c_b], _pad_cols(w_e[:, 2 * c_a + 3 * c_b:], LANES)], axis=1).astype(BF16)
    bf = _pad_cols(b_forget[0].reshape(1, -1), LANES)
    g_e = row(norm_mix_even[0])
    ag_p, k_p, qt_p, kt_p, vt_p, lft_p = _proj_even(hp, g_e, w_e, bf, c_a, c_b, tm_seq, t_p)
    ag_s, q_s, k_s, v_s, lf_s = _proj_even(hs, g_e, w_e, bf, c_a, c_b, ms)
    logf_s = lf_s[:, :h_b].reshape(n_s, t_s, h_b)

    conv_args = (w_dw[0], row(b_dw[0]), row(ln_conv_g[0]), row(ln_conv_b[0]))
    a_p, cst_p = _conv_module(ag_p.reshape(n_p, t_p, 2 * c_a), jnp.zeros((n_p, HIST, c_a), F32), *conv_args, tm_seq)
    a_s, cst_s = _conv_module(ag_s.reshape(n_s, t_s, 2 * c_a), state_conv[0], *conv_args, t_s)

    b_p = _fox_prompt(qt_p, k_p.reshape(n_p, t_p, c_b), vt_p, _seq_cumsum_columns(lft_p),
                      _tile(t_p, (512, 256, 128)), _tile(t_p, (256, 128)))

    n_pool = cache_fox_k.shape[1]
    pp = _tile(n_pages, (PAGES_PER_STEP, 4, 2, 1))
    keys_minor = lambda a: jnp.moveaxis(a, 2, -1)
    new_t = lambda a, f: _pad_cols(a.reshape(n_s, t_s, f).transpose(0, 2, 1), page)
    cpool = _page_cumsum(keys_minor(cache_fox_logf)[0], _tile(n_pool, (64, 32, 16, 8, 4, 2)))
    cnew = _page_cumsum(new_t(logf_s, h_b), _tile(n_s, (32, 16, 8, 4, 2)))
    b_s = _fox_sample(page_table, _block_diag_queries(q_s.reshape(n_s, t_s, c_b), h_b, h_b),
                      keys_minor(cache_fox_k), keys_minor(cache_fox_v), cpool,
                      new_t(k_s, c_b), new_t(v_s, c_b), cnew, t_s, pp)

    w_oe = w_out_even[0].astype(BF16)
    hp = _outproj([a_p.reshape(mp, c_a), b_p.reshape(mp, c_b)], w_oe, hp, tm_p)
    hs = _outproj([a_s.reshape(ms, c_a), b_s.reshape(ms, c_b)], w_oe, hs, ms)
    hp = moe_layer(hp, 0, _tile(mp, (1024, 512, 256, 128)), False)
    hs = moe_layer(hs, 0, ms, False)

    w_o = w_in_odd[0]
    s4 = nq + 2 * nkv + nqi
    w_o = jnp.concatenate([w_o[:, :s4], _pad_cols(w_o[:, s4:s4 + D_IDX], LANES), _pad_cols(w_o[:, s4 + D_IDX:], LANES)],
                          axis=1).astype(BF16)
    g_o = row(norm_mix_odd[0])
    wi_scale = (H_IDX ** -0.5) * (D_IDX ** -0.5)
    cos_p, sin_p = _rope_tables(jnp.arange(t_p))
    cos_s, sin_s = _rope_tables(jnp.tile(past + jnp.arange(t_s), n_s))
    dqt_p, dk_p, dkt_p, dvt_p, dqit_p, dki_p, dkit_p, dwit_p = _proj_odd(hp, g_o, w_o, cos_p, sin_p, nq, nkv, nqi,
                                                                         wi_scale, tm_seq, t_p)
    dq_s, dk_s, dv_s, dqi_s, dki_s, dwi_s = _proj_odd(hs, g_o, w_o, cos_s, sin_s, nq, nkv, nqi, wi_scale, ms)
    dki_s = dki_s[:, :D_IDX]

    n_sel_p = min(TOPK_MAX, t_p // 4)
    o_p = _dsa_prompt(dqit_p, dwit_p, dki_p.reshape(n_p, t_p, LANES), dqt_p, dk_p.reshape(n_p, t_p, nkv), dvt_p,
                      _tile(t_p, (256, 128)), _tile(t_p, (256, 128)), n_sel_p)

    n_sel_s = min(TOPK_MAX, (past + t_s) // 4)
    heads = lambda a, n, nh: a.reshape(n, -1, nh, HEAD_DIM).transpose(0, 2, 1, 3).astype(BF16)
    qi_rows = heads(dqi_s, n_s, H_IDX).reshape(n_s, H_IDX * t_s, D_IDX)
    wi_rows = dwi_s[:, :H_IDX].reshape(n_s, t_s, H_IDX).transpose(0, 2, 1).reshape(n_s, H_IDX * t_s, 1)
    bias_pages, bias_new = _dsa_sample_scores(page_table, qi_rows, wi_rows, keys_minor(cache_dsa_kidx),
                                              new_t(dki_s, D_IDX), t_s, n_sel_s, pp)
    o_s = _dsa_sample_attn(page_table, _block_diag_queries(dq_s.reshape(n_s, t_s, nq), h_c, kv_c),
                           keys_minor(cache_dsa_k), keys_minor(cache_dsa_v), new_t(dk_s, nkv), new_t(dv_s, nkv),
                           bias_pages, bias_new, t_s, h_c, pp)

    w_oo = w_out_odd[0].astype(BF16)
    hp = _outproj([o_p.reshape(mp, nq)], w_oo, hp, tm_p)
    hs = _outproj([o_s.reshape(ms, nq)], w_oo, hs, ms)
    hp = moe_layer(hp, 1, _tile(mp, (1024, 512, 256, 128)), True)
    hs = moe_layer(hs, 1, ms, True)

    t_minor = lambda a: jnp.moveaxis(a, -1, 1)[None]
    return (hp.reshape(n_p, t_p, d), hs.reshape(n_s, t_s, d),
            cst_p[None], cst_s[None],
            t_minor(kt_p), k_s.reshape(1, n_s, t_s, h_b, HEAD_DIM),
            t_minor(vt_p), v_s.reshape(1, n_s, t_s, h_b, HEAD_DIM),
            t_minor(lft_p), logf_s[None],
            t_minor(dkt_p), dk_s.reshape(1, n_s, t_s, kv_c, HEAD_DIM),
            t_minor(dvt_p), dv_s.reshape(1, n_s, t_s, kv_c, HEAD_DIM),
            t_minor(dkit_p), dki_s.reshape(1, n_s, t_s, D_IDX))
```
